```python
import jax, jax.numpy as jnp
from jax import lax
import numpy as np

D_MODEL = 1024
BATCH = 8
SEQ = 2048
DEPTH = 2
DEC_BATCH = 32
DEC_SEQ = 16
PAST_LEN = 4096

CHUNK = 64
Q_BLOCK = 128
HEAD_DIM = 64
H_SB = 6
H_FOX = 6
H_DSA = 6
MIX_W = 6 * 64
IDX_HEADS = 4
IDX_DIM = 64
TOPK_MAX = 256
N_BRANCH = 3
D_FF = 2816
ROPE_THETA = 10000.0
EPS = 1e-6
FFN_RES = 0.5
N_MOD = 9
N_NORM = 6
SPLIT_SIZES = (3 * MIX_W, 3 * MIX_W, H_FOX, 3 * MIX_W, IDX_HEADS * IDX_DIM, IDX_DIM, IDX_HEADS, N_BRANCH * D_MODEL)
N_IN = 9 * MIX_W + H_FOX + IDX_HEADS * IDX_DIM + IDX_DIM + IDX_HEADS + N_BRANCH * D_MODEL

kernel_name = 'hybrid_streaming_sb_fox_dsa_step'


def rmsnorm(x, g):
    xf = x.astype(jnp.float32)
    y = xf * lax.rsqrt(jnp.mean(xf * xf, axis=-1, keepdims=True) + EPS)
    return (y * g.astype(jnp.float32)).astype(x.dtype)


def rotary(x, pos):
    half = x.shape[-1] // 2
    inv_freq = ROPE_THETA ** (-jnp.arange(half, dtype=jnp.float32) / half)
    ang = pos.astype(jnp.float32)[:, None] * inv_freq[None, :]
    cos = jnp.cos(ang)[None, :, None, :]
    sin = jnp.sin(ang)[None, :, None, :]
    xf = x.astype(jnp.float32)
    x1, x2 = xf[..., :half], xf[..., half:]
    return jnp.concatenate([x1 * cos - x2 * sin, x1 * sin + x2 * cos], axis=-1).astype(x.dtype)


def block_map(fn, q_args, q_pos):
    tq = q_pos.shape[0]
    qb = min(Q_BLOCK, tq)
    nb = tq // qb

    def split(a):
        return jnp.moveaxis(a.reshape(a.shape[0], nb, qb, *a.shape[2:]), 1, 0)

    xs = tuple(split(a) for a in q_args) + (q_pos.reshape(nb, qb),)
    out = lax.map(lambda args: fn(*args), xs)
    out = jnp.moveaxis(out, 0, 1)
    return out.reshape(out.shape[0], tq, *out.shape[3:])


def stick_breaking_attention(q, k, v, q_pos, k_pos):
    scale = HEAD_DIM ** -0.5

    def block(qb_, qp):
        z = jnp.einsum('bqhd,bkhd->bhqk', qb_, k).astype(jnp.float32) * scale
        mask = k_pos[None, :] < qp[:, None]
        log_keep = jnp.where(mask, jax.nn.log_sigmoid(-z), 0.0)
        suffix = lax.cumsum(log_keep, axis=3, reverse=True) - log_keep
        a = jnp.where(mask, jnp.exp(jax.nn.log_sigmoid(z) + suffix), 0.0)
        return jnp.einsum('bhqk,bkhd->bqhd', a.astype(v.dtype), v)

    return block_map(block, (q,), q_pos)


def forgetting_attention(q, k, v, g_q, g_k, q_pos, k_pos):
    scale = HEAD_DIM ** -0.5
    g_kt = jnp.swapaxes(g_k, 1, 2)

    def block(qb_, gq, qp):
        s = jnp.einsum('bqhd,bkhd->bhqk', qb_, k).astype(jnp.float32) * scale
        s = s + g_kt[:, :, None, :] - jnp.swapaxes(gq, 1, 2)[:, :, :, None]
        mask = k_pos[None, :] <= qp[:, None]
        p = jax.nn.softmax(jnp.where(mask, s, -jnp.inf), axis=-1)
        return jnp.einsum('bhqk,bkhd->bqhd', p.astype(v.dtype), v)

    return block_map(block, (q, g_q), q_pos)


def dsa_attention(q, k, v, q_idx, w_idx, k_idx, q_pos, k_pos):
    scale = HEAD_DIM ** -0.5
    topk = min(TOPK_MAX, k.shape[1] // 4)

    def block(qb_, qib, wib, qp):
        isc = jnp.einsum('bqhd,bkd->bqhk', qib, k_idx).astype(jnp.float32) * (IDX_DIM ** -0.5)
        isc = jnp.einsum('bqhk,bqh->bqk', jax.nn.relu(isc), wib.astype(jnp.float32)) * (IDX_HEADS ** -0.5)
        allowed = (k_pos[None, :] // CHUNK) <= (qp[:, None] // CHUNK)
        isc = jnp.where(allowed[None], isc, -jnp.inf)
        vals, idx = lax.top_k(isc, topk)
        valid = vals > -jnp.inf
        k_sel = jax.vmap(lambda kb, ib: kb[ib])(k, idx)
        v_sel = jax.vmap(lambda vb, ib: vb[ib])(v, idx)
        s = jnp.einsum('bqhd,bqkhd->bhqk', qb_, k_sel).astype(jnp.float32) * scale
        p = jax.nn.softmax(jnp.where(valid[:, None], s, -jnp.inf), axis=-1)
        return jnp.einsum('bhqk,bqkhd->bqhd', p.astype(v.dtype), v_sel)

    return block_map(block, (q, q_idx, w_idx), q_pos)


def swiglu(h, w_up, w_down):
    g, u = jnp.split(h @ w_up, 2, axis=-1)
    return (jax.nn.silu(g) * u) @ w_down


def token_mixing(h, pos, past, w_in, b_forget, w_branch, w_out):
    b, t, _ = h.shape
    offs = np.cumsum(SPLIT_SIZES)[:-1].tolist()
    qkv_sb, qkv_fox, f_logit, qkv_dsa, q_idx, k_idx, w_idx, gate_logit = jnp.split(h @ w_in, offs, axis=-1)
    q_sb, k_sb, v_sb = (a.reshape(b, t, H_SB, HEAD_DIM) for a in jnp.split(qkv_sb, 3, axis=-1))
    q_fox, k_fox, v_fox = (a.reshape(b, t, H_FOX, HEAD_DIM) for a in jnp.split(qkv_fox, 3, axis=-1))
    q_dsa, k_dsa, v_dsa = (a.reshape(b, t, H_DSA, HEAD_DIM) for a in jnp.split(qkv_dsa, 3, axis=-1))
    log_f = jax.nn.log_sigmoid(f_logit + b_forget)
    q_dsa = rotary(q_dsa, pos)
    k_dsa = rotary(k_dsa, pos)
    q_idx = rotary(q_idx.reshape(b, t, IDX_HEADS, IDX_DIM), pos)
    k_idx = rotary(k_idx[:, :, None, :], pos)[:, :, 0, :]
    new = (k_sb, v_sb, k_fox, v_fox, log_f, k_dsa, v_dsa, k_idx)
    if past is None:
        full = new
        k_pos = pos
    else:
        full = tuple(jnp.concatenate([p_, n_], axis=1) for p_, n_ in zip(past, new))
        k_pos = jnp.concatenate([jnp.arange(past[0].shape[1], dtype=jnp.int32), pos])
    ks_sb, vs_sb, ks_fox, vs_fox, lf_all, ks_dsa, vs_dsa, ks_idx = full
    lf = lf_all.astype(jnp.float32)
    g_all = lax.cumsum(lf, axis=1, reverse=True) - lf
    o_sb = stick_breaking_attention(q_sb, ks_sb, vs_sb, pos, k_pos)
    o_fox = forgetting_attention(q_fox, ks_fox, vs_fox, g_all[:, -t:], g_all, pos, k_pos)
    o_dsa = dsa_attention(q_dsa, ks_dsa, vs_dsa, q_idx, w_idx, ks_idx, pos, k_pos)
    y = jnp.stack([o.reshape(b, t, MIX_W) for o in (o_sb, o_fox, o_dsa)], axis=2)
    y = jnp.einsum('btnc,ncd->btnd', y, w_branch)
    gates = jax.nn.sigmoid(gate_logit.reshape(b, t, N_BRANCH, D_MODEL))
    merged = jnp.sum(gates * y, axis=2)
    return merged @ w_out, new


def trunk(x, c, pos, caches, norm_g, w_ada, b_ada, ffn_w_in, ffn_w_out, w_in, b_forget, w_branch, w_out):
    layer_states = []
    for l in range(DEPTH):
        mod = (jax.nn.silu(c) @ w_ada[l] + b_ada[l]).reshape(c.shape[0], N_MOD, D_MODEL)[:, :, None, :]
        h = rmsnorm(x, norm_g[l, 0]) * (1.0 + mod[:, 1]) + mod[:, 0]
        x = x + FFN_RES * mod[:, 2] * rmsnorm(swiglu(h, ffn_w_in[l, 0], ffn_w_out[l, 0]), norm_g[l, 1])
        h = rmsnorm(x, norm_g[l, 2]) * (1.0 + mod[:, 4]) + mod[:, 3]
        past = None if caches is None else tuple(a[l] for a in caches)
        h, st = token_mixing(h, pos, past, w_in[l], b_forget[l], w_branch[l], w_out[l])
        x = x + mod[:, 5] * rmsnorm(h, norm_g[l, 3])
        h = rmsnorm(x, norm_g[l, 4]) * (1.0 + mod[:, 7]) + mod[:, 6]
        x = x + FFN_RES * mod[:, 8] * rmsnorm(swiglu(h, ffn_w_in[l, 1], ffn_w_out[l, 1]), norm_g[l, 5])
        layer_states.append(st)
    states = tuple(jnp.stack(s, axis=0) for s in zip(*layer_states))
    return x, states


def setup_inputs(seed: int = 0) -> dict:
    key = jax.random.key(seed)
    ks = jax.random.split(key, 21)

    def nrm(k, shape):
        return jax.random.normal(k, shape, dtype=jnp.float32)

    kv_sb = (DEPTH, DEC_BATCH, PAST_LEN, H_SB, HEAD_DIM)
    kv_fox = (DEPTH, DEC_BATCH, PAST_LEN, H_FOX, HEAD_DIM)
    kv_dsa = (DEPTH, DEC_BATCH, PAST_LEN, H_DSA, HEAD_DIM)
    return {
        'x_prompt': nrm(ks[0], (BATCH, SEQ, D_MODEL)),
        'x_sample': nrm(ks[1], (DEC_BATCH, DEC_SEQ, D_MODEL)),
        'cache_sb_k': nrm(ks[2], kv_sb),
        'cache_sb_v': nrm(ks[3], kv_sb),
        'cache_fox_k': nrm(ks[4], kv_fox),
        'cache_fox_v': nrm(ks[5], kv_fox),
        'cache_fox_logf': jax.nn.log_sigmoid(1.0 + 0.5 * nrm(ks[6], (DEPTH, DEC_BATCH, PAST_LEN, H_FOX))),
        'cache_dsa_k': nrm(ks[7], kv_dsa),
        'cache_dsa_v': nrm(ks[8], kv_dsa),
        'cache_dsa_kidx': nrm(ks[9], (DEPTH, DEC_BATCH, PAST_LEN, IDX_DIM)),
        'c_prompt': nrm(ks[10], (BATCH, D_MODEL)),
        'c_sample': nrm(ks[11], (DEC_BATCH, D_MODEL)),
        'norm_g': 1.0 + 0.05 * nrm(ks[12], (DEPTH, N_NORM, D_MODEL)),
        'w_ada': nrm(ks[13], (DEPTH, D_MODEL, N_MOD * D_MODEL)) * (0.5 * D_MODEL ** -0.5),
        'b_ada': 0.02 * nrm(ks[14], (DEPTH, N_MOD * D_MODEL)),
        'ffn_w_in': nrm(ks[15], (DEPTH, 2, D_MODEL, 2 * D_FF)) * D_MODEL ** -0.5,
        'ffn_w_out': nrm(ks[16], (DEPTH, 2, D_FF, D_MODEL)) * D_FF ** -0.5,
        'w_in': nrm(ks[17], (DEPTH, D_MODEL, N_IN)) * D_MODEL ** -0.5,
        'b_forget': 1.0 + 0.1 * nrm(ks[18], (DEPTH, H_FOX)),
        'w_branch': nrm(ks[19], (DEPTH, N_BRANCH, MIX_W, D_MODEL)) * MIX_W ** -0.5,
        'w_out': nrm(ks[20], (DEPTH, D_MODEL, D_MODEL)) * D_MODEL ** -0.5,
    }


def reference(x_prompt, x_sample, cache_sb_k, cache_sb_v, cache_fox_k, cache_fox_v, cache_fox_logf,
              cache_dsa_k, cache_dsa_v, cache_dsa_kidx, c_prompt, c_sample,
              norm_g, w_ada, b_ada, ffn_w_in, ffn_w_out, w_in, b_forget, w_branch, w_out):
    pos_p = jnp.arange(x_prompt.shape[1], dtype=jnp.int32)
    pos_s = cache_sb_k.shape[2] + jnp.arange(x_sample.shape[1], dtype=jnp.int32)
    y_prompt, st_p = trunk(x_prompt, c_prompt, pos_p, None,
                           norm_g, w_ada, b_ada, ffn_w_in, ffn_w_out, w_in, b_forget, w_branch, w_out)
    caches = (cache_sb_k, cache_sb_v, cache_fox_k, cache_fox_v, cache_fox_logf,
              cache_dsa_k, cache_dsa_v, cache_dsa_kidx)
    y_sample, st_s = trunk(x_sample, c_sample, pos_s, caches,
                           norm_g, w_ada, b_ada, ffn_w_in, ffn_w_out, w_in, b_forget, w_branch, w_out)
    sb_k_p, sb_v_p, fox_k_p, fox_v_p, fox_logf_p, dsa_k_p, dsa_v_p, dsa_kidx_p = st_p
    sb_k_s, sb_v_s, fox_k_s, fox_v_s, fox_logf_s, dsa_k_s, dsa_v_s, dsa_kidx_s = st_s
    return (y_prompt, y_sample,
            sb_k_p, sb_v_p, fox_k_p, fox_v_p, fox_logf_p, dsa_k_p, dsa_v_p, dsa_kidx_p,
            sb_k_s, sb_v_s, fox_k_s, fox_v_s, fox_logf_s, dsa_k_s, dsa_v_s, dsa_kidx_s)
```

```python
import functools

import numpy as np
import jax
import jax.numpy as jnp
from jax import lax
from jax.experimental import pallas as pl
from jax.experimental.pallas import tpu as pltpu

F32 = jnp.float32
BF16 = jnp.bfloat16

HEAD_DIM = 64
N_HEADS = 6
MIX_W = N_HEADS * HEAD_DIM
N_PAIR = MIX_W // 128
IDX_HEADS = 4
IDX_DIM = 64
CHUNK = 64
CHUNK_SHIFT = 6
TOPK_MAX = 256
N_BRANCH = 3
N_MOD = 9
ROPE_THETA = 10000.0
EPS = 1e-6
FFN_RES = 0.5
QK_SCALE = HEAD_DIM ** -0.5
IDX_SCALE = IDX_DIM ** -0.5
IDX_HEAD_SCALE = IDX_HEADS ** -0.5
LANES = 128
LF_LANE = IDX_DIM
WI_LANE = IDX_DIM + N_HEADS
PROJ_W = 3 * 3 * MIX_W + IDX_HEADS * IDX_DIM + LANES
VMEM_LIMIT = 56 * 1024 * 1024
INT_MIN = np.int32(-2 ** 31)
NEG_BIG = -1e30


def _cparams(sem):
    return pltpu.CompilerParams(dimension_semantics=sem, vmem_limit_bytes=VMEM_LIMIT)


def _dot(a, b):
    return jnp.dot(a, b, preferred_element_type=F32)


def _dot_nt(a, b):
    return lax.dot_general(a, b, (((1,), (1,)), ((), ())), preferred_element_type=F32)


def _split2(x):
    hi = x.astype(BF16)
    lo = (x - hi.astype(F32)).astype(BF16)
    return hi, lo


def _split3(x):
    p1 = x.astype(BF16)
    r1 = x - p1.astype(F32)
    p2 = r1.astype(BF16)
    p3 = (r1 - p2.astype(F32)).astype(BF16)
    return p1, p2, p3


def _log_sigmoid(x):
    return jnp.minimum(x, 0.0) - jnp.log1p(jnp.exp(-jnp.abs(x)))


def _rms(x, g):
    return x * lax.rsqrt(jnp.mean(x * x, axis=-1, keepdims=True) + EPS) * g


def _norm_mod(x, g, scale, shift):
    return _rms(x, g) * (1.0 + scale) + shift


def _strict_upper(n):
    r = lax.broadcasted_iota(jnp.int32, (n, n), 0)
    c = lax.broadcasted_iota(jnp.int32, (n, n), 1)
    return jnp.where(r > c, 1.0, 0.0).astype(BF16)


def _strict_lower(n):
    r = lax.broadcasted_iota(jnp.int32, (n, n), 0)
    c = lax.broadcasted_iota(jnp.int32, (n, n), 1)
    return jnp.where(c > r, 1.0, 0.0).astype(BF16)


def _ada_kernel(c_ref, w_ref, b_ref, o_ref):
    c = c_ref[...]
    s = (c * jax.nn.sigmoid(c)).astype(BF16)
    o_ref[0] = _dot(s, w_ref[0].astype(BF16)) + b_ref[0]


def _ada(c_all, w_ada, b_ada):
    depth, d, n = w_ada.shape
    rows = c_all.shape[0]
    tn = n // 8
    return pl.pallas_call(
        _ada_kernel,
        grid=(depth, n // tn),
        in_specs=[
            pl.BlockSpec((rows, d), lambda l, j: (0, 0)),
            pl.BlockSpec((1, d, tn), lambda l, j: (l, 0, j)),
            pl.BlockSpec((1, 1, tn), lambda l, j: (l, 0, j)),
        ],
        out_specs=pl.BlockSpec((1, rows, tn), lambda l, j: (l, 0, j)),
        out_shape=jax.ShapeDtypeStruct((depth, rows, n), F32),
        compiler_params=_cparams(("arbitrary", "arbitrary")),
        name="ada",
    )(c_all, w_ada, b_ada.reshape(depth, 1, n))


def _ffn_kernel(x_ref, shift_ref, scale_ref, gate_ref, gpre_ref, gpost_ref, wg_ref, wu_ref, wd_ref,
                o_ref, h_scr, acc_scr):
    j = pl.program_id(1)

    @pl.when(j == 0)
    def _():
        h = _norm_mod(x_ref[...], gpre_ref[...], scale_ref[0, 0], shift_ref[0, 0])
        h_scr[...] = h.astype(BF16)
        acc_scr[...] = jnp.zeros_like(acc_scr)

    h = h_scr[...]
    g = _dot(h, wg_ref[...])
    u = _dot(h, wu_ref[...])
    a = (g * jax.nn.sigmoid(g) * u).astype(BF16)
    acc_scr[...] += _dot(a, wd_ref[...])

    @pl.when(j == pl.num_programs(1) - 1)
    def _():
        y = _rms(acc_scr[...], gpost_ref[...])
        o_ref[...] = x_ref[...] + FFN_RES * gate_ref[0, 0] * y


def _mod_spec(r, d, gdiv, comp):
    return pl.BlockSpec((1, 1, r, d), lambda i, *_: (i // gdiv, comp, 0, 0))


def _ffn(x, mod, comp0, g_pre, g_post, w_up, w_dn, *, tm, gdiv):
    n, d = x.shape
    ff = w_dn.shape[0]
    nj = 2 if (ff // 2) % LANES == 0 else 1
    tf = ff // nj
    r = mod.shape[2]
    return pl.pallas_call(
        _ffn_kernel,
        grid=(n // tm, nj),
        in_specs=[
            pl.BlockSpec((tm, d), lambda i, j: (i, 0)),
            _mod_spec(r, d, gdiv, comp0),
            _mod_spec(r, d, gdiv, comp0 + 1),
            _mod_spec(r, d, gdiv, comp0 + 2),
            pl.BlockSpec((1, d), lambda i, j: (0, 0)),
            pl.BlockSpec((1, d), lambda i, j: (0, 0)),
            pl.BlockSpec((d, tf), lambda i, j: (0, j)),
            pl.BlockSpec((d, tf), lambda i, j: (0, nj + j)),
            pl.BlockSpec((tf, d), lambda i, j: (j, 0)),
        ],
        out_specs=pl.BlockSpec((tm, d), lambda i, j: (i, 0)),
        out_shape=jax.ShapeDtypeStruct((n, d), F32),
        scratch_shapes=[pltpu.VMEM((tm, d), BF16), pltpu.VMEM((tm, d), F32)],
        compiler_params=_cparams(("arbitrary", "arbitrary")),
        name="ffn",
    )(x, mod, mod, mod, g_pre, g_post, w_up, w_up, w_dn)


def _rope(x, cos, sin, first_half):
    outs = []
    for c in range(x.shape[1] // LANES):
        xs = x[:, c * LANES:(c + 1) * LANES]
        below = pltpu.roll(xs, 32, 1)
        above = pltpu.roll(xs, LANES - 32, 1)
        outs.append(xs * cos + jnp.where(first_half, above, below) * sin)
    return outs[0] if len(outs) == 1 else jnp.concatenate(outs, axis=1)


def _proj_kernel(x_ref, shift_ref, scale_ref, gpre_ref, w_ref, bf_ref, cos_ref, sin_ref,
                 qsb, ksb, vsb, qfx, kfx, vfx, qds, kds, vds, qidx, kidx2, misc):
    h = _norm_mod(x_ref[...], gpre_ref[...], scale_ref[0, 0], shift_ref[0, 0]).astype(BF16)
    cos = cos_ref[...]
    sin = sin_ref[...]
    lane = lax.broadcasted_iota(jnp.int32, (1, LANES), 1)
    first_half = (lane & (HEAD_DIM - 1)) < (HEAD_DIM // 2)
    w3 = 3 * MIX_W

    y = _dot(h, w_ref[:, 0:w3])
    qsb[...] = (y[:, 0:MIX_W] * QK_SCALE).astype(BF16)
    ksb[...] = y[:, MIX_W:2 * MIX_W]
    vsb[...] = y[:, 2 * MIX_W:w3]

    y = _dot(h, w_ref[:, w3:2 * w3])
    qfx[...] = (y[:, 0:MIX_W] * QK_SCALE).astype(BF16)
    kfx[...] = y[:, MIX_W:2 * MIX_W]
    vfx[...] = y[:, 2 * MIX_W:w3]

    y = _dot(h, w_ref[:, 2 * w3:3 * w3])
    qds[...] = (_rope(y[:, 0:MIX_W], cos, sin, first_half) * QK_SCALE).astype(BF16)
    kds[...] = _rope(y[:, MIX_W:2 * MIX_W], cos, sin, first_half)
    vds[...] = y[:, 2 * MIX_W:w3]

    y = _dot(h, w_ref[:, 3 * w3:PROJ_W])
    nq = IDX_HEADS * IDX_DIM
    qidx[...] = (_rope(y[:, 0:nq], cos, sin, first_half) * IDX_SCALE).astype(BF16)
    m = y[:, nq:nq + LANES]
    m_rot = _rope(m, cos, sin, first_half)
    lf = _log_sigmoid(m + bf_ref[...])
    misc[...] = jnp.where(lane < LF_LANE, m_rot,
                          jnp.where(lane < WI_LANE, lf,
                                    jnp.where(lane < WI_LANE + IDX_HEADS, m, 0.0)))
    kidx2[...] = jnp.where(lane < IDX_DIM, m_rot, pltpu.roll(m_rot, IDX_DIM, 1)).astype(BF16)


def _proj(x, mod, g_pre, w_main, bf_row, cos_t, sin_t, *, tm, gdiv):
    n, d = x.shape
    r = mod.shape[2]
    tab_tiles = cos_t.shape[0] // tm
    row = lambda w: pl.BlockSpec((tm, w), lambda i: (i, 0))
    tab = pl.BlockSpec((tm, LANES), lambda i: (i % tab_tiles, 0))
    shp = lambda w, dt: jax.ShapeDtypeStruct((n, w), dt)
    nq = IDX_HEADS * IDX_DIM
    return pl.pallas_call(
        _proj_kernel,
        grid=(n // tm,),
        in_specs=[
            row(d),
            _mod_spec(r, d, gdiv, 3),
            _mod_spec(r, d, gdiv, 4),
            pl.BlockSpec((1, d), lambda i: (0, 0)),
            pl.BlockSpec((d, PROJ_W), lambda i: (0, 0)),
            pl.BlockSpec((1, LANES), lambda i: (0, 0)),
            tab, tab,
        ],
        out_specs=[row(MIX_W)] * 9 + [row(nq), row(LANES), row(LANES)],
        out_shape=[shp(MIX_W, BF16), shp(MIX_W, F32), shp(MIX_W, F32)] * 3
                  + [shp(nq, BF16), shp(LANES, BF16), shp(LANES, F32)],
        compiler_params=_cparams(("arbitrary",)),
        name="proj",
    )(x, mod, mod, g_pre, w_main, bf_row, cos_t, sin_t)


def _merge_kernel(x_ref, shift_ref, scale_ref, gate_ref, gpre_ref, gpost_ref,
                  osb_ref, ofx_ref, ods_ref, wg_ref, wb_ref, wo_ref, o_ref):
    x = x_ref[...]
    d = x.shape[1]
    h = _norm_mod(x, gpre_ref[...], scale_ref[0, 0], shift_ref[0, 0]).astype(BF16)
    merged = None
    for nb, o_br in enumerate((osb_ref, ofx_ref, ods_ref)):
        gl = _dot(h, wg_ref[:, nb * d:(nb + 1) * d])
        y = _dot(o_br[...], wb_ref[nb])
        t = jax.nn.sigmoid(gl) * y
        merged = t if merged is None else merged + t
    out = _dot(merged.astype(BF16), wo_ref[...])
    o_ref[...] = x + gate_ref[0, 0] * _rms(out, gpost_ref[...])


def _merge(x, mod, g_pre, g_post, o_sb, o_fx, o_ds, w_gate, w_branch, w_out, *, tm, gdiv):
    n, d = x.shape
    r = mod.shape[2]
    row = lambda w: pl.BlockSpec((tm, w), lambda i: (i, 0))
    return pl.pallas_call(
        _merge_kernel,
        grid=(n // tm,),
        in_specs=[
            row(d),
            _mod_spec(r, d, gdiv, 3),
            _mod_spec(r, d, gdiv, 4),
            _mod_spec(r, d, gdiv, 5),
            pl.BlockSpec((1, d), lambda i: (0, 0)),
            pl.BlockSpec((1, d), lambda i: (0, 0)),
            row(MIX_W), row(MIX_W), row(MIX_W),
            pl.BlockSpec((d, N_BRANCH * d), lambda i: (0, 0)),
            pl.BlockSpec((N_BRANCH, MIX_W, d), lambda i: (0, 0, 0)),
            pl.BlockSpec((d, d), lambda i: (0, 0)),
        ],
        out_specs=row(d),
        out_shape=jax.ShapeDtypeStruct((n, d), F32),
        compiler_params=_cparams(("arbitrary",)),
        name="merge",
    )(x, mod, mod, mod, g_pre, g_post, o_sb, o_fx, o_ds, w_gate, w_branch, w_out)


def _aug_lanes(lane, base, ones_first, parts):
    one_lo, val_lo = (base, base + 3) if ones_first else (base + 3, base)
    out = jnp.where((lane >= one_lo) & (lane < one_lo + 3), 1.0, 0.0)
    for k, p in enumerate(parts):
        out = jnp.where(lane == val_lo + k, p, out)
    return out


def _fox_prep_kernel(q_ref, k_ref, misc_ref, qa_ref, ka_ref, carry):
    j = pl.program_id(1)

    @pl.when(j == 0)
    def _():
        carry[...] = jnp.zeros_like(carry)

    lf = misc_ref[...]
    tm = lf.shape[0]
    low = _strict_lower(tm)
    p1, p2, p3 = _split3(lf)
    g = _dot(low, p1) + _dot(low, p2) + _dot(low, p3) + carry[...]
    carry[...] += jnp.sum(lf, axis=0, keepdims=True)

    lane = lax.broadcasted_iota(jnp.int32, (1, LANES), 1)
    lo_half = lane < HEAD_DIM
    for p in range(N_PAIR):
        qp = q_ref[:, p * LANES:(p + 1) * LANES].astype(F32)
        kp = k_ref[:, p * LANES:(p + 1) * LANES]
        for a in range(2):
            hd = 2 * p + a
            gcol = g[:, LF_LANE + hd:LF_LANE + hd + 1]
            g1, g2, g3 = (t.astype(F32) for t in _split3(gcol))
            own = lo_half if a == 0 else jnp.logical_not(lo_half)
            base = HEAD_DIM if a == 0 else 0
            q_aug = jnp.where(own, qp, _aug_lanes(lane, base, True, (-g1, -g2, -g3)))
            k_aug = jnp.where(own, kp, _aug_lanes(lane, base, False, (g1, g2, g3)))
            qa_ref[:, hd * LANES:(hd + 1) * LANES] = q_aug.astype(BF16)
            ka_ref[:, hd * LANES:(hd + 1) * LANES] = k_aug.astype(BF16)


def _fox_prep(q, k, misc, *, tm):
    b, t, _ = q.shape
    nt = t // tm
    rev = lambda w: pl.BlockSpec((None, tm, w), lambda bb, j: (bb, nt - 1 - j, 0))
    return pl.pallas_call(
        _fox_prep_kernel,
        grid=(b, nt),
        in_specs=[rev(MIX_W), rev(MIX_W), rev(LANES)],
        out_specs=[rev(N_HEADS * LANES), rev(N_HEADS * LANES)],
        out_shape=[jax.ShapeDtypeStruct((b, t, N_HEADS * LANES), BF16)] * 2,
        scratch_shapes=[pltpu.VMEM((1, LANES), F32)],
        compiler_params=_cparams(("arbitrary", "arbitrary")),
        name="fox_prep",
    )(q, k, misc)


def _sb_block(qh, kblk, vblk, mask, upper, acc, c):
    z = _dot_nt(qh, kblk)
    ls = _log_sigmoid(z)
    lk = ls - z
    if mask is not None:
        lk = jnp.where(mask, lk, 0.0)
    hi, lo = _split2(lk)
    suffix = _dot(hi, upper) + _dot(lo, upper)
    a = jnp.exp(ls + suffix + c)
    if mask is not None:
        a = jnp.where(mask, a, 0.0)
    acc = acc + _dot(a.astype(BF16), vblk)
    c = c + jnp.sum(lk, axis=-1, keepdims=True)
    return acc, c


def _sb_kernel(q_ref, k_ref, v_ref, o_ref, *, tq, tk):
    i = pl.program_id(2)
    q = q_ref[...]
    lane = lax.broadcasted_iota(jnp.int32, (1, LANES), 1)
    lo_half = lane < HEAD_DIM
    zero = jnp.zeros_like(q)
    q0 = jnp.where(lo_half, q, zero)
    q1 = jnp.where(lo_half, zero, q)
    qpos = i * tq + lax.broadcasted_iota(jnp.int32, (tq, 1), 0)
    upper = _strict_upper(tk)
    nkb = (i + 1) * (tq // tk)

    def body(it, carry):
        acc0, c0, acc1, c1 = carry
        start = pl.multiple_of((nkb - 1 - it) * tk, tk)
        kblk = k_ref[pl.ds(start, tk), :].astype(BF16)
        vblk = v_ref[pl.ds(start, tk), :].astype(BF16)
        kpos = start + lax.broadcasted_iota(jnp.int32, (1, tk), 1)
        mask = kpos < qpos
        acc0, c0 = _sb_block(q0, kblk, vblk, mask, upper, acc0, c0)
        acc1, c1 = _sb_block(q1, kblk, vblk, mask, upper, acc1, c1)
        return acc0, c0, acc1, c1

    za = jnp.zeros((tq, LANES), F32)
    zc = jnp.zeros((tq, 1), F32)
    acc0, _, acc1, _ = lax.fori_loop(0, nkb, body, (za, zc, za, zc))
    o_ref[...] = jnp.where(lo_half, acc0, acc1).astype(o_ref.dtype)


def _sb_prompt(q, k, v, *, tq, tk):
    b, t, _ = q.shape
    return pl.pallas_call(
        functools.partial(_sb_kernel, tq=tq, tk=tk),
        grid=(b, N_PAIR, t // tq),
        in_specs=[
            pl.BlockSpec((None, tq, LANES), lambda bb, p, i: (bb, i, p)),
            pl.BlockSpec((None, t, LANES), lambda bb, p, i: (bb, 0, p)),
            pl.BlockSpec((None, t, LANES), lambda bb, p, i: (bb, 0, p)),
        ],
        out_specs=pl.BlockSpec((None, tq, LANES), lambda bb, p, i: (bb, i, p)),
        out_shape=jax.ShapeDtypeStruct((b, t, MIX_W), BF16),
        compiler_params=_cparams(("arbitrary", "arbitrary", "arbitrary")),
        name="sb_prompt",
    )(q, k, v)


def _softmax_step(s, vblk, m, l, acc):
    m_new = jnp.maximum(m, jnp.max(s, axis=-1, keepdims=True))
    alpha = jnp.exp(m - m_new)
    p = jnp.exp(s - m_new)
    l = alpha * l + jnp.sum(p, axis=-1, keepdims=True)
    acc = alpha * acc + _dot(p.astype(BF16), vblk)
    return m_new, l, acc


def _fox_kernel(qa_ref, ka_ref, v_ref, o_ref, *, tq, tk):
    i = pl.program_id(2)
    qa = qa_ref[...]
    qpos = i * tq + lax.broadcasted_iota(jnp.int32, (tq, 1), 0)
    nkb = (i + 1) * (tq // tk)

    def body(kb, carry):
        start = pl.multiple_of(kb * tk, tk)
        kblk = ka_ref[pl.ds(start, tk), :]
        vblk = v_ref[pl.ds(start, tk), :].astype(BF16)
        kpos = start + lax.broadcasted_iota(jnp.int32, (1, tk), 1)
        mask = kpos <= qpos
        out = []
        for a in range(2):
            m, l, acc = carry[3 * a:3 * a + 3]
            s = _dot_nt(qa[:, a * LANES:(a + 1) * LANES], kblk[:, a * LANES:(a + 1) * LANES])
            s = jnp.where(mask, s, -jnp.inf)
            out.extend(_softmax_step(s, vblk, m, l, acc))
        return tuple(out)

    m0 = jnp.full((tq, 1), NEG_BIG, F32)
    zc = jnp.zeros((tq, 1), F32)
    za = jnp.zeros((tq, LANES), F32)
    _, l0, acc0, _, l1, acc1 = lax.fori_loop(0, nkb, body, (m0, zc, za, m0, zc, za))
    lane = lax.broadcasted_iota(jnp.int32, (1, LANES), 1)
    o_ref[...] = jnp.where(lane < HEAD_DIM, acc0 / l0, acc1 / l1).astype(o_ref.dtype)


def _fox_prompt(qa, ka, v, *, tq, tk):
    b, t, _ = v.shape
    return pl.pallas_call(
        functools.partial(_fox_kernel, tq=tq, tk=tk),
        grid=(b, N_PAIR, t // tq),
        in_specs=[
            pl.BlockSpec((None, tq, 2 * LANES), lambda bb, p, i: (bb, i, p)),
            pl.BlockSpec((None, t, 2 * LANES), lambda bb, p, i: (bb, 0, p)),
            pl.BlockSpec((None, t, LANES), lambda bb, p, i: (bb, 0, p)),
        ],
        out_specs=pl.BlockSpec((None, tq, LANES), lambda bb, p, i: (bb, i, p)),
        out_shape=jax.ShapeDtypeStruct((b, t, MIX_W), BF16),
        compiler_params=_cparams(("arbitrary", "arbitrary", "arbitrary")),
        name="fox_prompt",
    )(qa, ka, v)


def _sort_key(x):
    bits = lax.bitcast_convert_type(x, jnp.int32)
    key = jnp.where(bits < 0, bits ^ jnp.int32(0x7FFFFFFF), bits)
    return jnp.where(x == 0.0, 0, key)


def _count(pred_parts):
    tot = None
    for preds in pred_parts:
        ind = jnp.where(preds[-1], 1.0, 0.0)
        for p in preds[-2::-1]:
            ind = jnp.where(p, ind, 0.0)
        c = jnp.sum(ind, axis=-1, keepdims=True)
        tot = c if tot is None else tot + c
    return tot


def _topk_select(key_refs, pos_list, topk, idx_bits):
    rows = key_refs[0].shape[0]
    kf = float(topk)

    def vbody(it, thr):
        cand = thr + jnp.left_shift(jnp.int32(1), 31 - it)
        cnt = _count([(kr[...] >= cand,) for kr in key_refs])
        return jnp.where(cnt >= kf, cand, thr)

    thr = lax.fori_loop(0, 32, vbody, jnp.full((rows, 1), INT_MIN, jnp.int32))
    need = kf - _count([(kr[...] > thr,) for kr in key_refs])

    def ibody(it, bound):
        cand = bound + jnp.left_shift(jnp.int32(1), idx_bits - 1 - it)
        cnt = _count([(kr[...] == thr, pos < cand) for kr, pos in zip(key_refs, pos_list)])
        return jnp.where(cnt <= need, cand, bound)

    bound = lax.fori_loop(0, idx_bits, ibody, jnp.zeros((rows, 1), jnp.int32))
    out = []
    for kr, pos in zip(key_refs, pos_list):
        key = kr[...]
        out.append((key > thr) | ((key == thr) & (pos < bound)))
    return out


def _dsa_kernel(qidx_ref, kidx2_ref, miscq_ref, q_ref, k_ref, v_ref, o_ref, key_scr, bias_scr, *, tq, topk):
    i = pl.program_id(1)
    t = k_ref.shape[0]
    lane = lax.broadcasted_iota(jnp.int32, (1, LANES), 1)
    lo_half = lane < HEAD_DIM
    kidx2 = kidx2_ref[...]
    wq = miscq_ref[...]
    isc = jnp.zeros((tq, t), F32)
    for h in range(IDX_HEADS):
        qp = qidx_ref[:, (h // 2) * LANES:(h // 2 + 1) * LANES]
        own = lo_half if h % 2 == 0 else jnp.logical_not(lo_half)
        qh = jnp.where(own, qp, jnp.zeros_like(qp))
        s = _dot_nt(qh, kidx2)
        isc = isc + jnp.maximum(s, 0.0) * wq[:, WI_LANE + h:WI_LANE + h + 1]
    isc = isc * IDX_HEAD_SCALE
    qpos = i * tq + lax.broadcasted_iota(jnp.int32, (tq, 1), 0)
    kpos = lax.broadcasted_iota(jnp.int32, (1, t), 1)
    allowed = jnp.right_shift(kpos, CHUNK_SHIFT) <= jnp.right_shift(qpos, CHUNK_SHIFT)
    isc = jnp.where(allowed, isc, -jnp.inf)
    key_scr[...] = _sort_key(isc)
    (sel,) = _topk_select([key_scr], [kpos], topk, int(t).bit_length())
    bias_scr[...] = jnp.where(sel & (isc > -jnp.inf), 0.0, -jnp.inf)

    for p in range(N_PAIR):
        qp = q_ref[:, p * LANES:(p + 1) * LANES]
        kp = k_ref[:, p * LANES:(p + 1) * LANES].astype(BF16)
        vp = v_ref[:, p * LANES:(p + 1) * LANES].astype(BF16)
        outs = []
        for a in range(2):
            own = lo_half if a == 0 else jnp.logical_not(lo_half)
            qh = jnp.where(own, qp, jnp.zeros_like(qp))
            s = _dot_nt(qh, kp) + bias_scr[...]
            m = jnp.max(s, axis=-1, keepdims=True)
            e = jnp.exp(s - m)
            l = jnp.sum(e, axis=-1, keepdims=True)
            outs.append(_dot(e.astype(BF16), vp) / l)
        o_ref[:, p * LANES:(p + 1) * LANES] = jnp.where(lo_half, outs[0], outs[1]).astype(o_ref.dtype)


def _dsa_prompt(qidx, kidx2, misc, q, k, v, *, tq):
    b, t, _ = q.shape
    topk = min(TOPK_MAX, t // 4)
    qrow = lambda w: pl.BlockSpec((None, tq, w), lambda bb, i: (bb, i, 0))
    full = lambda w: pl.BlockSpec((None, t, w), lambda bb, i: (bb, 0, 0))
    return pl.pallas_call(
        functools.partial(_dsa_kernel, tq=tq, topk=topk),
        grid=(b, t // tq),
        in_specs=[qrow(IDX_HEADS * IDX_DIM), full(LANES), qrow(LANES), qrow(MIX_W), full(MIX_W), full(MIX_W)],
        out_specs=qrow(MIX_W),
        out_shape=jax.ShapeDtypeStruct((b, t, MIX_W), BF16),
        scratch_shapes=[pltpu.VMEM((tq, t), jnp.int32), pltpu.VMEM((tq, t), F32)],
        compiler_params=_cparams(("arbitrary", "arbitrary")),
        name="dsa_prompt",
    )(qidx, kidx2, misc, q, k, v)


def _head_rows(x, n_rep):
    return jnp.concatenate([x] * n_rep, axis=0)


def _block_diag_q(q, tq):
    rows = N_HEADS * tq
    rhead = jnp.right_shift(lax.broadcasted_iota(jnp.int32, (rows, 1), 0), tq.bit_length() - 1)
    lhead = jnp.right_shift(lax.broadcasted_iota(jnp.int32, (1, MIX_W), 1), HEAD_DIM.bit_length() - 1)
    qq = _head_rows(q, N_HEADS)
    return jnp.where(rhead == lhead, qq, jnp.zeros_like(qq)), rhead == lhead


def _diag_out(o_bd, diag, tq):
    o = jnp.where(diag, o_bd, 0.0)
    out = o[0:tq]
    for h in range(1, N_HEADS):
        out = out + o[h * tq:(h + 1) * tq]
    return out


def _pad_rows(x, rows):
    return jnp.concatenate([x, jnp.zeros((rows - x.shape[0], x.shape[1]), x.dtype)], axis=0)


def _sb_sample_kernel(q_ref, kn_ref, vn_ref, kp_ref, vp_ref, o_ref, *, tk):
    tq = q_ref.shape[0]
    past = kp_ref.shape[0]
    rows = N_HEADS * tq
    qbd, diag = _block_diag_q(q_ref[...], tq)
    trow = lax.broadcasted_iota(jnp.int32, (rows, 1), 0) & (tq - 1)
    kn = _pad_rows(kn_ref[...], LANES).astype(BF16)
    vn = _pad_rows(vn_ref[...], LANES).astype(BF16)
    mask_n = lax.broadcasted_iota(jnp.int32, (1, LANES), 1) < trow
    acc = jnp.zeros((rows, MIX_W), F32)
    c = jnp.zeros((rows, 1), F32)
    acc, c = _sb_block(qbd, kn, vn, mask_n, _strict_upper(LANES), acc, c)
    upper = _strict_upper(tk)
    for kb in range(past // tk - 1, -1, -1):
        kblk = kp_ref[kb * tk:(kb + 1) * tk, :].astype(BF16)
        vblk = vp_ref[kb * tk:(kb + 1) * tk, :].astype(BF16)
        acc, c = _sb_block(qbd, kblk, vblk, None, upper, acc, c)
    o_ref[...] = _diag_out(acc, diag, tq).astype(o_ref.dtype)


def _fox_sample_kernel(q_ref, kn_ref, vn_ref, miscn_ref, kp_ref, vp_ref, lfp_ref, o_ref, *, tk):
    tq = q_ref.shape[0]
    past = kp_ref.shape[0]
    rows = N_HEADS * tq
    hp = lfp_ref.shape[0]
    qbd, diag = _block_diag_q(q_ref[...], tq)
    trow = lax.broadcasted_iota(jnp.int32, (rows, 1), 0) & (tq - 1)
    lane = lax.broadcasted_iota(jnp.int32, (1, LANES), 1)

    misc_n = _pad_rows(miscn_ref[...], LANES)
    pick = jnp.where(lax.broadcasted_iota(jnp.int32, (hp, LANES), 1)
                     == LF_LANE + lax.broadcasted_iota(jnp.int32, (hp, LANES), 0), 1.0, 0.0).astype(BF16)
    lf_new = sum(_dot_nt(pick, part) for part in _split3(misc_n))
    up_n = _strict_upper(LANES)
    g_new = sum(_dot(part, up_n) for part in _split3(lf_new))
    tot_new = jnp.sum(lf_new, axis=-1, keepdims=True)

    def expand(g):
        return jnp.concatenate([jnp.broadcast_to(g[h:h + 1], (tq, g.shape[1])) for h in range(N_HEADS)], axis=0)

    g_new_x = expand(g_new)
    g_q = jnp.sum(jnp.where(lane == trow, g_new_x, 0.0), axis=-1, keepdims=True)

    kn = _pad_rows(kn_ref[...], LANES).astype(BF16)
    vn = _pad_rows(vn_ref[...], LANES).astype(BF16)
    s_n = _dot_nt(qbd, kn) + g_new_x - g_q
    s_n = jnp.where(lane <= trow, s_n, -jnp.inf)
    m, l, acc = _softmax_step(s_n, vn, jnp.full((rows, 1), NEG_BIG, F32), jnp.zeros((rows, 1), F32),
                              jnp.zeros((rows, MIX_W), F32))
    upper = _strict_upper(tk)
    carry = tot_new
    for kb in range(past // tk - 1, -1, -1):
        lf_blk = lfp_ref[:, kb * tk:(kb + 1) * tk]
        g_blk = sum(_dot(part, upper) for part in _split3(lf_blk)) + carry
        carry = carry + jnp.sum(lf_blk, axis=-1, keepdims=True)
        kblk = kp_ref[kb * tk:(kb + 1) * tk, :].astype(BF16)
        vblk = vp_ref[kb * tk:(kb + 1) * tk, :].astype(BF16)
        s = _dot_nt(qbd, kblk) + expand(g_blk) - g_q
        m, l, acc = _softmax_step(s, vblk, m, l, acc)
    o_ref[...] = _diag_out(acc / l, diag, tq).astype(o_ref.dtype)


def _dsa_sample_kernel(qidx_ref, miscn_ref, kin_ref, q_ref, kn_ref, vn_ref, kip_ref, kp_ref, vp_ref, o_ref,
                       keyp_scr, keyn_scr, *, topk, pos0):
    tq = q_ref.shape[0]
    past = kp_ref.shape[0]
    rows = N_HEADS * tq
    lane = lax.broadcasted_iota(jnp.int32, (1, LANES), 1)
    qi = qidx_ref[...].astype(F32)
    qs = jnp.concatenate([qi[:, h * IDX_DIM:(h + 1) * IDX_DIM] for h in range(IDX_HEADS)], axis=0).astype(BF16)
    wq = miscn_ref[...]
    ws = jnp.concatenate([wq[:, WI_LANE + h:WI_LANE + h + 1] for h in range(IDX_HEADS)], axis=0)
    kip = kip_ref[...].astype(BF16)
    kin = _pad_rows(kin_ref[...][:, 0:IDX_DIM], LANES).astype(BF16)

    def head_sum(s):
        s = jnp.maximum(s, 0.0) * ws
        out = s[0:tq]
        for h in range(1, IDX_HEADS):
            out = out + s[h * tq:(h + 1) * tq]
        return out * IDX_HEAD_SCALE

    isc_p = head_sum(_dot_nt(qs, kip))
    isc_n = head_sum(_dot_nt(qs, kin))
    qchunk = jnp.right_shift(pos0 + lax.broadcasted_iota(jnp.int32, (tq, 1), 0), CHUNK_SHIFT)
    ppos = lax.broadcasted_iota(jnp.int32, (1, past), 1)
    npos = pos0 + lane
    ok_p = jnp.right_shift(ppos, CHUNK_SHIFT) <= qchunk
    ok_n = (jnp.right_shift(npos, CHUNK_SHIFT) <= qchunk) & (lane < tq)
    isc_p = jnp.where(ok_p, isc_p, -jnp.inf)
    isc_n = jnp.where(ok_n, isc_n, -jnp.inf)
    keyp_scr[...] = _sort_key(isc_p)
    keyn_scr[...] = jnp.where(lane < tq, _sort_key(isc_n), INT_MIN)
    sel_p, sel_n = _topk_select([keyp_scr, keyn_scr], [ppos, npos], topk, int(past + LANES).bit_length())
    bias_p = _head_rows(jnp.where(sel_p & (isc_p > -jnp.inf), 0.0, -jnp.inf), N_HEADS)
    bias_n = _head_rows(jnp.where(sel_n & (isc_n > -jnp.inf), 0.0, -jnp.inf), N_HEADS)

    qbd, diag = _block_diag_q(q_ref[...], tq)
    kn = _pad_rows(kn_ref[...], LANES).astype(BF16)
    vn = _pad_rows(vn_ref[...], LANES).astype(BF16)
    s_p = _dot_nt(qbd, kp_ref[...].astype(BF16)) + bias_p
    s_n = _dot_nt(qbd, kn) + bias_n
    m = jnp.maximum(jnp.max(s_p, axis=-1, keepdims=True), jnp.max(s_n, axis=-1, keepdims=True))
    e_p = jnp.exp(s_p - m)
    e_n = jnp.exp(s_n - m)
    l = jnp.sum(e_p, axis=-1, keepdims=True) + jnp.sum(e_n, axis=-1, keepdims=True)
    o = (_dot(e_p.astype(BF16), vp_ref[...].astype(BF16)) + _dot(e_n.astype(BF16), vn)) / l
    o_ref[...] = _diag_out(o, diag, tq).astype(o_ref.dtype)


def _sample_attention(layer, q_sb, k_sb, v_sb, q_fx, k_fx, v_fx, q_ds, k_ds, v_ds, qidx, misc,
                      c_sb_k, c_sb_v, c_fx_k, c_fx_v, lf_t, c_ds_k, c_ds_v, c_kidx):
    b, tq, _ = q_sb.shape
    past = c_sb_k.shape[2]
    tk = min(256, past)
    new = lambda w: pl.BlockSpec((None, tq, w), lambda bb: (bb, 0, 0))
    cache = lambda w: pl.BlockSpec((None, None, past, w), lambda bb: (layer, bb, 0, 0))
    out = jax.ShapeDtypeStruct((b, tq, MIX_W), BF16)
    cp = _cparams(("arbitrary",))
    o_sb = pl.pallas_call(
        functools.partial(_sb_sample_kernel, tk=tk), grid=(b,),
        in_specs=[new(MIX_W), new(MIX_W), new(MIX_W), cache(MIX_W), cache(MIX_W)],
        out_specs=new(MIX_W), out_shape=out, compiler_params=cp, name="sb_sample",
    )(q_sb, k_sb, v_sb, c_sb_k, c_sb_v)
    hp = lf_t.shape[2]
    o_fx = pl.pallas_call(
        functools.partial(_fox_sample_kernel, tk=tk), grid=(b,),
        in_specs=[new(MIX_W), new(MIX_W), new(MIX_W), new(LANES), cache(MIX_W), cache(MIX_W),
                  pl.BlockSpec((None, None, hp, past), lambda bb: (layer, bb, 0, 0))],
        out_specs=new(MIX_W), out_shape=out, compiler_params=cp, name="fox_sample",
    )(q_fx, k_fx, v_fx, misc, c_fx_k, c_fx_v, lf_t)
    topk = min(TOPK_MAX, (past + tq) // 4)
    o_ds = pl.pallas_call(
        functools.partial(_dsa_sample_kernel, topk=topk, pos0=past), grid=(b,),
        in_specs=[new(IDX_HEADS * IDX_DIM), new(LANES), new(LANES), new(MIX_W), new(MIX_W), new(MIX_W),
                  cache(IDX_DIM), cache(MIX_W), cache(MIX_W)],
        out_specs=new(MIX_W), out_shape=out,
        scratch_shapes=[pltpu.VMEM((tq, past), jnp.int32), pltpu.VMEM((tq, LANES), jnp.int32)],
        compiler_params=cp, name="dsa_sample",
    )(qidx, misc, misc, q_ds, k_ds, v_ds, c_kidx, c_ds_k, c_ds_v)
    return o_sb, o_fx, o_ds


def _rope_tables(pos):
    half = HEAD_DIM // 2
    inv_freq = ROPE_THETA ** (-jnp.arange(half, dtype=F32) / half)
    ang = pos.astype(F32)[:, None] * inv_freq[None, :]
    cos = jnp.cos(ang)
    sin = jnp.sin(ang)
    cos_t = jnp.concatenate([cos, cos, cos, cos], axis=1)
    sin_t = jnp.concatenate([-sin, sin, -sin, sin], axis=1)
    return cos_t, sin_t


def _prep_weights(w_in, b_forget, ffn_w_in, ffn_w_out, w_branch, w_out):
    w3 = 3 * MIX_W
    o = np.cumsum([0, w3, w3, N_HEADS, w3, IDX_HEADS * IDX_DIM, IDX_DIM, IDX_HEADS]).tolist()
    sb, fx, fl, ds, qi, ki, wi = (w_in[:, :, o[k]:o[k + 1]] for k in range(7))
    pad = jnp.zeros(w_in.shape[:2] + (LANES - IDX_DIM - N_HEADS - IDX_HEADS,), w_in.dtype)
    w_main = jnp.concatenate([sb, fx, ds, qi, ki, fl, wi, pad], axis=2).astype(BF16)
    w_gate = w_in[:, :, o[7]:].astype(BF16)
    depth = w_in.shape[0]
    bf_row = jnp.zeros((depth, 1, LANES), F32).at[:, 0, LF_LANE:LF_LANE + N_HEADS].set(b_forget)
    return (w_main, w_gate, bf_row, ffn_w_in.astype(BF16), ffn_w_out.astype(BF16),
            w_branch.astype(BF16), w_out.astype(BF16))


def _trunk(x3, mod_l, gdiv, tm, pos, caches, norm_g, weights, *, tq_attn):
    w_main, w_gate, bf_row, w_up, w_dn, w_branch, w_out = weights
    b, t, d = x3.shape
    n = b * t
    depth = w_main.shape[0]
    x = x3.reshape(n, d)
    cos_t, sin_t = _rope_tables(pos)
    if caches is not None:
        cos_t = jnp.tile(cos_t, (tm // t, 1))
        sin_t = jnp.tile(sin_t, (tm // t, 1))
        lf_t = jnp.swapaxes(caches[4], 2, 3)
        lf_t = jnp.pad(lf_t, ((0, 0), (0, 0), (0, 16 - lf_t.shape[2]), (0, 0)))
    states = []
    for l in range(depth):
        g = lambda k: norm_g[l, k][None, :]
        mod = mod_l[l]
        x = _ffn(x, mod, 0, g(0), g(1), w_up[l, 0], w_dn[l, 0], tm=tm, gdiv=gdiv)
        (q_sb, k_sb, v_sb, q_fx, k_fx, v_fx, q_ds, k_ds, v_ds, qidx, kidx2, misc) = _proj(
            x, mod, g(2), w_main[l], bf_row[l], cos_t, sin_t, tm=tm, gdiv=gdiv)
        r3 = lambda a: a.reshape(b, t, a.shape[-1])
        if caches is None:
            qa, ka = _fox_prep(r3(q_fx), r3(k_fx), r3(misc), tm=min(256, t))
            o_sb = _sb_prompt(r3(q_sb), r3(k_sb), r3(v_sb), tq=tq_attn, tk=LANES)
            o_fx = _fox_prompt(qa, ka, r3(v_fx), tq=tq_attn, tk=LANES)
            o_ds = _dsa_prompt(r3(qidx), r3(kidx2), r3(misc), r3(q_ds), r3(k_ds), r3(v_ds), tq=LANES)
        else:
            o_sb, o_fx, o_ds = _sample_attention(
                l, r3(q_sb), r3(k_sb), r3(v_sb), r3(q_fx), r3(k_fx), r3(v_fx), r3(q_ds), r3(k_ds), r3(v_ds),
                r3(qidx), r3(misc), caches[0], caches[1], caches[2], caches[3], lf_t,
                caches[5], caches[6], caches[7])
        o2 = lambda a: a.reshape(n, MIX_W)
        x = _merge(x, mod, g(2), g(3), o2(o_sb), o2(o_fx), o2(o_ds), w_gate[l], w_branch[l], w_out[l],
                   tm=tm, gdiv=gdiv)
        x = _ffn(x, mod, 6, g(4), g(5), w_up[l, 1], w_dn[l, 1], tm=tm, gdiv=gdiv)
        hd = lambda a: a.reshape(b, t, N_HEADS, HEAD_DIM)
        states.append((hd(k_sb), hd(v_sb), hd(k_fx), hd(v_fx),
                       misc[:, LF_LANE:LF_LANE + N_HEADS].reshape(b, t, N_HEADS),
                       hd(k_ds), hd(v_ds), misc[:, 0:IDX_DIM].reshape(b, t, IDX_DIM)))
    return x.reshape(b, t, d), tuple(jnp.stack(s, axis=0) for s in zip(*states))


def kernel(x_prompt, x_sample, cache_sb_k, cache_sb_v, cache_fox_k, cache_fox_v, cache_fox_logf, cache_dsa_k, cache_dsa_v, cache_dsa_kidx, c_prompt, c_sample, norm_g, w_ada, b_ada, ffn_w_in, ffn_w_out, w_in, b_forget, w_branch, w_out):
    bp, tp, d = x_prompt.shape
    bs, ts, _ = x_sample.shape
    depth = w_in.shape[0]
    past = cache_sb_k.shape[2]
    mix = lambda a: a.reshape(a.shape[:3] + (MIX_W,))
    caches = (mix(cache_sb_k), mix(cache_sb_v), mix(cache_fox_k), mix(cache_fox_v), cache_fox_logf,
              mix(cache_dsa_k), mix(cache_dsa_v), cache_dsa_kidx)
    weights = _prep_weights(w_in, b_forget, ffn_w_in, ffn_w_out, w_branch, w_out)

    rows = bp + bs
    rows_pad = -(-rows // 8) * 8
    c_all = jnp.concatenate([c_prompt, c_sample, jnp.zeros((rows_pad - rows, d), F32)], axis=0)
    mod = _ada(c_all, w_ada, b_ada)
    mod_p = [mod[l, :bp].reshape(bp, N_MOD, 1, d) for l in range(depth)]
    ns = bs * ts
    mod_s = [jnp.repeat(mod[l, bp:rows].reshape(bs, N_MOD, d), ts, axis=0)
             .reshape(1, ns, N_MOD, d).transpose(0, 2, 1, 3) for l in range(depth)]

    tm_p = min(512, tp)
    pos_p = jnp.arange(tp, dtype=jnp.int32)
    pos_s = past + jnp.arange(ts, dtype=jnp.int32)
    y_p, st_p = _trunk(x_prompt, mod_p, tp // tm_p, tm_p, pos_p, None, norm_g, weights, tq_attn=min(256, tp))
    y_s, st_s = _trunk(x_sample, mod_s, 1, ns, pos_s, caches, norm_g, weights, tq_attn=ts)
    return (y_p, y_s) + st_p + st_s
```

```python
import functools

import numpy as np
import jax
import jax.numpy as jnp
from jax import lax
from jax.experimental import pallas as pl
from jax.experimental.pallas import tpu as pltpu

F32 = jnp.float32
BF16 = jnp.bfloat16

HEAD_DIM = 64
N_HEADS = 6
MIX_W = N_HEADS * HEAD_DIM
N_PAIR = MIX_W // 128
IDX_HEADS = 4
IDX_DIM = 64
CHUNK = 64
CHUNK_SHIFT = 6
TOPK_MAX = 256
N_BRANCH = 3
N_MOD = 9
ROPE_THETA = 10000.0
EPS = 1e-6
FFN_RES = 0.5
QK_SCALE = HEAD_DIM ** -0.5
IDX_SCALE = IDX_DIM ** -0.5
IDX_HEAD_SCALE = IDX_HEADS ** -0.5
LANES = 128
LF_LANE = IDX_DIM
WI_LANE = IDX_DIM + N_HEADS
PROJ_W = 3 * 3 * MIX_W + IDX_HEADS * IDX_DIM + LANES
VMEM_LIMIT = 56 * 1024 * 1024
INT_MIN = np.int32(-2 ** 31)
NEG_BIG = -1e30


def _cparams(sem):
    return pltpu.CompilerParams(dimension_semantics=sem, vmem_limit_bytes=VMEM_LIMIT)


def _dot(a, b):
    return jnp.dot(a, b, preferred_element_type=F32)


def _dot_nt(a, b):
    return lax.dot_general(a, b, (((1,), (1,)), ((), ())), preferred_element_type=F32)


def _split2(x):
    hi = x.astype(BF16)
    lo = (x - hi.astype(F32)).astype(BF16)
    return hi, lo


def _split3(x):
    p1 = x.astype(BF16)
    r1 = x - p1.astype(F32)
    p2 = r1.astype(BF16)
    p3 = (r1 - p2.astype(F32)).astype(BF16)
    return p1, p2, p3


def _log_sigmoid(x):
    return jnp.minimum(x, 0.0) - jnp.log1p(jnp.exp(-jnp.abs(x)))


def _rms(x, g):
    return x * lax.rsqrt(jnp.mean(x * x, axis=-1, keepdims=True) + EPS) * g


def _norm_mod(x, g, scale, shift):
    return _rms(x, g) * (1.0 + scale) + shift


def _strict_upper(n):
    r = lax.broadcasted_iota(jnp.int32, (n, n), 0)
    c = lax.broadcasted_iota(jnp.int32, (n, n), 1)
    return jnp.where(r > c, 1.0, 0.0).astype(BF16)


def _strict_lower(n):
    r = lax.broadcasted_iota(jnp.int32, (n, n), 0)
    c = lax.broadcasted_iota(jnp.int32, (n, n), 1)
    return jnp.where(c > r, 1.0, 0.0).astype(BF16)


def _ada_kernel(c_ref, w_ref, b_ref, o_ref):
    c = c_ref[...]
    s = (c * jax.nn.sigmoid(c)).astype(BF16)
    o_ref[0] = _dot(s, w_ref[0].astype(BF16)) + b_ref[0]


def _ada(c_all, w_ada, b_ada):
    depth, d, n = w_ada.shape
    rows = c_all.shape[0]
    tn = n // 8
    return pl.pallas_call(
        _ada_kernel,
        grid=(depth, n // tn),
        in_specs=[
            pl.BlockSpec((rows, d), lambda l, j: (0, 0)),
            pl.BlockSpec((1, d, tn), lambda l, j: (l, 0, j)),
            pl.BlockSpec((1, 1, tn), lambda l, j: (l, 0, j)),
        ],
        out_specs=pl.BlockSpec((1, rows, tn), lambda l, j: (l, 0, j)),
        out_shape=jax.ShapeDtypeStruct((depth, rows, n), F32),
        compiler_params=_cparams(("arbitrary", "arbitrary")),
        name="ada",
    )(c_all, w_ada, b_ada.reshape(depth, 1, n))


def _ffn_kernel(x_ref, shift_ref, scale_ref, gate_ref, gpre_ref, gpost_ref, wg_ref, wu_ref, wd_ref,
                o_ref, h_scr, acc_scr):
    j = pl.program_id(1)

    @pl.when(j == 0)
    def _():
        h = _norm_mod(x_ref[...], gpre_ref[...], scale_ref[0, 0], shift_ref[0, 0])
        h_scr[...] = h.astype(BF16)
        acc_scr[...] = jnp.zeros_like(acc_scr)

    h = h_scr[...]
    g = _dot(h, wg_ref[...])
    u = _dot(h, wu_ref[...])
    a = (g * jax.nn.sigmoid(g) * u).astype(BF16)
    acc_scr[...] += _dot(a, wd_ref[...])

    @pl.when(j == pl.num_programs(1) - 1)
    def _():
        y = _rms(acc_scr[...], gpost_ref[...])
        o_ref[...] = x_ref[...] + FFN_RES * gate_ref[0, 0] * y


def _mod_spec(r, d, gdiv, comp):
    return pl.BlockSpec((1, 1, r, d), lambda i, *_: (i // gdiv, comp, 0, 0))


def _ffn(x, mod, comp0, g_pre, g_post, w_up, w_dn, *, tm, gdiv):
    n, d = x.shape
    ff = w_dn.shape[0]
    nj = 2 if (ff // 2) % LANES == 0 else 1
    tf = ff // nj
    r = mod.shape[2]
    return pl.pallas_call(
        _ffn_kernel,
        grid=(n // tm, nj),
        in_specs=[
            pl.BlockSpec((tm, d), lambda i, j: (i, 0)),
            _mod_spec(r, d, gdiv, comp0),
            _mod_spec(r, d, gdiv, comp0 + 1),
            _mod_spec(r, d, gdiv, comp0 + 2),
            pl.BlockSpec((1, d), lambda i, j: (0, 0)),
            pl.BlockSpec((1, d), lambda i, j: (0, 0)),
            pl.BlockSpec((d, tf), lambda i, j: (0, j)),
            pl.BlockSpec((d, tf), lambda i, j: (0, nj + j)),
            pl.BlockSpec((tf, d), lambda i, j: (j, 0)),
        ],
        out_specs=pl.BlockSpec((tm, d), lambda i, j: (i, 0)),
        out_shape=jax.ShapeDtypeStruct((n, d), F32),
        scratch_shapes=[pltpu.VMEM((tm, d), BF16), pltpu.VMEM((tm, d), F32)],
        compiler_params=_cparams(("arbitrary", "arbitrary")),
        name="ffn",
    )(x, mod, mod, mod, g_pre, g_post, w_up, w_up, w_dn)


def _rope(x, cos, sin, first_half):
    outs = []
    for c in range(x.shape[1] // LANES):
        xs = x[:, c * LANES:(c + 1) * LANES]
        below = pltpu.roll(xs, 32, 1)
        above = pltpu.roll(xs, LANES - 32, 1)
        outs.append(xs * cos + jnp.where(first_half, above, below) * sin)
    return outs[0] if len(outs) == 1 else jnp.concatenate(outs, axis=1)


def _proj_kernel(x_ref, shift_ref, scale_ref, gpre_ref, w_ref, bf_ref, cos_ref, sin_ref,
                 qsb, ksb, vsb, qfx, kfx, vfx, qds, kds, vds, qidx, kidx2, misc):
    h = _norm_mod(x_ref[...], gpre_ref[...], scale_ref[0, 0], shift_ref[0, 0]).astype(BF16)
    cos = cos_ref[...]
    sin = sin_ref[...]
    lane = lax.broadcasted_iota(jnp.int32, (1, LANES), 1)
    first_half = (lane & (HEAD_DIM - 1)) < (HEAD_DIM // 2)
    w3 = 3 * MIX_W

    y = _dot(h, w_ref[:, 0:w3])
    qsb[...] = (y[:, 0:MIX_W] * QK_SCALE).astype(BF16)
    ksb[...] = y[:, MIX_W:2 * MIX_W]
    vsb[...] = y[:, 2 * MIX_W:w3]

    y = _dot(h, w_ref[:, w3:2 * w3])
    qfx[...] = (y[:, 0:MIX_W] * QK_SCALE).astype(BF16)
    kfx[...] = y[:, MIX_W:2 * MIX_W]
    vfx[...] = y[:, 2 * MIX_W:w3]

    y = _dot(h, w_ref[:, 2 * w3:3 * w3])
    qds[...] = (_rope(y[:, 0:MIX_W], cos, sin, first_half) * QK_SCALE).astype(BF16)
    kds[...] = _rope(y[:, MIX_W:2 * MIX_W], cos, sin, first_half)
    vds[...] = y[:, 2 * MIX_W:w3]

    y = _dot(h, w_ref[:, 3 * w3:PROJ_W])
    nq = IDX_HEADS * IDX_DIM
    qidx[...] = (_rope(y[:, 0:nq], cos, sin, first_half) * IDX_SCALE).astype(BF16)
    m = y[:, nq:nq + LANES]
    m_rot = _rope(m, cos, sin, first_half)
    lf = _log_sigmoid(m + bf_ref[...])
    misc[...] = jnp.where(lane < LF_LANE, m_rot,
                          jnp.where(lane < WI_LANE, lf,
                                    jnp.where(lane < WI_LANE + IDX_HEADS, m, 0.0)))
    kidx2[...] = jnp.where(lane < IDX_DIM, m_rot, pltpu.roll(m_rot, IDX_DIM, 1)).astype(BF16)


def _proj(x, mod, g_pre, w_main, bf_row, cos_t, sin_t, *, tm, gdiv):
    n, d = x.shape
    r = mod.shape[2]
    tab_tiles = cos_t.shape[0] // tm
    row = lambda w: pl.BlockSpec((tm, w), lambda i: (i, 0))
    tab = pl.BlockSpec((tm, LANES), lambda i: (i % tab_tiles, 0))
    shp = lambda w, dt: jax.ShapeDtypeStruct((n, w), dt)
    nq = IDX_HEADS * IDX_DIM
    return pl.pallas_call(
        _proj_kernel,
        grid=(n // tm,),
        in_specs=[
            row(d),
            _mod_spec(r, d, gdiv, 3),
            _mod_spec(r, d, gdiv, 4),
            pl.BlockSpec((1, d), lambda i: (0, 0)),
            pl.BlockSpec((d, PROJ_W), lambda i: (0, 0)),
            pl.BlockSpec((1, LANES), lambda i: (0, 0)),
            tab, tab,
        ],
        out_specs=[row(MIX_W)] * 9 + [row(nq), row(LANES), row(LANES)],
        out_shape=[shp(MIX_W, BF16), shp(MIX_W, F32), shp(MIX_W, F32)] * 3
                  + [shp(nq, BF16), shp(LANES, BF16), shp(LANES, F32)],
        compiler_params=_cparams(("arbitrary",)),
        name="proj",
    )(x, mod, mod, g_pre, w_main, bf_row, cos_t, sin_t)


def _merge_kernel(x_ref, shift_ref, scale_ref, gate_ref, gpre_ref, gpost_ref,
                  osb_ref, ofx_ref, ods_ref, wg_ref, wb_ref, wo_ref, o_ref):
    x = x_ref[...]
    d = x.shape[1]
    h = _norm_mod(x, gpre_ref[...], scale_ref[0, 0], shift_ref[0, 0]).astype(BF16)
    merged = None
    for nb, o_br in enumerate((osb_ref, ofx_ref, ods_ref)):
        gl = _dot(h, wg_ref[:, nb * d:(nb + 1) * d])
        y = _dot(o_br[...], wb_ref[nb])
        t = jax.nn.sigmoid(gl) * y
        merged = t if merged is None else merged + t
    out = _dot(merged.astype(BF16), wo_ref[...])
    o_ref[...] = x + gate_ref[0, 0] * _rms(out, gpost_ref[...])


def _merge(x, mod, g_pre, g_post, o_sb, o_fx, o_ds, w_gate, w_branch, w_out, *, tm, gdiv):
    n, d = x.shape
    r = mod.shape[2]
    row = lambda w: pl.BlockSpec((tm, w), lambda i: (i, 0))
    return pl.pallas_call(
        _merge_kernel,
        grid=(n // tm,),
        in_specs=[
            row(d),
            _mod_spec(r, d, gdiv, 3),
            _mod_spec(r, d, gdiv, 4),
            _mod_spec(r, d, gdiv, 5),
            pl.BlockSpec((1, d), lambda i: (0, 0)),
            pl.BlockSpec((1, d), lambda i: (0, 0)),
            row(MIX_W), row(MIX_W), row(MIX_W),
            pl.BlockSpec((d, N_BRANCH * d), lambda i: (0, 0)),
            pl.BlockSpec((N_BRANCH, MIX_W, d), lambda i: (0, 0, 0)),
            pl.BlockSpec((d, d), lambda i: (0, 0)),
        ],
        out_specs=row(d),
        out_shape=jax.ShapeDtypeStruct((n, d), F32),
        compiler_params=_cparams(("arbitrary",)),
        name="merge",
    )(x, mod, mod, mod, g_pre, g_post, o_sb, o_fx, o_ds, w_gate, w_branch, w_out)


def _aug_lanes(lane, base, ones_first, parts):
    one_lo, val_lo = (base, base + 3) if ones_first else (base + 3, base)
    out = jnp.where((lane >= one_lo) & (lane < one_lo + 3), 1.0, 0.0)
    for k, p in enumerate(parts):
        out = jnp.where(lane == val_lo + k, p, out)
    return out


def _fox_prep_kernel(q_ref, k_ref, misc_ref, qa_ref, ka_ref, carry):
    j = pl.program_id(1)

    @pl.when(j == 0)
    def _():
        carry[...] = jnp.zeros_like(carry)

    lf = misc_ref[...]
    tm = lf.shape[0]
    low = _strict_lower(tm)
    p1, p2, p3 = _split3(lf)
    g = _dot(low, p1) + _dot(low, p2) + _dot(low, p3) + carry[...]
    carry[...] += jnp.sum(lf, axis=0, keepdims=True)

    lane = lax.broadcasted_iota(jnp.int32, (1, LANES), 1)
    lo_half = lane < HEAD_DIM
    for p in range(N_PAIR):
        qp = q_ref[:, p * LANES:(p + 1) * LANES].astype(F32)
        kp = k_ref[:, p * LANES:(p + 1) * LANES]
        for a in range(2):
            hd = 2 * p + a
            gcol = g[:, LF_LANE + hd:LF_LANE + hd + 1]
            g1, g2, g3 = (t.astype(F32) for t in _split3(gcol))
            own = lo_half if a == 0 else jnp.logical_not(lo_half)
            base = HEAD_DIM if a == 0 else 0
            q_aug = jnp.where(own, qp, _aug_lanes(lane, base, True, (-g1, -g2, -g3)))
            k_aug = jnp.where(own, kp, _aug_lanes(lane, base, False, (g1, g2, g3)))
            qa_ref[:, hd * LANES:(hd + 1) * LANES] = q_aug.astype(BF16)
            ka_ref[:, hd * LANES:(hd + 1) * LANES] = k_aug.astype(BF16)


def _fox_prep(q, k, misc, *, tm):
    b, t, _ = q.shape
    nt = t // tm
    rev = lambda w: pl.BlockSpec((None, tm, w), lambda bb, j: (bb, nt - 1 - j, 0))
    return pl.pallas_call(
        _fox_prep_kernel,
        grid=(b, nt),
        in_specs=[rev(MIX_W), rev(MIX_W), rev(LANES)],
        out_specs=[rev(N_HEADS * LANES), rev(N_HEADS * LANES)],
        out_shape=[jax.ShapeDtypeStruct((b, t, N_HEADS * LANES), BF16)] * 2,
        scratch_shapes=[pltpu.VMEM((1, LANES), F32)],
        compiler_params=_cparams(("arbitrary", "arbitrary")),
        name="fox_prep",
    )(q, k, misc)


def _sb_block(qh, kblk, vblk, mask, upper, acc, c, kv_t=False):
    z = _dot(qh, kblk) if kv_t else _dot_nt(qh, kblk)
    neg_abs = lax.bitcast_convert_type(lax.bitcast_convert_type(z, jnp.int32) | INT_MIN, F32)
    ls = jnp.minimum(z, 0.0) - jnp.log(1.0 + jnp.exp(neg_abs))
    lk = ls - z
    if mask is not None:
        lk = jnp.where(mask, lk, 0.0)
    hi, lo = _split2(lk)
    sub = upper.shape[0]
    tot = []
    for j in range(z.shape[1] // sub - 1, -1, -1):
        sl = slice(j * sub, (j + 1) * sub)
        tot.append(_dot(hi[:, sl], upper) + _dot(lo[:, sl], upper) + c)
        c = c + jnp.sum(lk[:, sl], axis=-1, keepdims=True)
    tot = tot[0] if len(tot) == 1 else jnp.concatenate(tot[::-1], axis=1)
    a = jnp.exp(ls + tot)
    if mask is not None:
        a = jnp.where(mask, a, 0.0)
    a = a.astype(BF16)
    acc = acc + (_dot_nt(a, vblk) if kv_t else _dot(a, vblk))
    return acc, c


def _sb_kernel(q_ref, k_ref, v_ref, o_ref, *, tq, tk):
    i = pl.program_id(2)
    q = q_ref[...]
    lane = lax.broadcasted_iota(jnp.int32, (1, LANES), 1)
    lo_half = lane < HEAD_DIM
    zero = jnp.zeros_like(q)
    q0 = jnp.where(lo_half, q, zero)
    q1 = jnp.where(lo_half, zero, q)
    qpos = i * tq + lax.broadcasted_iota(jnp.int32, (tq, 1), 0)
    upper = _strict_upper(LANES)
    nkb = jnp.right_shift((i + 1) * tq + (tk - 1), tk.bit_length() - 1)

    def body(it, carry):
        acc0, c0, acc1, c1 = carry
        start = pl.multiple_of((nkb - 1 - it) * tk, tk)
        kblk = k_ref[pl.ds(start, tk), :].astype(BF16)
        vblk = v_ref[pl.ds(start, tk), :].astype(BF16)
        kpos = start + lax.broadcasted_iota(jnp.int32, (1, tk), 1)
        mask = kpos < qpos
        acc0, c0 = _sb_block(q0, kblk, vblk, mask, upper, acc0, c0)
        acc1, c1 = _sb_block(q1, kblk, vblk, mask, upper, acc1, c1)
        return acc0, c0, acc1, c1

    za = jnp.zeros((tq, LANES), F32)
    zc = jnp.zeros((tq, 1), F32)
    acc0, _, acc1, _ = lax.fori_loop(0, nkb, body, (za, zc, za, zc))
    o_ref[...] = jnp.where(lo_half, acc0, acc1).astype(o_ref.dtype)


def _sb_prompt(q, k, v, *, tq, tk):
    b, t, _ = q.shape
    return pl.pallas_call(
        functools.partial(_sb_kernel, tq=tq, tk=tk),
        grid=(b, N_PAIR, t // tq),
        in_specs=[
            pl.BlockSpec((None, tq, LANES), lambda bb, p, i: (bb, i, p)),
            pl.BlockSpec((None, t, LANES), lambda bb, p, i: (bb, 0, p)),
            pl.BlockSpec((None, t, LANES), lambda bb, p, i: (bb, 0, p)),
        ],
        out_specs=pl.BlockSpec((None, tq, LANES), lambda bb, p, i: (bb, i, p)),
        out_shape=jax.ShapeDtypeStruct((b, t, MIX_W), BF16),
        compiler_params=_cparams(("arbitrary", "arbitrary", "arbitrary")),
        name="sb_prompt",
    )(q, k, v)


def _softmax_step(s, vblk, m, l, acc, v_t=False):
    m_new = jnp.maximum(m, jnp.max(s, axis=-1, keepdims=True))
    alpha = jnp.exp(m - m_new)
    p = jnp.exp(s - m_new)
    l = alpha * l + jnp.sum(p, axis=-1, keepdims=True)
    p = p.astype(BF16)
    acc = alpha * acc + (_dot_nt(p, vblk) if v_t else _dot(p, vblk))
    return m_new, l, acc


def _fox_kernel(qa_ref, ka_ref, v_ref, o_ref, *, tq, tk):
    i = pl.program_id(2)
    qa = qa_ref[...]
    qpos = i * tq + lax.broadcasted_iota(jnp.int32, (tq, 1), 0)
    nkb = jnp.right_shift((i + 1) * tq + (tk - 1), tk.bit_length() - 1)

    def body(kb, carry):
        start = pl.multiple_of(kb * tk, tk)
        kblk = ka_ref[pl.ds(start, tk), :]
        vblk = v_ref[pl.ds(start, tk), :].astype(BF16)
        kpos = start + lax.broadcasted_iota(jnp.int32, (1, tk), 1)
        mask = kpos <= qpos
        out = []
        for a in range(2):
            m, l, acc = carry[3 * a:3 * a + 3]
            s = _dot_nt(qa[:, a * LANES:(a + 1) * LANES], kblk[:, a * LANES:(a + 1) * LANES])
            s = jnp.where(mask, s, -jnp.inf)
            out.extend(_softmax_step(s, vblk, m, l, acc))
        return tuple(out)

    m0 = jnp.full((tq, 1), NEG_BIG, F32)
    zc = jnp.zeros((tq, 1), F32)
    za = jnp.zeros((tq, LANES), F32)
    _, l0, acc0, _, l1, acc1 = lax.fori_loop(0, nkb, body, (m0, zc, za, m0, zc, za))
    lane = lax.broadcasted_iota(jnp.int32, (1, LANES), 1)
    o_ref[...] = jnp.where(lane < HEAD_DIM, acc0 / l0, acc1 / l1).astype(o_ref.dtype)


def _fox_prompt(qa, ka, v, *, tq, tk):
    b, t, _ = v.shape
    return pl.pallas_call(
        functools.partial(_fox_kernel, tq=tq, tk=tk),
        grid=(b, N_PAIR, t // tq),
        in_specs=[
            pl.BlockSpec((None, tq, 2 * LANES), lambda bb, p, i: (bb, i, p)),
            pl.BlockSpec((None, t, 2 * LANES), lambda bb, p, i: (bb, 0, p)),
            pl.BlockSpec((None, t, LANES), lambda bb, p, i: (bb, 0, p)),
        ],
        out_specs=pl.BlockSpec((None, tq, LANES), lambda bb, p, i: (bb, i, p)),
        out_shape=jax.ShapeDtypeStruct((b, t, MIX_W), BF16),
        compiler_params=_cparams(("arbitrary", "arbitrary", "arbitrary")),
        name="fox_prompt",
    )(qa, ka, v)


def _sort_key(x):
    bits = lax.bitcast_convert_type(x, jnp.int32)
    key = jnp.where(bits < 0, bits ^ jnp.int32(0x7FFFFFFF), bits)
    return jnp.where(x == 0.0, 0, key)


def _count(pred_parts):
    tot = None
    for preds in pred_parts:
        ind = jnp.where(preds[-1], 1.0, 0.0)
        for p in preds[-2::-1]:
            ind = jnp.where(p, ind, 0.0)
        c = jnp.sum(ind, axis=-1, keepdims=True)
        tot = c if tot is None else tot + c
    return tot


def _topk_select(key_gets, pos_list, rows, topk, idx_bits):
    kf = float(topk)

    def vbody(it, thr):
        cand = thr + jnp.left_shift(jnp.int32(1), 31 - it)
        cnt = _count([(kg() >= cand,) for kg in key_gets])
        return jnp.where(cnt >= kf, cand, thr)

    thr = lax.fori_loop(0, 32, vbody, jnp.full((rows, 1), INT_MIN, jnp.int32))
    need = kf - _count([(kg() > thr,) for kg in key_gets])
    n_eq = _count([(kg() == thr,) for kg in key_gets])

    def ibody(it, bound):
        cand = bound + jnp.left_shift(jnp.int32(1), idx_bits - 1 - it)
        cnt = _count([(kg() == thr, pos < cand) for kg, pos in zip(key_gets, pos_list)])
        return jnp.where(cnt <= need, cand, bound)

    some_partial = jnp.max(jnp.where(n_eq > need, 1.0, 0.0)) > 0.0
    bound = lax.cond(
        some_partial,
        lambda: lax.fori_loop(0, idx_bits, ibody, jnp.zeros((rows, 1), jnp.int32)),
        lambda: jnp.full((rows, 1), 1 << idx_bits, jnp.int32))
    out = []
    for kg, pos in zip(key_gets, pos_list):
        key = kg()
        out.append((key > thr) | ((key == thr) & (pos < bound)))
    return out


def _dsa_body(i, n_keys, qidx_ref, kidx2_ref, miscq_ref, q_ref, k_ref, v_ref, o_ref, key_scr, bias_scr, *, tq, topk):
    lane = lax.broadcasted_iota(jnp.int32, (1, LANES), 1)
    lo_half = lane < HEAD_DIM
    kidx2 = kidx2_ref[0:n_keys, :]
    wq = miscq_ref[...]
    isc = jnp.zeros((tq, n_keys), F32)
    for h in range(IDX_HEADS):
        qp = qidx_ref[:, (h // 2) * LANES:(h // 2 + 1) * LANES]
        own = lo_half if h % 2 == 0 else jnp.logical_not(lo_half)
        qh = jnp.where(own, qp, jnp.zeros_like(qp))
        s = _dot_nt(qh, kidx2)
        isc = isc + jnp.maximum(s, 0.0) * wq[:, WI_LANE + h:WI_LANE + h + 1]
    isc = isc * IDX_HEAD_SCALE
    qpos = i * tq + lax.broadcasted_iota(jnp.int32, (tq, 1), 0)
    kpos = lax.broadcasted_iota(jnp.int32, (1, n_keys), 1)
    allowed = jnp.right_shift(kpos, CHUNK_SHIFT) <= jnp.right_shift(qpos, CHUNK_SHIFT)
    isc = jnp.where(allowed, isc, -jnp.inf)
    key_scr[:, 0:n_keys] = _sort_key(isc)
    (sel,) = _topk_select([lambda: key_scr[:, 0:n_keys]], [kpos], tq, topk, int(n_keys).bit_length())
    bias_scr[:, 0:n_keys] = jnp.where(sel & (isc > -jnp.inf), 0.0, -jnp.inf)

    for p in range(N_PAIR):
        qp = q_ref[:, p * LANES:(p + 1) * LANES]
        kp = k_ref[0:n_keys, p * LANES:(p + 1) * LANES].astype(BF16)
        vp = v_ref[0:n_keys, p * LANES:(p + 1) * LANES].astype(BF16)
        outs = []
        for a in range(2):
            own = lo_half if a == 0 else jnp.logical_not(lo_half)
            qh = jnp.where(own, qp, jnp.zeros_like(qp))
            s = _dot_nt(qh, kp) + bias_scr[:, 0:n_keys]
            m = jnp.max(s, axis=-1, keepdims=True)
            e = jnp.exp(s - m)
            l = jnp.sum(e, axis=-1, keepdims=True)
            outs.append(_dot(e.astype(BF16), vp) / l)
        o_ref[:, p * LANES:(p + 1) * LANES] = jnp.where(lo_half, outs[0], outs[1]).astype(o_ref.dtype)


def _dsa_kernel(*refs, tq, topk, n_groups):
    i = pl.program_id(1)
    t = refs[4].shape[0]
    glen = t // n_groups
    group = jnp.right_shift(i * tq, glen.bit_length() - 1)
    for g in range(n_groups):
        @pl.when(group == g)
        def _():
            _dsa_body(i, (g + 1) * glen, *refs, tq=tq, topk=topk)


def _dsa_prompt(qidx, kidx2, misc, q, k, v, *, tq):
    b, t, _ = q.shape
    topk = min(TOPK_MAX, t // 4)
    glen = t // 4
    n_groups = 4 if (glen % tq == 0 and glen >= topk) else 1
    qrow = lambda w: pl.BlockSpec((None, tq, w), lambda bb, i: (bb, i, 0))
    full = lambda w: pl.BlockSpec((None, t, w), lambda bb, i: (bb, 0, 0))
    return pl.pallas_call(
        functools.partial(_dsa_kernel, tq=tq, topk=topk, n_groups=n_groups),
        grid=(b, t // tq),
        in_specs=[qrow(IDX_HEADS * IDX_DIM), full(LANES), qrow(LANES), qrow(MIX_W), full(MIX_W), full(MIX_W)],
        out_specs=qrow(MIX_W),
        out_shape=jax.ShapeDtypeStruct((b, t, MIX_W), BF16),
        scratch_shapes=[pltpu.VMEM((tq, t), jnp.int32), pltpu.VMEM((tq, t), F32)],
        compiler_params=_cparams(("arbitrary", "arbitrary")),
        name="dsa_prompt",
    )(qidx, kidx2, misc, q, k, v)


def _head_rows(x, n_rep):
    return jnp.concatenate([x] * n_rep, axis=0)


def _block_diag_q(q, tq):
    rows = N_HEADS * tq
    rhead = jnp.right_shift(lax.broadcasted_iota(jnp.int32, (rows, 1), 0), tq.bit_length() - 1)
    lhead = jnp.right_shift(lax.broadcasted_iota(jnp.int32, (1, MIX_W), 1), HEAD_DIM.bit_length() - 1)
    qq = _head_rows(q, N_HEADS)
    return jnp.where(rhead == lhead, qq, jnp.zeros_like(qq)), rhead == lhead


def _diag_out(o_bd, diag, tq):
    o = jnp.where(diag, o_bd, 0.0)
    out = o[0:tq]
    for h in range(1, N_HEADS):
        out = out + o[h * tq:(h + 1) * tq]
    return out


def _pad_rows(x, rows):
    return jnp.concatenate([x, jnp.zeros((rows - x.shape[0], x.shape[1]), x.dtype)], axis=0)


def _sb_sample_kernel(q_ref, kn_ref, vn_ref, kp_ref, vp_ref, o_ref, *, tk):
    tq = q_ref.shape[0]
    past = kp_ref.shape[1]
    rows = N_HEADS * tq
    qbd, diag = _block_diag_q(q_ref[...], tq)
    trow = lax.broadcasted_iota(jnp.int32, (rows, 1), 0) & (tq - 1)
    kn = _pad_rows(kn_ref[...], LANES).astype(BF16)
    vn = _pad_rows(vn_ref[...], LANES).astype(BF16)
    mask_n = lax.broadcasted_iota(jnp.int32, (1, LANES), 1) < trow
    acc = jnp.zeros((rows, MIX_W), F32)
    c = jnp.zeros((rows, 1), F32)
    upper = _strict_upper(LANES)
    acc, c = _sb_block(qbd, kn, vn, mask_n, upper, acc, c)
    for kb in range(past // tk - 1, -1, -1):
        kblk = kp_ref[:, kb * tk:(kb + 1) * tk].astype(BF16)
        vblk = vp_ref[:, kb * tk:(kb + 1) * tk].astype(BF16)
        acc, c = _sb_block(qbd, kblk, vblk, None, upper, acc, c, kv_t=True)
    o_ref[...] = _diag_out(acc, diag, tq).astype(o_ref.dtype)


def _fox_sample_kernel(q_ref, kn_ref, vn_ref, miscn_ref, kp_ref, vp_ref, lfp_ref, o_ref, *, tk):
    tq = q_ref.shape[0]
    past = kp_ref.shape[1]
    rows = N_HEADS * tq
    hp = lfp_ref.shape[0]
    qbd, diag = _block_diag_q(q_ref[...], tq)
    trow = lax.broadcasted_iota(jnp.int32, (rows, 1), 0) & (tq - 1)
    lane = lax.broadcasted_iota(jnp.int32, (1, LANES), 1)

    misc_n = _pad_rows(miscn_ref[...], LANES)
    pick = jnp.where(lax.broadcasted_iota(jnp.int32, (hp, LANES), 1)
                     == LF_LANE + lax.broadcasted_iota(jnp.int32, (hp, LANES), 0), 1.0, 0.0).astype(BF16)
    lf_new = sum(_dot_nt(pick, part) for part in _split3(misc_n))
    up_n = _strict_upper(LANES)
    g_new = sum(_dot(part, up_n) for part in _split3(lf_new))
    tot_new = jnp.sum(lf_new, axis=-1, keepdims=True)

    def expand(g):
        return jnp.concatenate([jnp.broadcast_to(g[h:h + 1], (tq, g.shape[1])) for h in range(N_HEADS)], axis=0)

    g_new_x = expand(g_new)
    g_q = jnp.sum(jnp.where(lane == trow, g_new_x, 0.0), axis=-1, keepdims=True)

    kn = _pad_rows(kn_ref[...], LANES).astype(BF16)
    vn = _pad_rows(vn_ref[...], LANES).astype(BF16)
    s_n = _dot_nt(qbd, kn) + g_new_x - g_q
    s_n = jnp.where(lane <= trow, s_n, -jnp.inf)
    m, l, acc = _softmax_step(s_n, vn, jnp.full((rows, 1), NEG_BIG, F32), jnp.zeros((rows, 1), F32),
                              jnp.zeros((rows, MIX_W), F32))
    upper = _strict_upper(tk)
    carry = tot_new
    for kb in range(past // tk - 1, -1, -1):
        lf_blk = lfp_ref[:, kb * tk:(kb + 1) * tk]
        g_blk = sum(_dot(part, upper) for part in _split3(lf_blk)) + carry
        carry = carry + jnp.sum(lf_blk, axis=-1, keepdims=True)
        kblk = kp_ref[:, kb * tk:(kb + 1) * tk].astype(BF16)
        vblk = vp_ref[:, kb * tk:(kb + 1) * tk].astype(BF16)
        s = _dot(qbd, kblk) + expand(g_blk) - g_q
        m, l, acc = _softmax_step(s, vblk, m, l, acc, v_t=True)
    o_ref[...] = _diag_out(acc / l, diag, tq).astype(o_ref.dtype)


def _dsa_sample_kernel(qidx_ref, miscn_ref, kin_ref, q_ref, kn_ref, vn_ref, kip_ref, kp_ref, vp_ref, o_ref,
                       keyp_scr, keyn_scr, *, topk, pos0):
    tq = q_ref.shape[0]
    past = kp_ref.shape[1]
    lane = lax.broadcasted_iota(jnp.int32, (1, LANES), 1)
    qi = qidx_ref[...].astype(F32)
    qs = jnp.concatenate([qi[:, h * IDX_DIM:(h + 1) * IDX_DIM] for h in range(IDX_HEADS)], axis=0).astype(BF16)
    wq = miscn_ref[...]
    ws = jnp.concatenate([wq[:, WI_LANE + h:WI_LANE + h + 1] for h in range(IDX_HEADS)], axis=0)
    kip = kip_ref[...].astype(BF16)
    kin = _pad_rows(kin_ref[...][:, 0:IDX_DIM], LANES).astype(BF16)

    def head_sum(s):
        s = jnp.maximum(s, 0.0) * ws
        out = s[0:tq]
        for h in range(1, IDX_HEADS):
            out = out + s[h * tq:(h + 1) * tq]
        return out * IDX_HEAD_SCALE

    isc_p = head_sum(_dot(qs, kip))
    isc_n = head_sum(_dot_nt(qs, kin))
    qchunk = jnp.right_shift(pos0 + lax.broadcasted_iota(jnp.int32, (tq, 1), 0), CHUNK_SHIFT)
    ppos = lax.broadcasted_iota(jnp.int32, (1, past), 1)
    npos = pos0 + lane
    ok_p = jnp.right_shift(ppos, CHUNK_SHIFT) <= qchunk
    ok_n = (jnp.right_shift(npos, CHUNK_SHIFT) <= qchunk) & (lane < tq)
    isc_p = jnp.where(ok_p, isc_p, -jnp.inf)
    isc_n = jnp.where(ok_n, isc_n, -jnp.inf)
    keyp_scr[...] = _sort_key(isc_p)
    keyn_scr[...] = jnp.where(lane < tq, _sort_key(isc_n), INT_MIN)
    sel_p, sel_n = _topk_select([lambda: keyp_scr[...], lambda: keyn_scr[...]], [ppos, npos], tq, topk,
                                int(past + LANES).bit_length())
    bias_p = _head_rows(jnp.where(sel_p & (isc_p > -jnp.inf), 0.0, -jnp.inf), N_HEADS)
    bias_n = _head_rows(jnp.where(sel_n & (isc_n > -jnp.inf), 0.0, -jnp.inf), N_HEADS)

    qbd, diag = _block_diag_q(q_ref[...], tq)
    kn = _pad_rows(kn_ref[...], LANES).astype(BF16)
    vn = _pad_rows(vn_ref[...], LANES).astype(BF16)
    s_p = _dot(qbd, kp_ref[...].astype(BF16)) + bias_p
    s_n = _dot_nt(qbd, kn) + bias_n
    m = jnp.maximum(jnp.max(s_p, axis=-1, keepdims=True), jnp.max(s_n, axis=-1, keepdims=True))
    e_p = jnp.exp(s_p - m)
    e_n = jnp.exp(s_n - m)
    l = jnp.sum(e_p, axis=-1, keepdims=True) + jnp.sum(e_n, axis=-1, keepdims=True)
    o = (_dot_nt(e_p.astype(BF16), vp_ref[...].astype(BF16)) + _dot(e_n.astype(BF16), vn)) / l
    o_ref[...] = _diag_out(o, diag, tq).astype(o_ref.dtype)


def _sample_attention(layer, q_sb, k_sb, v_sb, q_fx, k_fx, v_fx, q_ds, k_ds, v_ds, qidx, misc,
                      c_sb_k, c_sb_v, c_fx_k, c_fx_v, lf_t, c_ds_k, c_ds_v, c_kidx):
    b, tq, _ = q_sb.shape
    past = c_sb_k.shape[3]
    tk = min(256, past)
    new = lambda w: pl.BlockSpec((None, tq, w), lambda bb: (bb, 0, 0))
    cache = lambda w: pl.BlockSpec((None, None, w, past), lambda bb: (layer, bb, 0, 0))
    out = jax.ShapeDtypeStruct((b, tq, MIX_W), BF16)
    cp = _cparams(("arbitrary",))
    o_sb = pl.pallas_call(
        functools.partial(_sb_sample_kernel, tk=tk), grid=(b,),
        in_specs=[new(MIX_W), new(MIX_W), new(MIX_W), cache(MIX_W), cache(MIX_W)],
        out_specs=new(MIX_W), out_shape=out, compiler_params=cp, name="sb_sample",
    )(q_sb, k_sb, v_sb, c_sb_k, c_sb_v)
    hp = lf_t.shape[2]
    o_fx = pl.pallas_call(
        functools.partial(_fox_sample_kernel, tk=tk), grid=(b,),
        in_specs=[new(MIX_W), new(MIX_W), new(MIX_W), new(LANES), cache(MIX_W), cache(MIX_W),
                  cache(hp)],
        out_specs=new(MIX_W), out_shape=out, compiler_params=cp, name="fox_sample",
    )(q_fx, k_fx, v_fx, misc, c_fx_k, c_fx_v, lf_t)
    topk = min(TOPK_MAX, (past + tq) // 4)
    o_ds = pl.pallas_call(
        functools.partial(_dsa_sample_kernel, topk=topk, pos0=past), grid=(b,),
        in_specs=[new(IDX_HEADS * IDX_DIM), new(LANES), new(LANES), new(MIX_W), new(MIX_W), new(MIX_W),
                  cache(IDX_DIM), cache(MIX_W), cache(MIX_W)],
        out_specs=new(MIX_W), out_shape=out,
        scratch_shapes=[pltpu.VMEM((tq, past), jnp.int32), pltpu.VMEM((tq, LANES), jnp.int32)],
        compiler_params=cp, name="dsa_sample",
    )(qidx, misc, misc, q_ds, k_ds, v_ds, c_kidx, c_ds_k, c_ds_v)
    return o_sb, o_fx, o_ds


def _rope_tables(pos):
    half = HEAD_DIM // 2
    inv_freq = ROPE_THETA ** (-jnp.arange(half, dtype=F32) / half)
    ang = pos.astype(F32)[:, None] * inv_freq[None, :]
    cos = jnp.cos(ang)
    sin = jnp.sin(ang)
    cos_t = jnp.concatenate([cos, cos, cos, cos], axis=1)
    sin_t = jnp.concatenate([-sin, sin, -sin, sin], axis=1)
    return cos_t, sin_t


def _prep_weights(w_in, b_forget, ffn_w_in, ffn_w_out, w_branch, w_out):
    w3 = 3 * MIX_W
    o = np.cumsum([0, w3, w3, N_HEADS, w3, IDX_HEADS * IDX_DIM, IDX_DIM, IDX_HEADS]).tolist()
    sb, fx, fl, ds, qi, ki, wi = (w_in[:, :, o[k]:o[k + 1]] for k in range(7))
    pad = jnp.zeros(w_in.shape[:2] + (LANES - IDX_DIM - N_HEADS - IDX_HEADS,), w_in.dtype)
    w_main = jnp.concatenate([sb, fx, ds, qi, ki, fl, wi, pad], axis=2).astype(BF16)
    w_gate = w_in[:, :, o[7]:].astype(BF16)
    depth = w_in.shape[0]
    bf_row = jnp.zeros((depth, 1, LANES), F32).at[:, 0, LF_LANE:LF_LANE + N_HEADS].set(b_forget)
    return (w_main, w_gate, bf_row, ffn_w_in.astype(BF16), ffn_w_out.astype(BF16),
            w_branch.astype(BF16), w_out.astype(BF16))


def _trunk(x3, mod_l, gdiv, tm, pos, caches, norm_g, weights, *, tq_attn):
    w_main, w_gate, bf_row, w_up, w_dn, w_branch, w_out = weights
    b, t, d = x3.shape
    n = b * t
    depth = w_main.shape[0]
    x = x3.reshape(n, d)
    cos_t, sin_t = _rope_tables(pos)
    if caches is not None:
        cos_t = jnp.tile(cos_t, (tm // t, 1))
        sin_t = jnp.tile(sin_t, (tm // t, 1))
        lf_t = jnp.swapaxes(caches[4], 2, 3)
        lf_t = jnp.pad(lf_t, ((0, 0), (0, 0), (0, 16 - lf_t.shape[2]), (0, 0)))
    states = []
    for l in range(depth):
        g = lambda k: norm_g[l, k][None, :]
        mod = mod_l[l]
        x = _ffn(x, mod, 0, g(0), g(1), w_up[l, 0], w_dn[l, 0], tm=tm, gdiv=gdiv)
        (q_sb, k_sb, v_sb, q_fx, k_fx, v_fx, q_ds, k_ds, v_ds, qidx, kidx2, misc) = _proj(
            x, mod, g(2), w_main[l], bf_row[l], cos_t, sin_t, tm=tm, gdiv=gdiv)
        r3 = lambda a: a.reshape(b, t, a.shape[-1])
        if caches is None:
            qa, ka = _fox_prep(r3(q_fx), r3(k_fx), r3(misc), tm=min(256, t))
            o_sb = _sb_prompt(r3(q_sb), r3(k_sb), r3(v_sb), tq=tq_attn, tk=min(512, t))
            o_fx = _fox_prompt(qa, ka, r3(v_fx), tq=tq_attn, tk=min(512, t))
            o_ds = _dsa_prompt(r3(qidx), r3(kidx2), r3(misc), r3(q_ds), r3(k_ds), r3(v_ds), tq=tq_attn)
        else:
            o_sb, o_fx, o_ds = _sample_attention(
                l, r3(q_sb), r3(k_sb), r3(v_sb), r3(q_fx), r3(k_fx), r3(v_fx), r3(q_ds), r3(k_ds), r3(v_ds),
                r3(qidx), r3(misc), caches[0], caches[1], caches[2], caches[3], lf_t,
                caches[5], caches[6], caches[7])
        o2 = lambda a: a.reshape(n, MIX_W)
        x = _merge(x, mod, g(2), g(3), o2(o_sb), o2(o_fx), o2(o_ds), w_gate[l], w_branch[l], w_out[l],
                   tm=tm, gdiv=gdiv)
        x = _ffn(x, mod, 6, g(4), g(5), w_up[l, 1], w_dn[l, 1], tm=tm, gdiv=gdiv)
        hd = lambda a: a.reshape(b, t, N_HEADS, HEAD_DIM)
        states.append((hd(k_sb), hd(v_sb), hd(k_fx), hd(v_fx),
                       misc[:, LF_LANE:LF_LANE + N_HEADS].reshape(b, t, N_HEADS),
                       hd(k_ds), hd(v_ds), misc[:, 0:IDX_DIM].reshape(b, t, IDX_DIM)))
    return x.reshape(b, t, d), tuple(jnp.stack(s, axis=0) for s in zip(*states))


def kernel(x_prompt, x_sample, cache_sb_k, cache_sb_v, cache_fox_k, cache_fox_v, cache_fox_logf, cache_dsa_k, cache_dsa_v, cache_dsa_kidx, c_prompt, c_sample, norm_g, w_ada, b_ada, ffn_w_in, ffn_w_out, w_in, b_forget, w_branch, w_out):
    bp, tp, d = x_prompt.shape
    bs, ts, _ = x_sample.shape
    depth = w_in.shape[0]
    past = cache_sb_k.shape[2]
    mix = lambda a: jnp.transpose(a, (0, 1, 3, 4, 2)).reshape(a.shape[:2] + (MIX_W, a.shape[2]))
    caches = (mix(cache_sb_k), mix(cache_sb_v), mix(cache_fox_k), mix(cache_fox_v), cache_fox_logf,
              mix(cache_dsa_k), mix(cache_dsa_v), jnp.swapaxes(cache_dsa_kidx, 2, 3))
    weights = _prep_weights(w_in, b_forget, ffn_w_in, ffn_w_out, w_branch, w_out)

    rows = bp + bs
    rows_pad = -(-rows // 8) * 8
    c_all = jnp.concatenate([c_prompt, c_sample, jnp.zeros((rows_pad - rows, d), F32)], axis=0)
    mod = _ada(c_all, w_ada, b_ada)
    mod_p = [mod[l, :bp].reshape(bp, N_MOD, 1, d) for l in range(depth)]
    ns = bs * ts
    mod_s = [jnp.repeat(mod[l, bp:rows].reshape(bs, N_MOD, d), ts, axis=0)
             .reshape(1, ns, N_MOD, d).transpose(0, 2, 1, 3) for l in range(depth)]

    tm_p = min(512, tp)
    pos_p = jnp.arange(tp, dtype=jnp.int32)
    pos_s = past + jnp.arange(ts, dtype=jnp.int32)
    y_p, st_p = _trunk(x_prompt, mod_p, tp // tm_p, tm_p, pos_p, None, norm_g, weights, tq_attn=min(256, tp))
    y_s, st_s = _trunk(x_sample, mod_s, 1, ns, pos_s, caches, norm_g, weights, tq_attn=ts)
    return (y_p, y_s) + st_p + st_s
```

```python
import functools

import numpy as np
import jax
import jax.numpy as jnp
from jax import lax
from jax.experimental import pallas as pl
from jax.experimental.pallas import tpu as pltpu

F32 = jnp.float32
BF16 = jnp.bfloat16

HEAD_DIM = 64
N_HEADS = 6
MIX_W = N_HEADS * HEAD_DIM
N_PAIR = MIX_W // 128
IDX_HEADS = 4
IDX_DIM = 64
CHUNK = 64
CHUNK_SHIFT = 6
TOPK_MAX = 256
N_BRANCH = 3
N_MOD = 9
ROPE_THETA = 10000.0
EPS = 1e-6
FFN_RES = 0.5
QK_SCALE = HEAD_DIM ** -0.5
IDX_SCALE = IDX_DIM ** -0.5
IDX_HEAD_SCALE = IDX_HEADS ** -0.5
LANES = 128
LF_LANE = IDX_DIM
WI_LANE = IDX_DIM + N_HEADS
PROJ_W = 3 * 3 * MIX_W + IDX_HEADS * IDX_DIM + LANES
VMEM_LIMIT = 56 * 1024 * 1024
SOFTMAX_ROWS = 256
INT_MIN = np.int32(-2 ** 31)
KEY_NEG_INF = np.int32(-2139095041)
NEG_BIG = -1e30


def _cparams(sem):
    return pltpu.CompilerParams(dimension_semantics=sem, vmem_limit_bytes=VMEM_LIMIT)


def _dot(a, b):
    return jnp.dot(a, b, preferred_element_type=F32)


def _dot_nt(a, b):
    return lax.dot_general(a, b, (((1,), (1,)), ((), ())), preferred_element_type=F32)


def _split2(x):
    hi = x.astype(BF16)
    lo = (x - hi.astype(F32)).astype(BF16)
    return hi, lo


def _split3(x):
    p1 = x.astype(BF16)
    r1 = x - p1.astype(F32)
    p2 = r1.astype(BF16)
    p3 = (r1 - p2.astype(F32)).astype(BF16)
    return p1, p2, p3


def _log_sigmoid(x):
    return jnp.minimum(x, 0.0) - jnp.log1p(jnp.exp(-jnp.abs(x)))


def _rms(x, g):
    return x * lax.rsqrt(jnp.mean(x * x, axis=-1, keepdims=True) + EPS) * g


def _norm_mod(x, g, scale, shift):
    return _rms(x, g) * (1.0 + scale) + shift


def _strict_upper(n):
    r = lax.broadcasted_iota(jnp.int32, (n, n), 0)
    c = lax.broadcasted_iota(jnp.int32, (n, n), 1)
    return jnp.where(r > c, 1.0, 0.0).astype(BF16)


def _strict_lower(n):
    r = lax.broadcasted_iota(jnp.int32, (n, n), 0)
    c = lax.broadcasted_iota(jnp.int32, (n, n), 1)
    return jnp.where(c > r, 1.0, 0.0).astype(BF16)


def _ada_kernel(c_ref, w_ref, b_ref, o_ref):
    c = c_ref[...]
    s = (c * jax.nn.sigmoid(c)).astype(BF16)
    o_ref[0] = _dot(s, w_ref[0].astype(BF16)) + b_ref[0]


def _ada(c_all, w_ada, b_ada):
    depth, d, n = w_ada.shape
    rows = c_all.shape[0]
    tn = n // 8
    return pl.pallas_call(
        _ada_kernel,
        grid=(depth, n // tn),
        in_specs=[
            pl.BlockSpec((rows, d), lambda l, j: (0, 0)),
            pl.BlockSpec((1, d, tn), lambda l, j: (l, 0, j)),
            pl.BlockSpec((1, 1, tn), lambda l, j: (l, 0, j)),
        ],
        out_specs=pl.BlockSpec((1, rows, tn), lambda l, j: (l, 0, j)),
        out_shape=jax.ShapeDtypeStruct((depth, rows, n), F32),
        compiler_params=_cparams(("arbitrary", "arbitrary")),
        name="ada",
    )(c_all, w_ada, b_ada.reshape(depth, 1, n))


def _ffn_kernel(x_ref, shift_ref, scale_ref, gate_ref, gpre_ref, gpost_ref, wg_ref, wu_ref, wd_ref,
                o_ref, h_scr, acc_scr):
    j = pl.program_id(1)

    @pl.when(j == 0)
    def _():
        h = _norm_mod(x_ref[...], gpre_ref[...], scale_ref[0, 0], shift_ref[0, 0])
        h_scr[...] = h.astype(BF16)
        acc_scr[...] = jnp.zeros_like(acc_scr)

    h = h_scr[...]
    g = _dot(h, wg_ref[...])
    u = _dot(h, wu_ref[...])
    a = (g * jax.nn.sigmoid(g) * u).astype(BF16)
    acc_scr[...] += _dot(a, wd_ref[...])

    @pl.when(j == pl.num_programs(1) - 1)
    def _():
        y = _rms(acc_scr[...], gpost_ref[...])
        o_ref[...] = x_ref[...] + FFN_RES * gate_ref[0, 0] * y


def _mod_spec(r, d, gdiv, comp):
    return pl.BlockSpec((1, 1, r, d), lambda i, *_: (i // gdiv, comp, 0, 0))


def _ffn(x, mod, comp0, g_pre, g_post, w_up, w_dn, *, tm, gdiv):
    n, d = x.shape
    ff = w_dn.shape[0]
    nj = 2 if (ff // 2) % LANES == 0 else 1
    tf = ff // nj
    r = mod.shape[2]
    return pl.pallas_call(
        _ffn_kernel,
        grid=(n // tm, nj),
        in_specs=[
            pl.BlockSpec((tm, d), lambda i, j: (i, 0)),
            _mod_spec(r, d, gdiv, comp0),
            _mod_spec(r, d, gdiv, comp0 + 1),
            _mod_spec(r, d, gdiv, comp0 + 2),
            pl.BlockSpec((1, d), lambda i, j: (0, 0)),
            pl.BlockSpec((1, d), lambda i, j: (0, 0)),
            pl.BlockSpec((d, tf), lambda i, j: (0, j)),
            pl.BlockSpec((d, tf), lambda i, j: (0, nj + j)),
            pl.BlockSpec((tf, d), lambda i, j: (j, 0)),
        ],
        out_specs=pl.BlockSpec((tm, d), lambda i, j: (i, 0)),
        out_shape=jax.ShapeDtypeStruct((n, d), F32),
        scratch_shapes=[pltpu.VMEM((tm, d), BF16), pltpu.VMEM((tm, d), F32)],
        compiler_params=_cparams(("arbitrary", "arbitrary")),
        name="ffn",
    )(x, mod, mod, mod, g_pre, g_post, w_up, w_up, w_dn)


def _rope(x, cos, sin, first_half):
    outs = []
    for c in range(x.shape[1] // LANES):
        xs = x[:, c * LANES:(c + 1) * LANES]
        below = pltpu.roll(xs, 32, 1)
        above = pltpu.roll(xs, LANES - 32, 1)
        outs.append(xs * cos + jnp.where(first_half, above, below) * sin)
    return outs[0] if len(outs) == 1 else jnp.concatenate(outs, axis=1)


def _proj_kernel(x_ref, shift_ref, scale_ref, gpre_ref, w_ref, bf_ref, cos_ref, sin_ref,
                 qsb, ksb, vsb, qfx, kfx, vfx, qds, kds, vds, qidx, kidx2, misc, *state_t):
    h = _norm_mod(x_ref[...], gpre_ref[...], scale_ref[0, 0], shift_ref[0, 0]).astype(BF16)
    cos = cos_ref[...]
    sin = sin_ref[...]
    lane = lax.broadcasted_iota(jnp.int32, (1, LANES), 1)
    first_half = (lane & (HEAD_DIM - 1)) < (HEAD_DIM // 2)
    w3 = 3 * MIX_W

    def put_kv(k, v, k_ref, v_ref, branch):
        k_ref[...] = k.astype(k_ref.dtype)
        v_ref[...] = v.astype(v_ref.dtype)
        if state_t:
            state_t[2 * branch][...] = k.T
            state_t[2 * branch + 1][...] = v.T

    y = _dot(h, w_ref[:, 0:w3])
    qsb[...] = (y[:, 0:MIX_W] * QK_SCALE).astype(BF16)
    put_kv(y[:, MIX_W:2 * MIX_W], y[:, 2 * MIX_W:w3], ksb, vsb, 0)

    y = _dot(h, w_ref[:, w3:2 * w3])
    qfx[...] = (y[:, 0:MIX_W] * QK_SCALE).astype(BF16)
    put_kv(y[:, MIX_W:2 * MIX_W], y[:, 2 * MIX_W:w3], kfx, vfx, 1)

    y = _dot(h, w_ref[:, 2 * w3:3 * w3])
    qds[...] = (_rope(y[:, 0:MIX_W], cos, sin, first_half) * QK_SCALE).astype(BF16)
    put_kv(_rope(y[:, MIX_W:2 * MIX_W], cos, sin, first_half), y[:, 2 * MIX_W:w3], kds, vds, 2)

    y = _dot(h, w_ref[:, 3 * w3:PROJ_W])
    nq = IDX_HEADS * IDX_DIM
    qidx[...] = (_rope(y[:, 0:nq], cos, sin, first_half) * IDX_SCALE).astype(BF16)
    m = y[:, nq:nq + LANES]
    m_rot = _rope(m, cos, sin, first_half)
    lf = _log_sigmoid(m + bf_ref[...])
    misc[...] = jnp.where(lane < LF_LANE, m_rot,
                          jnp.where(lane < WI_LANE, lf,
                                    jnp.where(lane < WI_LANE + IDX_HEADS, m, 0.0)))
    kidx2[...] = jnp.where(lane < IDX_DIM, m_rot, pltpu.roll(m_rot, IDX_DIM, 1)).astype(BF16)


def _proj(x, mod, g_pre, w_main, bf_row, cos_t, sin_t, *, tm, gdiv, batch_t=None):
    n, d = x.shape
    r = mod.shape[2]
    tab_tiles = cos_t.shape[0] // tm
    row = lambda w: pl.BlockSpec((tm, w), lambda i: (i, 0))
    tab = pl.BlockSpec((tm, LANES), lambda i: (i % tab_tiles, 0))
    shp = lambda w, dt: jax.ShapeDtypeStruct((n, w), dt)
    nq = IDX_HEADS * IDX_DIM
    kv_dt = F32 if batch_t is None else BF16
    out_specs = [row(MIX_W)] * 9 + [row(nq), row(LANES), row(LANES)]
    out_shape = [shp(MIX_W, BF16), shp(MIX_W, kv_dt), shp(MIX_W, kv_dt)] * 3 + [
        shp(nq, BF16), shp(LANES, BF16), shp(LANES, F32)]
    if batch_t is not None:
        t = n // batch_t
        out_specs += [pl.BlockSpec((None, MIX_W, tm), lambda i: (i // gdiv, 0, i % gdiv))] * 6
        out_shape += [jax.ShapeDtypeStruct((batch_t, MIX_W, t), F32)] * 6
    return pl.pallas_call(
        _proj_kernel,
        grid=(n // tm,),
        in_specs=[
            row(d),
            _mod_spec(r, d, gdiv, 3),
            _mod_spec(r, d, gdiv, 4),
            pl.BlockSpec((1, d), lambda i: (0, 0)),
            pl.BlockSpec((d, PROJ_W), lambda i: (0, 0)),
            pl.BlockSpec((1, LANES), lambda i: (0, 0)),
            tab, tab,
        ],
        out_specs=out_specs,
        out_shape=out_shape,
        compiler_params=_cparams(("arbitrary",)),
        name="proj",
    )(x, mod, mod, g_pre, w_main, bf_row, cos_t, sin_t)


def _merge_kernel(x_ref, shift_ref, scale_ref, gate_ref, gpre_ref, gpost_ref,
                  osb_ref, ofx_ref, ods_ref, wg_ref, wb_ref, wo_ref, o_ref):
    x = x_ref[...]
    d = x.shape[1]
    h = _norm_mod(x, gpre_ref[...], scale_ref[0, 0], shift_ref[0, 0]).astype(BF16)
    merged = None
    for nb, o_br in enumerate((osb_ref, ofx_ref, ods_ref)):
        gl = _dot(h, wg_ref[:, nb * d:(nb + 1) * d])
        y = _dot(o_br[...], wb_ref[nb])
        t = jax.nn.sigmoid(gl) * y
        merged = t if merged is None else merged + t
    out = _dot(merged.astype(BF16), wo_ref[...])
    o_ref[...] = x + gate_ref[0, 0] * _rms(out, gpost_ref[...])


def _merge(x, mod, g_pre, g_post, o_sb, o_fx, o_ds, w_gate, w_branch, w_out, *, tm, gdiv):
    n, d = x.shape
    r = mod.shape[2]
    row = lambda w: pl.BlockSpec((tm, w), lambda i: (i, 0))
    return pl.pallas_call(
        _merge_kernel,
        grid=(n // tm,),
        in_specs=[
            row(d),
            _mod_spec(r, d, gdiv, 3),
            _mod_spec(r, d, gdiv, 4),
            _mod_spec(r, d, gdiv, 5),
            pl.BlockSpec((1, d), lambda i: (0, 0)),
            pl.BlockSpec((1, d), lambda i: (0, 0)),
            row(MIX_W), row(MIX_W), row(MIX_W),
            pl.BlockSpec((d, N_BRANCH * d), lambda i: (0, 0)),
            pl.BlockSpec((N_BRANCH, MIX_W, d), lambda i: (0, 0, 0)),
            pl.BlockSpec((d, d), lambda i: (0, 0)),
        ],
        out_specs=row(d),
        out_shape=jax.ShapeDtypeStruct((n, d), F32),
        compiler_params=_cparams(("arbitrary",)),
        name="merge",
    )(x, mod, mod, mod, g_pre, g_post, o_sb, o_fx, o_ds, w_gate, w_branch, w_out)


def _aug_lanes(lane, base, ones_first, parts):
    one_lo, val_lo = (base, base + 3) if ones_first else (base + 3, base)
    out = jnp.where((lane >= one_lo) & (lane < one_lo + 3), 1.0, 0.0)
    for k, p in enumerate(parts):
        out = jnp.where(lane == val_lo + k, p, out)
    return out


def _fox_prep_kernel(q_ref, k_ref, misc_ref, qa_ref, ka_ref, carry):
    j = pl.program_id(1)

    @pl.when(j == 0)
    def _():
        carry[...] = jnp.zeros_like(carry)

    lf = misc_ref[...]
    tm = lf.shape[0]
    low = _strict_lower(tm)
    p1, p2, p3 = _split3(lf)
    g = _dot(low, p1) + _dot(low, p2) + _dot(low, p3) + carry[...]
    carry[...] += jnp.sum(lf, axis=0, keepdims=True)

    lane = lax.broadcasted_iota(jnp.int32, (1, LANES), 1)
    lo_half = lane < HEAD_DIM
    for p in range(N_PAIR):
        qp = q_ref[:, p * LANES:(p + 1) * LANES].astype(F32)
        kp = k_ref[:, p * LANES:(p + 1) * LANES]
        for a in range(2):
            hd = 2 * p + a
            gcol = g[:, LF_LANE + hd:LF_LANE + hd + 1]
            g1, g2, g3 = (t.astype(F32) for t in _split3(gcol))
            own = lo_half if a == 0 else jnp.logical_not(lo_half)
            base = HEAD_DIM if a == 0 else 0
            q_aug = jnp.where(own, qp, _aug_lanes(lane, base, True, (-g1, -g2, -g3)))
            k_aug = jnp.where(own, kp, _aug_lanes(lane, base, False, (g1, g2, g3)))
            qa_ref[:, hd * LANES:(hd + 1) * LANES] = q_aug.astype(BF16)
            ka_ref[:, hd * LANES:(hd + 1) * LANES] = k_aug.astype(BF16)


def _fox_prep(q, k, misc, *, tm):
    b, t, _ = q.shape
    nt = t // tm
    rev = lambda w: pl.BlockSpec((None, tm, w), lambda bb, j: (bb, nt - 1 - j, 0))
    return pl.pallas_call(
        _fox_prep_kernel,
        grid=(b, nt),
        in_specs=[rev(MIX_W), rev(MIX_W), rev(LANES)],
        out_specs=[rev(N_HEADS * LANES), rev(N_HEADS * LANES)],
        out_shape=[jax.ShapeDtypeStruct((b, t, N_HEADS * LANES), BF16)] * 2,
        scratch_shapes=[pltpu.VMEM((1, LANES), F32)],
        compiler_params=_cparams(("arbitrary", "arbitrary")),
        name="fox_prep",
    )(q, k, misc)


def _sb_block(qh, kblk, vblk, mask, upper, acc, c, kv_t=False):
    z = _dot(qh, kblk) if kv_t else _dot_nt(qh, kblk)
    neg_abs = lax.bitcast_convert_type(lax.bitcast_convert_type(z, jnp.int32) | INT_MIN, F32)
    ls = jnp.minimum(z, 0.0) - jnp.log(1.0 + jnp.exp(neg_abs))
    lk = ls - z
    if mask is not None:
        lk = jnp.where(mask, lk, 0.0)
    hi, lo = _split2(lk)
    sub = upper.shape[0]
    tot = []
    for j in range(z.shape[1] // sub - 1, -1, -1):
        sl = slice(j * sub, (j + 1) * sub)
        tot.append(_dot(hi[:, sl], upper) + _dot(lo[:, sl], upper) + c)
        c = c + jnp.sum(lk[:, sl], axis=-1, keepdims=True)
    tot = tot[0] if len(tot) == 1 else jnp.concatenate(tot[::-1], axis=1)
    a = jnp.exp(ls + tot)
    if mask is not None:
        a = jnp.where(mask, a, 0.0)
    a = a.astype(BF16)
    acc = acc + (_dot_nt(a, vblk) if kv_t else _dot(a, vblk))
    return acc, c


def _sb_kernel(q_ref, k_ref, v_ref, o_ref, *, tq, tk):
    i = pl.program_id(2)
    q = q_ref[...]
    lane = lax.broadcasted_iota(jnp.int32, (1, LANES), 1)
    lo_half = lane < HEAD_DIM
    zero = jnp.zeros_like(q)
    q2 = jnp.concatenate([jnp.where(lo_half, q, zero), jnp.where(lo_half, zero, q)], axis=0)
    qpos = i * tq + (lax.broadcasted_iota(jnp.int32, (2 * tq, 1), 0) & (tq - 1))
    upper = _strict_upper(LANES)
    nkb = jnp.right_shift((i + 1) * tq + (tk - 1), tk.bit_length() - 1)

    def body(it, carry):
        acc, c = carry
        start = pl.multiple_of((nkb - 1 - it) * tk, tk)
        kblk = k_ref[pl.ds(start, tk), :].astype(BF16)
        vblk = v_ref[pl.ds(start, tk), :].astype(BF16)
        kpos = start + lax.broadcasted_iota(jnp.int32, (1, tk), 1)
        return _sb_block(q2, kblk, vblk, kpos < qpos, upper, acc, c)

    acc, _ = lax.fori_loop(0, nkb, body, (jnp.zeros((2 * tq, LANES), F32), jnp.zeros((2 * tq, 1), F32)))
    o_ref[...] = jnp.where(lo_half, acc[0:tq], acc[tq:2 * tq]).astype(o_ref.dtype)


def _sb_prompt(q, k, v, *, tq, tk):
    b, t, _ = q.shape
    return pl.pallas_call(
        functools.partial(_sb_kernel, tq=tq, tk=tk),
        grid=(b, N_PAIR, t // tq),
        in_specs=[
            pl.BlockSpec((None, tq, LANES), lambda bb, p, i: (bb, i, p)),
            pl.BlockSpec((None, t, LANES), lambda bb, p, i: (bb, 0, p)),
            pl.BlockSpec((None, t, LANES), lambda bb, p, i: (bb, 0, p)),
        ],
        out_specs=pl.BlockSpec((None, tq, LANES), lambda bb, p, i: (bb, i, p)),
        out_shape=jax.ShapeDtypeStruct((b, t, MIX_W), BF16),
        compiler_params=_cparams(("arbitrary", "arbitrary", "arbitrary")),
        name="sb_prompt",
    )(q, k, v)


def _softmax_step(s, vblk, m, l, acc, v_t=False):
    m_new = jnp.maximum(m, jnp.max(s, axis=-1, keepdims=True))
    alpha = jnp.exp(m - m_new)
    p = jnp.exp(s - m_new)
    l = alpha * l + jnp.sum(p, axis=-1, keepdims=True)
    p = p.astype(BF16)
    acc = alpha * acc + (_dot_nt(p, vblk) if v_t else _dot(p, vblk))
    return m_new, l, acc


def _fox_kernel(qa_ref, ka_ref, v_ref, o_ref, s_scr, p_scr, *, tq, tk):
    i = pl.program_id(2)
    qa = qa_ref[...]
    qpos = i * tq + lax.broadcasted_iota(jnp.int32, (tq, 1), 0)
    nkb = jnp.right_shift((i + 1) * tq + (tk - 1), tk.bit_length() - 1)
    n_pairs = jnp.right_shift(nkb + 1, 1)
    last_blk = v_ref.shape[0] // tk - 1

    def block_start(kb):
        return pl.multiple_of(jnp.clip(kb, 0, last_blk) * tk, tk)

    def scores(kb, slot):
        kblk = ka_ref[pl.ds(block_start(kb), tk), :]
        mask = (kb * tk + lax.broadcasted_iota(jnp.int32, (1, tk), 1)) <= qpos
        for a in range(2):
            s = _dot_nt(qa[:, a * LANES:(a + 1) * LANES], kblk[:, a * LANES:(a + 1) * LANES])
            s_scr[slot, a] = jnp.where(mask, s, -jnp.inf)

    def weighted_values(kb, slot, alpha, acc):
        v = v_ref[pl.ds(block_start(kb), tk), :].astype(BF16)
        return [alpha[a] * acc[a] + _dot(p_scr[slot, a], v) for a in range(2)]

    def softmax(slot, m, l):
        alpha, m_out, l_out = [], [], []
        rs = min(SOFTMAX_ROWS, tq)
        for a in range(2):
            al_c, m_c, l_c, p_c = [], [], [], []
            for r in range(tq // rs):
                rows = slice(r * rs, (r + 1) * rs)
                s = s_scr[slot, a, rows, :]
                m_new = jnp.maximum(m[a][rows], jnp.max(s, axis=-1, keepdims=True))
                al = jnp.exp(m[a][rows] - m_new)
                p = jnp.exp(s - m_new)
                p_c.append(p.astype(BF16))
                al_c.append(al)
                m_c.append(m_new)
                l_c.append(al * l[a][rows] + jnp.sum(p, axis=-1, keepdims=True))
            cat = lambda xs: xs[0] if len(xs) == 1 else jnp.concatenate(xs, axis=0)
            p_scr[slot, a] = cat(p_c)
            alpha.append(cat(al_c))
            m_out.append(cat(m_c))
            l_out.append(cat(l_c))
        return alpha, m_out, l_out

    def body(jj, carry):
        alpha, m, l, acc = carry
        b0 = 2 * jj
        scores(b0 + 1, 1)
        acc = weighted_values(b0 - 1, 1, alpha, acc)
        alpha, m, l = softmax(0, m, l)
        scores(b0 + 2, 0)
        acc = weighted_values(b0, 0, alpha, acc)
        alpha, m, l = softmax(1, m, l)
        return alpha, m, l, acc

    scores(0, 0)
    p_scr[1] = jnp.zeros(p_scr.shape[1:], BF16)
    two = lambda x: [x, x]
    init = (two(jnp.ones((tq, 1), F32)), two(jnp.full((tq, 1), NEG_BIG, F32)), two(jnp.zeros((tq, 1), F32)),
            two(jnp.zeros((tq, LANES), F32)))
    alpha, _, l, acc = lax.fori_loop(0, n_pairs, body, init)
    acc = weighted_values(2 * n_pairs - 1, 1, alpha, acc)
    lane = lax.broadcasted_iota(jnp.int32, (1, LANES), 1)
    o_ref[...] = jnp.where(lane < HEAD_DIM, acc[0] / l[0], acc[1] / l[1]).astype(o_ref.dtype)


def _fox_prompt(qa, ka, v, *, tq, tk):
    b, t, _ = v.shape
    return pl.pallas_call(
        functools.partial(_fox_kernel, tq=tq, tk=tk),
        grid=(b, N_PAIR, t // tq),
        in_specs=[
            pl.BlockSpec((None, tq, 2 * LANES), lambda bb, p, i: (bb, i, p)),
            pl.BlockSpec((None, t, 2 * LANES), lambda bb, p, i: (bb, 0, p)),
            pl.BlockSpec((None, t, LANES), lambda bb, p, i: (bb, 0, p)),
        ],
        out_specs=pl.BlockSpec((None, tq, LANES), lambda bb, p, i: (bb, i, p)),
        out_shape=jax.ShapeDtypeStruct((b, t, MIX_W), BF16),
        scratch_shapes=[pltpu.VMEM((2, 2, tq, tk), F32), pltpu.VMEM((2, 2, tq, tk), BF16)],
        compiler_params=_cparams(("arbitrary", "arbitrary", "arbitrary")),
        name="fox_prompt",
    )(qa, ka, v)


def _sort_key(x):
    bits = lax.bitcast_convert_type(x, jnp.int32)
    key = jnp.where(bits < 0, bits ^ jnp.int32(0x7FFFFFFF), bits)
    return jnp.where(x == 0.0, 0, key)


def _count(pred_parts):
    tot = None
    for preds in pred_parts:
        ind = jnp.where(preds[-1], 1.0, 0.0)
        for p in preds[-2::-1]:
            ind = jnp.where(p, ind, 0.0)
        c = jnp.sum(ind, axis=-1, keepdims=True)
        tot = c if tot is None else tot + c
    return tot


def _topk_select(key_gets, pos_list, rows, topk, idx_bits):
    kf = float(topk)

    def vbody(it, thr):
        cand = thr + jnp.left_shift(jnp.int32(1), 31 - it)
        cnt = _count([(kg() >= cand,) for kg in key_gets])
        return jnp.where(cnt >= kf, cand, thr)

    thr = lax.fori_loop(0, 32, vbody, jnp.full((rows, 1), INT_MIN, jnp.int32))
    need = kf - _count([(kg() > thr,) for kg in key_gets])
    n_eq = _count([(kg() == thr,) for kg in key_gets])

    def ibody(it, bound):
        cand = bound + jnp.left_shift(jnp.int32(1), idx_bits - 1 - it)
        cnt = _count([(kg() == thr, pos < cand) for kg, pos in zip(key_gets, pos_list)])
        return jnp.where(cnt <= need, cand, bound)

    some_partial = jnp.max(jnp.where(n_eq > need, 1.0, 0.0)) > 0.0
    bound = lax.cond(
        some_partial,
        lambda: lax.fori_loop(0, idx_bits, ibody, jnp.zeros((rows, 1), jnp.int32)),
        lambda: jnp.full((rows, 1), 1 << idx_bits, jnp.int32))
    tie = jnp.where(thr > KEY_NEG_INF, 0.0, -jnp.inf)
    out = []
    for kg, pos in zip(key_gets, pos_list):
        key = kg()
        at_thr = jnp.where(pos < bound, tie, -jnp.inf)
        out.append(jnp.where(key > thr, 0.0, jnp.where(key == thr, at_thr, -jnp.inf)))
    return out


def _dsa_body(i, n_keys, qidx_ref, kidx2_ref, miscq_ref, q_ref, k_ref, v_ref, o_ref, key_scr, bias_scr, *, tq, topk):
    lane = lax.broadcasted_iota(jnp.int32, (1, LANES), 1)
    lo_half = lane < HEAD_DIM
    kidx2 = kidx2_ref[0:n_keys, :]
    wq = miscq_ref[...]
    isc = jnp.zeros((tq, n_keys), F32)
    for h in range(IDX_HEADS):
        qp = qidx_ref[:, (h // 2) * LANES:(h // 2 + 1) * LANES]
        own = lo_half if h % 2 == 0 else jnp.logical_not(lo_half)
        qh = jnp.where(own, qp, jnp.zeros_like(qp))
        s = _dot_nt(qh, kidx2)
        isc = isc + jnp.maximum(s, 0.0) * wq[:, WI_LANE + h:WI_LANE + h + 1]
    isc = isc * IDX_HEAD_SCALE
    qpos = i * tq + lax.broadcasted_iota(jnp.int32, (tq, 1), 0)
    kpos = lax.broadcasted_iota(jnp.int32, (1, n_keys), 1)
    allowed = jnp.right_shift(kpos, CHUNK_SHIFT) <= jnp.right_shift(qpos, CHUNK_SHIFT)
    isc = jnp.where(allowed, isc, -jnp.inf)
    key_scr[:, 0:n_keys] = _sort_key(isc)
    (bias,) = _topk_select([lambda: key_scr[:, 0:n_keys]], [kpos], tq, topk, int(n_keys).bit_length())
    bias_scr[:, 0:n_keys] = bias

    for p in range(N_PAIR):
        qp = q_ref[:, p * LANES:(p + 1) * LANES]
        kp = k_ref[0:n_keys, p * LANES:(p + 1) * LANES].astype(BF16)
        vp = v_ref[0:n_keys, p * LANES:(p + 1) * LANES].astype(BF16)
        outs = []
        for a in range(2):
            own = lo_half if a == 0 else jnp.logical_not(lo_half)
            qh = jnp.where(own, qp, jnp.zeros_like(qp))
            s = _dot_nt(qh, kp) + bias_scr[:, 0:n_keys]
            m = jnp.max(s, axis=-1, keepdims=True)
            e = jnp.exp(s - m)
            l = jnp.sum(e, axis=-1, keepdims=True)
            outs.append(_dot(e.astype(BF16), vp) / l)
        o_ref[:, p * LANES:(p + 1) * LANES] = jnp.where(lo_half, outs[0], outs[1]).astype(o_ref.dtype)


def _dsa_kernel(*refs, tq, topk, n_groups):
    i = pl.program_id(1)
    t = refs[4].shape[0]
    glen = t // n_groups
    group = jnp.right_shift(i * tq, glen.bit_length() - 1)
    for g in range(n_groups):
        @pl.when(group == g)
        def _():
            _dsa_body(i, (g + 1) * glen, *refs, tq=tq, topk=topk)


def _dsa_prompt(qidx, kidx2, misc, q, k, v, *, tq):
    b, t, _ = q.shape
    topk = min(TOPK_MAX, t // 4)
    glen = t // 4
    n_groups = 4 if (glen % tq == 0 and glen >= topk) else 1
    qrow = lambda w: pl.BlockSpec((None, tq, w), lambda bb, i: (bb, i, 0))
    full = lambda w: pl.BlockSpec((None, t, w), lambda bb, i: (bb, 0, 0))
    return pl.pallas_call(
        functools.partial(_dsa_kernel, tq=tq, topk=topk, n_groups=n_groups),
        grid=(b, t // tq),
        in_specs=[qrow(IDX_HEADS * IDX_DIM), full(LANES), qrow(LANES), qrow(MIX_W), full(MIX_W), full(MIX_W)],
        out_specs=qrow(MIX_W),
        out_shape=jax.ShapeDtypeStruct((b, t, MIX_W), BF16),
        scratch_shapes=[pltpu.VMEM((tq, t), jnp.int32), pltpu.VMEM((tq, t), F32)],
        compiler_params=_cparams(("arbitrary", "arbitrary")),
        name="dsa_prompt",
    )(qidx, kidx2, misc, q, k, v)


def _head_rows(x, n_rep):
    return jnp.concatenate([x] * n_rep, axis=0)


def _block_diag_q(q, tq):
    rows = N_HEADS * tq
    rhead = jnp.right_shift(lax.broadcasted_iota(jnp.int32, (rows, 1), 0), tq.bit_length() - 1)
    lhead = jnp.right_shift(lax.broadcasted_iota(jnp.int32, (1, MIX_W), 1), HEAD_DIM.bit_length() - 1)
    qq = _head_rows(q, N_HEADS)
    return jnp.where(rhead == lhead, qq, jnp.zeros_like(qq)), rhead == lhead


def _diag_out(o_bd, diag, tq):
    o = jnp.where(diag, o_bd, 0.0)
    out = o[0:tq]
    for h in range(1, N_HEADS):
        out = out + o[h * tq:(h + 1) * tq]
    return out


def _pad_rows(x, rows):
    return jnp.concatenate([x, jnp.zeros((rows - x.shape[0], x.shape[1]), x.dtype)], axis=0)


def _sb_sample_kernel(q_ref, kn_ref, vn_ref, kp_ref, vp_ref, o_ref, *, tk):
    tq = q_ref.shape[0]
    past = kp_ref.shape[1]
    rows = N_HEADS * tq
    qbd, diag = _block_diag_q(q_ref[...], tq)
    trow = lax.broadcasted_iota(jnp.int32, (rows, 1), 0) & (tq - 1)
    kn = _pad_rows(kn_ref[...], LANES).astype(BF16)
    vn = _pad_rows(vn_ref[...], LANES).astype(BF16)
    mask_n = lax.broadcasted_iota(jnp.int32, (1, LANES), 1) < trow
    acc = jnp.zeros((rows, MIX_W), F32)
    c = jnp.zeros((rows, 1), F32)
    upper = _strict_upper(LANES)
    acc, c = _sb_block(qbd, kn, vn, mask_n, upper, acc, c)
    for kb in range(past // tk - 1, -1, -1):
        kblk = kp_ref[:, kb * tk:(kb + 1) * tk].astype(BF16)
        vblk = vp_ref[:, kb * tk:(kb + 1) * tk].astype(BF16)
        acc, c = _sb_block(qbd, kblk, vblk, None, upper, acc, c, kv_t=True)
    o_ref[...] = _diag_out(acc, diag, tq).astype(o_ref.dtype)


def _fox_sample_kernel(q_ref, kn_ref, vn_ref, miscn_ref, kp_ref, vp_ref, lfp_ref, o_ref, *, tk):
    tq = q_ref.shape[0]
    past = kp_ref.shape[1]
    rows = N_HEADS * tq
    hp = lfp_ref.shape[0]
    qbd, diag = _block_diag_q(q_ref[...], tq)
    trow = lax.broadcasted_iota(jnp.int32, (rows, 1), 0) & (tq - 1)
    lane = lax.broadcasted_iota(jnp.int32, (1, LANES), 1)

    misc_n = _pad_rows(miscn_ref[...], LANES)
    pick = jnp.where(lax.broadcasted_iota(jnp.int32, (hp, LANES), 1)
                     == LF_LANE + lax.broadcasted_iota(jnp.int32, (hp, LANES), 0), 1.0, 0.0).astype(BF16)
    lf_new = sum(_dot_nt(pick, part) for part in _split3(misc_n))
    up_n = _strict_upper(LANES)
    g_new = sum(_dot(part, up_n) for part in _split3(lf_new))
    tot_new = jnp.sum(lf_new, axis=-1, keepdims=True)

    def expand(g):
        return jnp.concatenate([jnp.broadcast_to(g[h:h + 1], (tq, g.shape[1])) for h in range(N_HEADS)], axis=0)

    g_new_x = expand(g_new)
    g_q = jnp.sum(jnp.where(lane == trow, g_new_x, 0.0), axis=-1, keepdims=True)

    kn = _pad_rows(kn_ref[...], LANES).astype(BF16)
    vn = _pad_rows(vn_ref[...], LANES).astype(BF16)
    s_n = _dot_nt(qbd, kn) + g_new_x - g_q
    s_n = jnp.where(lane <= trow, s_n, -jnp.inf)
    m, l, acc = _softmax_step(s_n, vn, jnp.full((rows, 1), NEG_BIG, F32), jnp.zeros((rows, 1), F32),
                              jnp.zeros((rows, MIX_W), F32))
    upper = _strict_upper(tk)
    carry = tot_new
    for kb in range(past // tk - 1, -1, -1):
        lf_blk = lfp_ref[:, kb * tk:(kb + 1) * tk]
        g_blk = sum(_dot(part, upper) for part in _split3(lf_blk)) + carry
        carry = carry + jnp.sum(lf_blk, axis=-1, keepdims=True)
        kblk = kp_ref[:, kb * tk:(kb + 1) * tk].astype(BF16)
        vblk = vp_ref[:, kb * tk:(kb + 1) * tk].astype(BF16)
        s = _dot(qbd, kblk) + expand(g_blk) - g_q
        m, l, acc = _softmax_step(s, vblk, m, l, acc, v_t=True)
    o_ref[...] = _diag_out(acc / l, diag, tq).astype(o_ref.dtype)


def _dsa_sample_kernel(qidx_ref, miscn_ref, kin_ref, q_ref, kn_ref, vn_ref, kip_ref, kp_ref, vp_ref, o_ref,
                       keyp_scr, keyn_scr, *, topk, pos0):
    tq = q_ref.shape[0]
    past = kp_ref.shape[1]
    lane = lax.broadcasted_iota(jnp.int32, (1, LANES), 1)
    qi = qidx_ref[...].astype(F32)
    qs = jnp.concatenate([qi[:, h * IDX_DIM:(h + 1) * IDX_DIM] for h in range(IDX_HEADS)], axis=0).astype(BF16)
    wq = miscn_ref[...]
    ws = jnp.concatenate([wq[:, WI_LANE + h:WI_LANE + h + 1] for h in range(IDX_HEADS)], axis=0)
    kip = kip_ref[...].astype(BF16)
    kin = _pad_rows(kin_ref[...][:, 0:IDX_DIM], LANES).astype(BF16)

    def head_sum(s):
        s = jnp.maximum(s, 0.0) * ws
        out = s[0:tq]
        for h in range(1, IDX_HEADS):
            out = out + s[h * tq:(h + 1) * tq]
        return out * IDX_HEAD_SCALE

    isc_p = head_sum(_dot(qs, kip))
    isc_n = head_sum(_dot_nt(qs, kin))
    qchunk = jnp.right_shift(pos0 + lax.broadcasted_iota(jnp.int32, (tq, 1), 0), CHUNK_SHIFT)
    ppos = lax.broadcasted_iota(jnp.int32, (1, past), 1)
    npos = pos0 + lane
    ok_p = jnp.right_shift(ppos, CHUNK_SHIFT) <= qchunk
    ok_n = (jnp.right_shift(npos, CHUNK_SHIFT) <= qchunk) & (lane < tq)
    isc_p = jnp.where(ok_p, isc_p, -jnp.inf)
    isc_n = jnp.where(ok_n, isc_n, -jnp.inf)
    keyp_scr[...] = _sort_key(isc_p)
    keyn_scr[...] = jnp.where(lane < tq, _sort_key(isc_n), INT_MIN)
    bias_p, bias_n = _topk_select([lambda: keyp_scr[...], lambda: keyn_scr[...]], [ppos, npos], tq, topk,
                                  int(past + LANES).bit_length())
    bias_p = _head_rows(bias_p, N_HEADS)
    bias_n = _head_rows(bias_n, N_HEADS)

    qbd, diag = _block_diag_q(q_ref[...], tq)
    kn = _pad_rows(kn_ref[...], LANES).astype(BF16)
    vn = _pad_rows(vn_ref[...], LANES).astype(BF16)
    s_p = _dot(qbd, kp_ref[...].astype(BF16)) + bias_p
    s_n = _dot_nt(qbd, kn) + bias_n
    m = jnp.maximum(jnp.max(s_p, axis=-1, keepdims=True), jnp.max(s_n, axis=-1, keepdims=True))
    e_p = jnp.exp(s_p - m)
    e_n = jnp.exp(s_n - m)
    l = jnp.sum(e_p, axis=-1, keepdims=True) + jnp.sum(e_n, axis=-1, keepdims=True)
    o = (_dot_nt(e_p.astype(BF16), vp_ref[...].astype(BF16)) + _dot(e_n.astype(BF16), vn)) / l
    o_ref[...] = _diag_out(o, diag, tq).astype(o_ref.dtype)


def _sample_attention(layer, q_sb, k_sb, v_sb, q_fx, k_fx, v_fx, q_ds, k_ds, v_ds, qidx, misc,
                      c_sb_k, c_sb_v, c_fx_k, c_fx_v, lf_t, c_ds_k, c_ds_v, c_kidx):
    b, tq, _ = q_sb.shape
    past = c_sb_k.shape[3]
    tk = min(256, past)
    new = lambda w: pl.BlockSpec((None, tq, w), lambda bb: (bb, 0, 0))
    cache = lambda w: pl.BlockSpec((None, None, w, past), lambda bb: (layer, bb, 0, 0))
    out = jax.ShapeDtypeStruct((b, tq, MIX_W), BF16)
    cp = _cparams(("arbitrary",))
    o_sb = pl.pallas_call(
        functools.partial(_sb_sample_kernel, tk=tk), grid=(b,),
        in_specs=[new(MIX_W), new(MIX_W), new(MIX_W), cache(MIX_W), cache(MIX_W)],
        out_specs=new(MIX_W), out_shape=out, compiler_params=cp, name="sb_sample",
    )(q_sb, k_sb, v_sb, c_sb_k, c_sb_v)
    hp = lf_t.shape[2]
    o_fx = pl.pallas_call(
        functools.partial(_fox_sample_kernel, tk=tk), grid=(b,),
        in_specs=[new(MIX_W), new(MIX_W), new(MIX_W), new(LANES), cache(MIX_W), cache(MIX_W),
                  cache(hp)],
        out_specs=new(MIX_W), out_shape=out, compiler_params=cp, name="fox_sample",
    )(q_fx, k_fx, v_fx, misc, c_fx_k, c_fx_v, lf_t)
    topk = min(TOPK_MAX, (past + tq) // 4)
    o_ds = pl.pallas_call(
        functools.partial(_dsa_sample_kernel, topk=topk, pos0=past), grid=(b,),
        in_specs=[new(IDX_HEADS * IDX_DIM), new(LANES), new(LANES), new(MIX_W), new(MIX_W), new(MIX_W),
                  cache(IDX_DIM), cache(MIX_W), cache(MIX_W)],
        out_specs=new(MIX_W), out_shape=out,
        scratch_shapes=[pltpu.VMEM((tq, past), jnp.int32), pltpu.VMEM((tq, LANES), jnp.int32)],
        compiler_params=cp, name="dsa_sample",
    )(qidx, misc, misc, q_ds, k_ds, v_ds, c_kidx, c_ds_k, c_ds_v)
    return o_sb, o_fx, o_ds


def _rope_tables(pos):
    half = HEAD_DIM // 2
    inv_freq = ROPE_THETA ** (-jnp.arange(half, dtype=F32) / half)
    ang = pos.astype(F32)[:, None] * inv_freq[None, :]
    cos = jnp.cos(ang)
    sin = jnp.sin(ang)
    cos_t = jnp.concatenate([cos, cos, cos, cos], axis=1)
    sin_t = jnp.concatenate([-sin, sin, -sin, sin], axis=1)
    return cos_t, sin_t


def _prep_weights(w_in, b_forget, ffn_w_in, ffn_w_out, w_branch, w_out):
    w3 = 3 * MIX_W
    o = np.cumsum([0, w3, w3, N_HEADS, w3, IDX_HEADS * IDX_DIM, IDX_DIM, IDX_HEADS]).tolist()
    sb, fx, fl, ds, qi, ki, wi = (w_in[:, :, o[k]:o[k + 1]] for k in range(7))
    pad = jnp.zeros(w_in.shape[:2] + (LANES - IDX_DIM - N_HEADS - IDX_HEADS,), w_in.dtype)
    w_main = jnp.concatenate([sb, fx, ds, qi, ki, fl, wi, pad], axis=2).astype(BF16)
    w_gate = w_in[:, :, o[7]:].astype(BF16)
    depth = w_in.shape[0]
    bf_row = jnp.zeros((depth, 1, LANES), F32).at[:, 0, LF_LANE:LF_LANE + N_HEADS].set(b_forget)
    return (w_main, w_gate, bf_row, ffn_w_in.astype(BF16), ffn_w_out.astype(BF16),
            w_branch.astype(BF16), w_out.astype(BF16))


def _trunk(x3, mod_l, gdiv, tm, pos, caches, norm_g, weights, *, tq_attn):
    w_main, w_gate, bf_row, w_up, w_dn, w_branch, w_out = weights
    b, t, d = x3.shape
    n = b * t
    depth = w_main.shape[0]
    x = x3.reshape(n, d)
    cos_t, sin_t = _rope_tables(pos)
    if caches is not None:
        cos_t = jnp.tile(cos_t, (tm // t, 1))
        sin_t = jnp.tile(sin_t, (tm // t, 1))
        lf_t = jnp.swapaxes(caches[4], 2, 3)
        lf_t = jnp.pad(lf_t, ((0, 0), (0, 0), (0, 16 - lf_t.shape[2]), (0, 0)))
    states = []
    for l in range(depth):
        g = lambda k: norm_g[l, k][None, :]
        mod = mod_l[l]
        x = _ffn(x, mod, 0, g(0), g(1), w_up[l, 0], w_dn[l, 0], tm=tm, gdiv=gdiv)
        (q_sb, k_sb, v_sb, q_fx, k_fx, v_fx, q_ds, k_ds, v_ds, qidx, kidx2, misc, *state_t) = _proj(
            x, mod, g(2), w_main[l], bf_row[l], cos_t, sin_t, tm=tm, gdiv=gdiv,
            batch_t=b if caches is None else None)
        r3 = lambda a: a.reshape(b, t, a.shape[-1])
        if caches is None:
            qa, ka = _fox_prep(r3(q_fx), r3(k_fx), r3(misc), tm=min(256, t))
            o_sb = _sb_prompt(r3(q_sb), r3(k_sb), r3(v_sb), tq=tq_attn, tk=min(512, t))
            o_fx = _fox_prompt(qa, ka, r3(v_fx), tq=tq_attn, tk=min(256, t))
            o_ds = _dsa_prompt(r3(qidx), r3(kidx2), r3(misc), r3(q_ds), r3(k_ds), r3(v_ds), tq=tq_attn)
        else:
            o_sb, o_fx, o_ds = _sample_attention(
                l, r3(q_sb), r3(k_sb), r3(v_sb), r3(q_fx), r3(k_fx), r3(v_fx), r3(q_ds), r3(k_ds), r3(v_ds),
                r3(qidx), r3(misc), caches[0], caches[1], caches[2], caches[3], lf_t,
                caches[5], caches[6], caches[7])
        o2 = lambda a: a.reshape(n, MIX_W)
        x = _merge(x, mod, g(2), g(3), o2(o_sb), o2(o_fx), o2(o_ds), w_gate[l], w_branch[l], w_out[l],
                   tm=tm, gdiv=gdiv)
        x = _ffn(x, mod, 6, g(4), g(5), w_up[l, 1], w_dn[l, 1], tm=tm, gdiv=gdiv)
        if state_t:
            hd = lambda a: jnp.transpose(a.reshape(b, N_HEADS, HEAD_DIM, t), (0, 3, 1, 2))
            kv = [hd(a) for a in state_t]
        else:
            hd = lambda a: a.reshape(b, t, N_HEADS, HEAD_DIM)
            kv = [hd(a) for a in (k_sb, v_sb, k_fx, v_fx, k_ds, v_ds)]
        states.append((kv[0], kv[1], kv[2], kv[3],
                       misc[:, LF_LANE:LF_LANE + N_HEADS].reshape(b, t, N_HEADS),
                       kv[4], kv[5], misc[:, 0:IDX_DIM].reshape(b, t, IDX_DIM)))
    return x.reshape(b, t, d), tuple(jnp.stack(s, axis=0) for s in zip(*states))


def kernel(x_prompt, x_sample, cache_sb_k, cache_sb_v, cache_fox_k, cache_fox_v, cache_fox_logf, cache_dsa_k, cache_dsa_v, cache_dsa_kidx, c_prompt, c_sample, norm_g, w_ada, b_ada, ffn_w_in, ffn_w_out, w_in, b_forget, w_branch, w_out):
    bp, tp, d = x_prompt.shape
    bs, ts, _ = x_sample.shape
    depth = w_in.shape[0]
    past = cache_sb_k.shape[2]
    mix = lambda a: jnp.transpose(a, (0, 1, 3, 4, 2)).reshape(a.shape[:2] + (MIX_W, a.shape[2]))
    caches = (mix(cache_sb_k), mix(cache_sb_v), mix(cache_fox_k), mix(cache_fox_v), cache_fox_logf,
              mix(cache_dsa_k), mix(cache_dsa_v), jnp.swapaxes(cache_dsa_kidx, 2, 3))
    weights = _prep_weights(w_in, b_forget, ffn_w_in, ffn_w_out, w_branch, w_out)

    rows = bp + bs
    rows_pad = -(-rows // 8) * 8
    c_all = jnp.concatenate([c_prompt, c_sample, jnp.zeros((rows_pad - rows, d), F32)], axis=0)
    mod = _ada(c_all, w_ada, b_ada)
    mod_p = [mod[l, :bp].reshape(bp, N_MOD, 1, d) for l in range(depth)]
    ns = bs * ts
    mod_s = [jnp.repeat(mod[l, bp:rows].reshape(bs, N_MOD, d), ts, axis=0)
             .reshape(1, ns, N_MOD, d).transpose(0, 2, 1, 3) for l in range(depth)]

    tm_p = min(512, tp)
    pos_p = jnp.arange(tp, dtype=jnp.int32)
    pos_s = past + jnp.arange(ts, dtype=jnp.int32)
    y_p, st_p = _trunk(x_prompt, mod_p, tp // tm_p, tm_p, pos_p, None, norm_g, weights, tq_attn=min(256, tp))
    y_s, st_s = _trunk(x_sample, mod_s, 1, ns, pos_s, caches, norm_g, weights, tq_attn=ts)
    return (y_p, y_s) + st_p + st_s
```

```python
import functools

import numpy as np
import jax
import jax.numpy as jnp
from jax import lax
from jax.experimental import pallas as pl
from jax.experimental.pallas import tpu as pltpu

F32 = jnp.float32
BF16 = jnp.bfloat16

HEAD_DIM = 64
N_HEADS = 6
MIX_W = N_HEADS * HEAD_DIM
N_PAIR = MIX_W // 128
IDX_HEADS = 4
IDX_DIM = 64
CHUNK = 64
CHUNK_SHIFT = 6
TOPK_MAX = 256
N_BRANCH = 3
N_MOD = 9
ROPE_THETA = 10000.0
EPS = 1e-6
FFN_RES = 0.5
QK_SCALE = HEAD_DIM ** -0.5
IDX_SCALE = IDX_DIM ** -0.5
IDX_HEAD_SCALE = IDX_HEADS ** -0.5
LANES = 128
LF_LANE = IDX_DIM
WI_LANE = IDX_DIM + N_HEADS
PROJ_W = 3 * 3 * MIX_W + IDX_HEADS * IDX_DIM + LANES
VMEM_LIMIT = 56 * 1024 * 1024
SAMPLE_KEYS = 2048
INT_MIN = np.int32(-2 ** 31)
KEY_NEG_INF = np.int32(-2139095041)
NEG_BIG = -1e30


def _cparams(sem):
    return pltpu.CompilerParams(dimension_semantics=sem, vmem_limit_bytes=VMEM_LIMIT)


def _dot(a, b):
    return jnp.dot(a, b, preferred_element_type=F32)


def _dot_nt(a, b):
    return lax.dot_general(a, b, (((1,), (1,)), ((), ())), preferred_element_type=F32)


def _split2(x):
    hi = x.astype(BF16)
    lo = (x - hi.astype(F32)).astype(BF16)
    return hi, lo


def _split3(x):
    p1 = x.astype(BF16)
    r1 = x - p1.astype(F32)
    p2 = r1.astype(BF16)
    p3 = (r1 - p2.astype(F32)).astype(BF16)
    return p1, p2, p3


def _log_sigmoid(x):
    return jnp.minimum(x, 0.0) - jnp.log1p(jnp.exp(-jnp.abs(x)))


def _rms(x, g):
    return x * lax.rsqrt(jnp.mean(x * x, axis=-1, keepdims=True) + EPS) * g


def _norm_mod(x, g, scale, shift):
    return _rms(x, g) * (1.0 + scale) + shift


def _strict_upper(n):
    r = lax.broadcasted_iota(jnp.int32, (n, n), 0)
    c = lax.broadcasted_iota(jnp.int32, (n, n), 1)
    return jnp.where(r > c, 1.0, 0.0).astype(BF16)


def _strict_lower(n):
    r = lax.broadcasted_iota(jnp.int32, (n, n), 0)
    c = lax.broadcasted_iota(jnp.int32, (n, n), 1)
    return jnp.where(c > r, 1.0, 0.0).astype(BF16)


def _ada_kernel(c_ref, w_ref, b_ref, o_ref):
    c = c_ref[...]
    s = (c * jax.nn.sigmoid(c)).astype(BF16)
    o_ref[0] = _dot(s, w_ref[0].astype(BF16)) + b_ref[0]


def _ada(c_all, w_ada, b_ada):
    depth, d, n = w_ada.shape
    rows = c_all.shape[0]
    tn = n // 8
    return pl.pallas_call(
        _ada_kernel,
        grid=(depth, n // tn),
        in_specs=[
            pl.BlockSpec((rows, d), lambda l, j: (0, 0)),
            pl.BlockSpec((1, d, tn), lambda l, j: (l, 0, j)),
            pl.BlockSpec((1, 1, tn), lambda l, j: (l, 0, j)),
        ],
        out_specs=pl.BlockSpec((1, rows, tn), lambda l, j: (l, 0, j)),
        out_shape=jax.ShapeDtypeStruct((depth, rows, n), F32),
        compiler_params=_cparams(("arbitrary", "arbitrary")),
        name="ada",
    )(c_all, w_ada, b_ada.reshape(depth, 1, n))


def _ffn_kernel(x_ref, shift_ref, scale_ref, gate_ref, gpre_ref, gpost_ref, wg_ref, wu_ref, wd_ref,
                o_ref, acc_scr, *, nj):
    j = pl.program_id(1)
    for jj in range(nj):
        @pl.when(j == jj)
        def _():
            x = x_ref[...]
            h = _norm_mod(x, gpre_ref[...], scale_ref[0, 0], shift_ref[0, 0]).astype(BF16)
            g = _dot(h, wg_ref[...])
            u = _dot(h, wu_ref[...])
            a = (g * jax.nn.sigmoid(g) * u).astype(BF16)
            d = _dot(a, wd_ref[...])
            acc = d if jj == 0 else acc_scr[...] + d
            if jj < nj - 1:
                acc_scr[...] = acc
            else:
                o_ref[...] = x + FFN_RES * gate_ref[0, 0] * _rms(acc, gpost_ref[...])


def _mod_spec(r, d, gdiv, comp):
    return pl.BlockSpec((1, 1, r, d), lambda i, *_: (i // gdiv, comp, 0, 0))


def _ffn(x, mod, comp0, g_pre, g_post, w_up, w_dn, *, tm, gdiv):
    n, d = x.shape
    ff = w_dn.shape[0]
    nj = 2 if (ff // 2) % LANES == 0 else 1
    tf = ff // nj
    r = mod.shape[2]
    return pl.pallas_call(
        functools.partial(_ffn_kernel, nj=nj),
        grid=(n // tm, nj),
        in_specs=[
            pl.BlockSpec((tm, d), lambda i, j: (i, 0)),
            _mod_spec(r, d, gdiv, comp0),
            _mod_spec(r, d, gdiv, comp0 + 1),
            _mod_spec(r, d, gdiv, comp0 + 2),
            pl.BlockSpec((1, d), lambda i, j: (0, 0)),
            pl.BlockSpec((1, d), lambda i, j: (0, 0)),
            pl.BlockSpec((d, tf), lambda i, j: (0, j)),
            pl.BlockSpec((d, tf), lambda i, j: (0, nj + j)),
            pl.BlockSpec((tf, d), lambda i, j: (j, 0)),
        ],
        out_specs=pl.BlockSpec((tm, d), lambda i, j: (i, 0)),
        out_shape=jax.ShapeDtypeStruct((n, d), F32),
        scratch_shapes=[pltpu.VMEM((tm, d), F32)],
        compiler_params=_cparams(("arbitrary", "arbitrary")),
        name="ffn",
    )(x, mod, mod, mod, g_pre, g_post, w_up, w_up, w_dn)


def _rope(x, cos, sin, first_half):
    outs = []
    for c in range(x.shape[1] // LANES):
        xs = x[:, c * LANES:(c + 1) * LANES]
        below = pltpu.roll(xs, 32, 1)
        above = pltpu.roll(xs, LANES - 32, 1)
        outs.append(xs * cos + jnp.where(first_half, above, below) * sin)
    return outs[0] if len(outs) == 1 else jnp.concatenate(outs, axis=1)


def _proj_kernel(x_ref, shift_ref, scale_ref, gpre_ref, w_ref, bf_ref, cos_ref, sin_ref,
                 qsb, ksb, vsb, qfx, kfx, vfx, qds, kds, vds, qidx, kidx2, misc, *state_t):
    h = _norm_mod(x_ref[...], gpre_ref[...], scale_ref[0, 0], shift_ref[0, 0]).astype(BF16)
    cos = cos_ref[...]
    sin = sin_ref[...]
    lane = lax.broadcasted_iota(jnp.int32, (1, LANES), 1)
    first_half = (lane & (HEAD_DIM - 1)) < (HEAD_DIM // 2)
    w3 = 3 * MIX_W

    def put_kv(k, v, k_ref, v_ref, branch):
        k_ref[...] = k.astype(k_ref.dtype)
        v_ref[...] = v.astype(v_ref.dtype)
        if state_t:
            state_t[2 * branch][...] = k.T
            state_t[2 * branch + 1][...] = v.T

    y = _dot(h, w_ref[:, 0:w3])
    qsb[...] = (y[:, 0:MIX_W] * QK_SCALE).astype(BF16)
    put_kv(y[:, MIX_W:2 * MIX_W], y[:, 2 * MIX_W:w3], ksb, vsb, 0)

    y = _dot(h, w_ref[:, w3:2 * w3])
    qfx[...] = (y[:, 0:MIX_W] * QK_SCALE).astype(BF16)
    put_kv(y[:, MIX_W:2 * MIX_W], y[:, 2 * MIX_W:w3], kfx, vfx, 1)

    y = _dot(h, w_ref[:, 2 * w3:3 * w3])
    qds[...] = (_rope(y[:, 0:MIX_W], cos, sin, first_half) * QK_SCALE).astype(BF16)
    put_kv(_rope(y[:, MIX_W:2 * MIX_W], cos, sin, first_half), y[:, 2 * MIX_W:w3], kds, vds, 2)

    y = _dot(h, w_ref[:, 3 * w3:PROJ_W])
    nq = IDX_HEADS * IDX_DIM
    qidx[...] = (_rope(y[:, 0:nq], cos, sin, first_half) * IDX_SCALE).astype(BF16)
    m = y[:, nq:nq + LANES]
    m_rot = _rope(m, cos, sin, first_half)
    lf = _log_sigmoid(m + bf_ref[...])
    misc[...] = jnp.where(lane < LF_LANE, m_rot,
                          jnp.where(lane < WI_LANE, lf,
                                    jnp.where(lane < WI_LANE + IDX_HEADS, m, 0.0)))
    kidx2[...] = jnp.where(lane < IDX_DIM, m_rot, pltpu.roll(m_rot, IDX_DIM, 1)).astype(BF16)


def _proj(x, mod, g_pre, w_main, bf_row, cos_t, sin_t, *, tm, gdiv, batch_t=None):
    n, d = x.shape
    r = mod.shape[2]
    tab_tiles = cos_t.shape[0] // tm
    row = lambda w: pl.BlockSpec((tm, w), lambda i: (i, 0))
    tab = pl.BlockSpec((tm, LANES), lambda i: (i % tab_tiles, 0))
    shp = lambda w, dt: jax.ShapeDtypeStruct((n, w), dt)
    nq = IDX_HEADS * IDX_DIM
    kv_dt = F32 if batch_t is None else BF16
    out_specs = [row(MIX_W)] * 9 + [row(nq), row(LANES), row(LANES)]
    out_shape = [shp(MIX_W, BF16), shp(MIX_W, kv_dt), shp(MIX_W, kv_dt)] * 3 + [
        shp(nq, BF16), shp(LANES, BF16), shp(LANES, F32)]
    if batch_t is not None:
        t = n // batch_t
        out_specs += [pl.BlockSpec((None, MIX_W, tm), lambda i: (i // gdiv, 0, i % gdiv))] * 6
        out_shape += [jax.ShapeDtypeStruct((batch_t, MIX_W, t), F32)] * 6
    return pl.pallas_call(
        _proj_kernel,
        grid=(n // tm,),
        in_specs=[
            row(d),
            _mod_spec(r, d, gdiv, 3),
            _mod_spec(r, d, gdiv, 4),
            pl.BlockSpec((1, d), lambda i: (0, 0)),
            pl.BlockSpec((d, PROJ_W), lambda i: (0, 0)),
            pl.BlockSpec((1, LANES), lambda i: (0, 0)),
            tab, tab,
        ],
        out_specs=out_specs,
        out_shape=out_shape,
        compiler_params=_cparams(("arbitrary",)),
        name="proj",
    )(x, mod, mod, g_pre, w_main, bf_row, cos_t, sin_t)


def _merge_kernel(x_ref, shift_ref, scale_ref, gate_ref, gpre_ref, gpost_ref,
                  osb_ref, ofx_ref, ods_ref, wg_ref, wb_ref, wo_ref, o_ref):
    x = x_ref[...]
    d = x.shape[1]
    h = _norm_mod(x, gpre_ref[...], scale_ref[0, 0], shift_ref[0, 0]).astype(BF16)
    merged = None
    for nb, o_br in enumerate((osb_ref, ofx_ref, ods_ref)):
        gl = _dot(h, wg_ref[:, nb * d:(nb + 1) * d])
        y = _dot(o_br[...], wb_ref[nb])
        t = jax.nn.sigmoid(gl) * y
        merged = t if merged is None else merged + t
    out = _dot(merged.astype(BF16), wo_ref[...])
    o_ref[...] = x + gate_ref[0, 0] * _rms(out, gpost_ref[...])


def _merge(x, mod, g_pre, g_post, o_sb, o_fx, o_ds, w_gate, w_branch, w_out, *, tm, gdiv):
    n, d = x.shape
    r = mod.shape[2]
    row = lambda w: pl.BlockSpec((tm, w), lambda i: (i, 0))
    return pl.pallas_call(
        _merge_kernel,
        grid=(n // tm,),
        in_specs=[
            row(d),
            _mod_spec(r, d, gdiv, 3),
            _mod_spec(r, d, gdiv, 4),
            _mod_spec(r, d, gdiv, 5),
            pl.BlockSpec((1, d), lambda i: (0, 0)),
            pl.BlockSpec((1, d), lambda i: (0, 0)),
            row(MIX_W), row(MIX_W), row(MIX_W),
            pl.BlockSpec((d, N_BRANCH * d), lambda i: (0, 0)),
            pl.BlockSpec((N_BRANCH, MIX_W, d), lambda i: (0, 0, 0)),
            pl.BlockSpec((d, d), lambda i: (0, 0)),
        ],
        out_specs=row(d),
        out_shape=jax.ShapeDtypeStruct((n, d), F32),
        compiler_params=_cparams(("arbitrary",)),
        name="merge",
    )(x, mod, mod, mod, g_pre, g_post, o_sb, o_fx, o_ds, w_gate, w_branch, w_out)


def _aug_lanes(lane, base, ones_first, parts):
    one_lo, val_lo = (base, base + 3) if ones_first else (base + 3, base)
    out = jnp.where((lane >= one_lo) & (lane < one_lo + 3), 1.0, 0.0)
    for k, p in enumerate(parts):
        out = jnp.where(lane == val_lo + k, p, out)
    return out


def _fox_prep_kernel(q_ref, k_ref, misc_ref, qa_ref, ka_ref, carry):
    j = pl.program_id(1)

    @pl.when(j == 0)
    def _():
        carry[...] = jnp.zeros_like(carry)

    lf = misc_ref[...]
    tm = lf.shape[0]
    low = _strict_lower(tm)
    p1, p2, p3 = _split3(lf)
    g = _dot(low, p1) + _dot(low, p2) + _dot(low, p3) + carry[...]
    carry[...] += jnp.sum(lf, axis=0, keepdims=True)

    lane = lax.broadcasted_iota(jnp.int32, (1, LANES), 1)
    lo_half = lane < HEAD_DIM
    for p in range(N_PAIR):
        qp = q_ref[:, p * LANES:(p + 1) * LANES].astype(F32)
        kp = k_ref[:, p * LANES:(p + 1) * LANES]
        for a in range(2):
            hd = 2 * p + a
            gcol = g[:, LF_LANE + hd:LF_LANE + hd + 1]
            g1, g2, g3 = (t.astype(F32) for t in _split3(gcol))
            own = lo_half if a == 0 else jnp.logical_not(lo_half)
            base = HEAD_DIM if a == 0 else 0
            q_aug = jnp.where(own, qp, _aug_lanes(lane, base, True, (-g1, -g2, -g3)))
            k_aug = jnp.where(own, kp, _aug_lanes(lane, base, False, (g1, g2, g3)))
            qa_ref[:, hd * LANES:(hd + 1) * LANES] = q_aug.astype(BF16)
            ka_ref[:, hd * LANES:(hd + 1) * LANES] = k_aug.astype(BF16)


def _fox_prep(q, k, misc, *, tm):
    b, t, _ = q.shape
    nt = t // tm
    rev = lambda w: pl.BlockSpec((None, tm, w), lambda bb, j: (bb, nt - 1 - j, 0))
    return pl.pallas_call(
        _fox_prep_kernel,
        grid=(b, nt),
        in_specs=[rev(MIX_W), rev(MIX_W), rev(LANES)],
        out_specs=[rev(N_HEADS * LANES), rev(N_HEADS * LANES)],
        out_shape=[jax.ShapeDtypeStruct((b, t, N_HEADS * LANES), BF16)] * 2,
        scratch_shapes=[pltpu.VMEM((1, LANES), F32)],
        compiler_params=_cparams(("arbitrary", "arbitrary")),
        name="fox_prep",
    )(q, k, misc)


def _sb_block(qh, kblk, vblk, mask, upper, acc, c, kv_t=False):
    z = _dot(qh, kblk) if kv_t else _dot_nt(qh, kblk)
    neg_abs = lax.bitcast_convert_type(lax.bitcast_convert_type(z, jnp.int32) | INT_MIN, F32)
    ls = jnp.minimum(z, 0.0) - jnp.log(1.0 + jnp.exp(neg_abs))
    lk = ls - z
    if mask is not None:
        lk = jnp.where(mask, lk, 0.0)
    hi, lo = _split2(lk)
    sub = upper.shape[0]
    tot = []
    for j in range(z.shape[1] // sub - 1, -1, -1):
        sl = slice(j * sub, (j + 1) * sub)
        tot.append(_dot(hi[:, sl], upper) + _dot(lo[:, sl], upper) + c)
        c = c + jnp.sum(lk[:, sl], axis=-1, keepdims=True)
    tot = tot[0] if len(tot) == 1 else jnp.concatenate(tot[::-1], axis=1)
    a = jnp.exp(ls + tot)
    if mask is not None:
        a = jnp.where(mask, a, 0.0)
    a = a.astype(BF16)
    acc = acc + (_dot_nt(a, vblk) if kv_t else _dot(a, vblk))
    return acc, c


def _sb_kernel(q_ref, k_ref, v_ref, o_ref, *, tq, tk):
    i = pl.program_id(2)
    q = q_ref[...]
    lane = lax.broadcasted_iota(jnp.int32, (1, LANES), 1)
    lo_half = lane < HEAD_DIM
    zero = jnp.zeros_like(q)
    q2 = jnp.concatenate([jnp.where(lo_half, q, zero), jnp.where(lo_half, zero, q)], axis=0)
    qpos = i * tq + (lax.broadcasted_iota(jnp.int32, (2 * tq, 1), 0) & (tq - 1))
    upper = _strict_upper(LANES)
    nkb = jnp.right_shift((i + 1) * tq + (tk - 1), tk.bit_length() - 1)

    def body(it, carry):
        acc, c = carry
        start = pl.multiple_of((nkb - 1 - it) * tk, tk)
        kblk = k_ref[pl.ds(start, tk), :].astype(BF16)
        vblk = v_ref[pl.ds(start, tk), :].astype(BF16)
        kpos = start + lax.broadcasted_iota(jnp.int32, (1, tk), 1)
        return _sb_block(q2, kblk, vblk, kpos < qpos, upper, acc, c)

    acc, _ = lax.fori_loop(0, nkb, body, (jnp.zeros((2 * tq, LANES), F32), jnp.zeros((2 * tq, 1), F32)))
    o_ref[...] = jnp.where(lo_half, acc[0:tq], acc[tq:2 * tq]).astype(o_ref.dtype)


def _sb_prompt(q, k, v, *, tq, tk):
    b, t, _ = q.shape
    return pl.pallas_call(
        functools.partial(_sb_kernel, tq=tq, tk=tk),
        grid=(b, N_PAIR, t // tq),
        in_specs=[
            pl.BlockSpec((None, tq, LANES), lambda bb, p, i: (bb, i, p)),
            pl.BlockSpec((None, t, LANES), lambda bb, p, i: (bb, 0, p)),
            pl.BlockSpec((None, t, LANES), lambda bb, p, i: (bb, 0, p)),
        ],
        out_specs=pl.BlockSpec((None, tq, LANES), lambda bb, p, i: (bb, i, p)),
        out_shape=jax.ShapeDtypeStruct((b, t, MIX_W), BF16),
        compiler_params=_cparams(("arbitrary", "arbitrary", "arbitrary")),
        name="sb_prompt",
    )(q, k, v)


def _softmax_step(s, vblk, m, l, acc, v_t=False):
    m_new = jnp.maximum(m, jnp.max(s, axis=-1, keepdims=True))
    alpha = jnp.exp(m - m_new)
    p = jnp.exp(s - m_new)
    l = alpha * l + jnp.sum(p, axis=-1, keepdims=True)
    p = p.astype(BF16)
    acc = alpha * acc + (_dot_nt(p, vblk) if v_t else _dot(p, vblk))
    return m_new, l, acc


def _fox_kernel(qa_ref, ka_ref, v_ref, o_ref, *, tq, tk):
    i = pl.program_id(2)
    qa = qa_ref[...]
    qpos = i * tq + lax.broadcasted_iota(jnp.int32, (tq, 1), 0)
    nkb = jnp.right_shift((i + 1) * tq + (tk - 1), tk.bit_length() - 1)

    def body(kb, carry):
        start = pl.multiple_of(kb * tk, tk)
        kblk = ka_ref[pl.ds(start, tk), :]
        vblk = v_ref[pl.ds(start, tk), :].astype(BF16)
        kpos = start + lax.broadcasted_iota(jnp.int32, (1, tk), 1)
        mask = kpos <= qpos
        out = []
        for a in range(2):
            m, l, acc = carry[3 * a:3 * a + 3]
            s = _dot_nt(qa[:, a * LANES:(a + 1) * LANES], kblk[:, a * LANES:(a + 1) * LANES])
            s = jnp.where(mask, s, -jnp.inf)
            out.extend(_softmax_step(s, vblk, m, l, acc))
        return tuple(out)

    m0 = jnp.full((tq, 1), NEG_BIG, F32)
    zc = jnp.zeros((tq, 1), F32)
    za = jnp.zeros((tq, LANES), F32)
    _, l0, acc0, _, l1, acc1 = lax.fori_loop(0, nkb, body, (m0, zc, za, m0, zc, za))
    lane = lax.broadcasted_iota(jnp.int32, (1, LANES), 1)
    o_ref[...] = jnp.where(lane < HEAD_DIM, acc0 / l0, acc1 / l1).astype(o_ref.dtype)


def _fox_prompt(qa, ka, v, *, tq, tk):
    b, t, _ = v.shape
    return pl.pallas_call(
        functools.partial(_fox_kernel, tq=tq, tk=tk),
        grid=(b, N_PAIR, t // tq),
        in_specs=[
            pl.BlockSpec((None, tq, 2 * LANES), lambda bb, p, i: (bb, i, p)),
            pl.BlockSpec((None, t, 2 * LANES), lambda bb, p, i: (bb, 0, p)),
            pl.BlockSpec((None, t, LANES), lambda bb, p, i: (bb, 0, p)),
        ],
        out_specs=pl.BlockSpec((None, tq, LANES), lambda bb, p, i: (bb, i, p)),
        out_shape=jax.ShapeDtypeStruct((b, t, MIX_W), BF16),
        compiler_params=_cparams(("arbitrary", "arbitrary", "arbitrary")),
        name="fox_prompt",
    )(qa, ka, v)


def _sort_key(x):
    bits = lax.bitcast_convert_type(x, jnp.int32)
    key = jnp.where(bits < 0, bits ^ jnp.int32(0x7FFFFFFF), bits)
    return jnp.where(x == 0.0, 0, key)


def _count(pred_parts):
    tot = None
    for preds in pred_parts:
        ind = jnp.where(preds[-1], 1.0, 0.0)
        for p in preds[-2::-1]:
            ind = jnp.where(p, ind, 0.0)
        c = jnp.sum(ind, axis=-1, keepdims=True)
        tot = c if tot is None else tot + c
    return tot


def _kth_threshold(key_gets, rows, kf):
    def step(it, thr):
        cand = thr + jnp.left_shift(jnp.int32(1), 31 - it)
        cnt = _count([(kg() >= cand,) for kg in key_gets])
        return jnp.where(cnt >= kf, cand, thr)

    return lax.fori_loop(0, 32, step, jnp.full((rows, 1), INT_MIN, jnp.int32))


def _topk_select(key_gets, pos_list, rows, topk, idx_bits):
    kf = float(topk)
    thr = _kth_threshold(key_gets, rows, kf)
    need = kf - _count([(kg() > thr,) for kg in key_gets])
    n_eq = _count([(kg() == thr,) for kg in key_gets])

    def ibody(it, bound):
        cand = bound + jnp.left_shift(jnp.int32(1), idx_bits - 1 - it)
        cnt = _count([(kg() == thr, pos < cand) for kg, pos in zip(key_gets, pos_list)])
        return jnp.where(cnt <= need, cand, bound)

    some_partial = jnp.max(jnp.where(n_eq > need, 1.0, 0.0)) > 0.0
    bound = lax.cond(
        some_partial,
        lambda: lax.fori_loop(0, idx_bits, ibody, jnp.zeros((rows, 1), jnp.int32)),
        lambda: jnp.full((rows, 1), 1 << idx_bits, jnp.int32))
    tie = jnp.where(thr > KEY_NEG_INF, 0.0, -jnp.inf)
    out = []
    for kg, pos in zip(key_gets, pos_list):
        key = kg()
        at_thr = jnp.where(pos < bound, tie, -jnp.inf)
        out.append(jnp.where(key > thr, 0.0, jnp.where(key == thr, at_thr, -jnp.inf)))
    return out


def _dsa_body(i, n_keys, qidx_ref, kidx2_ref, miscq_ref, q_ref, k_ref, v_ref, o_ref, key_scr, bias_scr, *, tq, topk):
    lane = lax.broadcasted_iota(jnp.int32, (1, LANES), 1)
    lo_half = lane < HEAD_DIM
    kidx2 = kidx2_ref[0:n_keys, :]
    wq = miscq_ref[...]
    isc = jnp.zeros((tq, n_keys), F32)
    for h in range(IDX_HEADS):
        qp = qidx_ref[:, (h // 2) * LANES:(h // 2 + 1) * LANES]
        own = lo_half if h % 2 == 0 else jnp.logical_not(lo_half)
        qh = jnp.where(own, qp, jnp.zeros_like(qp))
        s = _dot_nt(qh, kidx2)
        isc = isc + jnp.maximum(s, 0.0) * wq[:, WI_LANE + h:WI_LANE + h + 1]
    isc = isc * IDX_HEAD_SCALE
    qpos = i * tq + lax.broadcasted_iota(jnp.int32, (tq, 1), 0)
    kpos = lax.broadcasted_iota(jnp.int32, (1, n_keys), 1)
    allowed = jnp.right_shift(kpos, CHUNK_SHIFT) <= jnp.right_shift(qpos, CHUNK_SHIFT)
    isc = jnp.where(allowed, isc, -jnp.inf)
    key_scr[:, 0:n_keys] = _sort_key(isc)
    (bias,) = _topk_select([lambda: key_scr[:, 0:n_keys]], [kpos], tq, topk, int(n_keys).bit_length())
    bias_scr[:, 0:n_keys] = bias

    for p in range(N_PAIR):
        qp = q_ref[:, p * LANES:(p + 1) * LANES]
        kp = k_ref[0:n_keys, p * LANES:(p + 1) * LANES].astype(BF16)
        vp = v_ref[0:n_keys, p * LANES:(p + 1) * LANES].astype(BF16)
        outs = []
        for a in range(2):
            own = lo_half if a == 0 else jnp.logical_not(lo_half)
            qh = jnp.where(own, qp, jnp.zeros_like(qp))
            s = _dot_nt(qh, kp) + bias_scr[:, 0:n_keys]
            m = jnp.max(s, axis=-1, keepdims=True)
            e = jnp.exp(s - m)
            l = jnp.sum(e, axis=-1, keepdims=True)
            outs.append(_dot(e.astype(BF16), vp) / l)
        o_ref[:, p * LANES:(p + 1) * LANES] = jnp.where(lo_half, outs[0], outs[1]).astype(o_ref.dtype)


def _dsa_kernel(*refs, tq, topk, n_groups):
    i = pl.program_id(1)
    t = refs[4].shape[0]
    glen = t // n_groups
    group = jnp.right_shift(i * tq, glen.bit_length() - 1)
    for g in range(n_groups):
        @pl.when(group == g)
        def _():
            _dsa_body(i, (g + 1) * glen, *refs, tq=tq, topk=topk)


def _dsa_prompt(qidx, kidx2, misc, q, k, v, *, tq):
    b, t, _ = q.shape
    topk = min(TOPK_MAX, t // 4)
    glen = t // 4
    n_groups = 4 if (glen % tq == 0 and glen >= topk) else 1
    qrow = lambda w: pl.BlockSpec((None, tq, w), lambda bb, i: (bb, i, 0))
    full = lambda w: pl.BlockSpec((None, t, w), lambda bb, i: (bb, 0, 0))
    return pl.pallas_call(
        functools.partial(_dsa_kernel, tq=tq, topk=topk, n_groups=n_groups),
        grid=(b, t // tq),
        in_specs=[qrow(IDX_HEADS * IDX_DIM), full(LANES), qrow(LANES), qrow(MIX_W), full(MIX_W), full(MIX_W)],
        out_specs=qrow(MIX_W),
        out_shape=jax.ShapeDtypeStruct((b, t, MIX_W), BF16),
        scratch_shapes=[pltpu.VMEM((tq, t), jnp.int32), pltpu.VMEM((tq, t), F32)],
        compiler_params=_cparams(("arbitrary", "arbitrary")),
        name="dsa_prompt",
    )(qidx, kidx2, misc, q, k, v)


def _head_rows(x, n_rep):
    return jnp.concatenate([x] * n_rep, axis=0)


def _block_diag_q(q, tq):
    rows = N_HEADS * tq
    rhead = jnp.right_shift(lax.broadcasted_iota(jnp.int32, (rows, 1), 0), tq.bit_length() - 1)
    lhead = jnp.right_shift(lax.broadcasted_iota(jnp.int32, (1, MIX_W), 1), HEAD_DIM.bit_length() - 1)
    qq = _head_rows(q, N_HEADS)
    return jnp.where(rhead == lhead, qq, jnp.zeros_like(qq)), rhead == lhead


def _diag_out(o_bd, diag, tq):
    o = jnp.where(diag, o_bd, 0.0)
    out = o[0:tq]
    for h in range(1, N_HEADS):
        out = out + o[h * tq:(h + 1) * tq]
    return out


def _pad_rows(x, rows):
    return jnp.concatenate([x, jnp.zeros((rows - x.shape[0], x.shape[1]), x.dtype)], axis=0)


def _sb_sample_kernel(q_ref, kn_ref, vn_ref, kp_ref, vp_ref, o_ref, *, tk):
    tq = q_ref.shape[0]
    past = kp_ref.shape[1]
    rows = N_HEADS * tq
    qbd, diag = _block_diag_q(q_ref[...], tq)
    trow = lax.broadcasted_iota(jnp.int32, (rows, 1), 0) & (tq - 1)
    kn = _pad_rows(kn_ref[...], LANES).astype(BF16)
    vn = _pad_rows(vn_ref[...], LANES).astype(BF16)
    mask_n = lax.broadcasted_iota(jnp.int32, (1, LANES), 1) < trow
    acc = jnp.zeros((rows, MIX_W), F32)
    c = jnp.zeros((rows, 1), F32)
    upper = _strict_upper(LANES)
    acc, c = _sb_block(qbd, kn, vn, mask_n, upper, acc, c)
    for kb in range(past // tk - 1, -1, -1):
        kblk = kp_ref[:, kb * tk:(kb + 1) * tk].astype(BF16)
        vblk = vp_ref[:, kb * tk:(kb + 1) * tk].astype(BF16)
        acc, c = _sb_block(qbd, kblk, vblk, None, upper, acc, c, kv_t=True)
    o_ref[...] = _diag_out(acc, diag, tq).astype(o_ref.dtype)


def _fox_sample_kernel(q_ref, kn_ref, vn_ref, miscn_ref, kp_ref, vp_ref, lfp_ref, o_ref, *, tk):
    tq = q_ref.shape[0]
    past = kp_ref.shape[1]
    rows = N_HEADS * tq
    hp = lfp_ref.shape[0]
    qbd, diag = _block_diag_q(q_ref[...], tq)
    trow = lax.broadcasted_iota(jnp.int32, (rows, 1), 0) & (tq - 1)
    lane = lax.broadcasted_iota(jnp.int32, (1, LANES), 1)

    misc_n = _pad_rows(miscn_ref[...], LANES)
    pick = jnp.where(lax.broadcasted_iota(jnp.int32, (hp, LANES), 1)
                     == LF_LANE + lax.broadcasted_iota(jnp.int32, (hp, LANES), 0), 1.0, 0.0).astype(BF16)
    lf_new = sum(_dot_nt(pick, part) for part in _split3(misc_n))
    up_n = _strict_upper(LANES)
    g_new = sum(_dot(part, up_n) for part in _split3(lf_new))
    tot_new = jnp.sum(lf_new, axis=-1, keepdims=True)

    def expand(g):
        return jnp.concatenate([jnp.broadcast_to(g[h:h + 1], (tq, g.shape[1])) for h in range(N_HEADS)], axis=0)

    g_new_x = expand(g_new)
    g_q = jnp.sum(jnp.where(lane == trow, g_new_x, 0.0), axis=-1, keepdims=True)

    kn = _pad_rows(kn_ref[...], LANES).astype(BF16)
    vn = _pad_rows(vn_ref[...], LANES).astype(BF16)
    s_n = _dot_nt(qbd, kn) + g_new_x - g_q
    s_n = jnp.where(lane <= trow, s_n, -jnp.inf)
    m, l, acc = _softmax_step(s_n, vn, jnp.full((rows, 1), NEG_BIG, F32), jnp.zeros((rows, 1), F32),
                              jnp.zeros((rows, MIX_W), F32))
    sub = min(2 * LANES, tk)
    upper = _strict_upper(sub)
    carry = tot_new
    for kb in range(past // tk - 1, -1, -1):
        g_parts = []
        for j in range(tk // sub - 1, -1, -1):
            lf_sub = lfp_ref[:, kb * tk + j * sub:kb * tk + (j + 1) * sub]
            g_parts.append(sum(_dot(part, upper) for part in _split3(lf_sub)) + carry)
            carry = carry + jnp.sum(lf_sub, axis=-1, keepdims=True)
        g_blk = g_parts[0] if len(g_parts) == 1 else jnp.concatenate(g_parts[::-1], axis=1)
        kblk = kp_ref[:, kb * tk:(kb + 1) * tk].astype(BF16)
        vblk = vp_ref[:, kb * tk:(kb + 1) * tk].astype(BF16)
        s = _dot(qbd, kblk) + expand(g_blk) - g_q
        m, l, acc = _softmax_step(s, vblk, m, l, acc, v_t=True)
    o_ref[...] = _diag_out(acc / l, diag, tq).astype(o_ref.dtype)


def _dsa_sample_kernel(qidx_ref, miscn_ref, kin_ref, q_ref, kn_ref, vn_ref, kip_ref, kp_ref, vp_ref, o_ref,
                       keyp_scr, keyn_scr, *, topk, pos0):
    tq = q_ref.shape[0]
    past = kp_ref.shape[1]
    lane = lax.broadcasted_iota(jnp.int32, (1, LANES), 1)
    qi = qidx_ref[...].astype(F32)
    qs = jnp.concatenate([qi[:, h * IDX_DIM:(h + 1) * IDX_DIM] for h in range(IDX_HEADS)], axis=0).astype(BF16)
    wq = miscn_ref[...]
    ws = jnp.concatenate([wq[:, WI_LANE + h:WI_LANE + h + 1] for h in range(IDX_HEADS)], axis=0)
    kip = kip_ref[...].astype(BF16)
    kin = _pad_rows(kin_ref[...][:, 0:IDX_DIM], LANES).astype(BF16)

    def head_sum(s):
        s = jnp.maximum(s, 0.0) * ws
        out = s[0:tq]
        for h in range(1, IDX_HEADS):
            out = out + s[h * tq:(h + 1) * tq]
        return out * IDX_HEAD_SCALE

    isc_p = head_sum(_dot(qs, kip))
    isc_n = head_sum(_dot_nt(qs, kin))
    qchunk = jnp.right_shift(pos0 + lax.broadcasted_iota(jnp.int32, (tq, 1), 0), CHUNK_SHIFT)
    ppos = lax.broadcasted_iota(jnp.int32, (1, past), 1)
    npos = pos0 + lane
    ok_p = jnp.right_shift(ppos, CHUNK_SHIFT) <= qchunk
    ok_n = (jnp.right_shift(npos, CHUNK_SHIFT) <= qchunk) & (lane < tq)
    isc_p = jnp.where(ok_p, isc_p, -jnp.inf)
    isc_n = jnp.where(ok_n, isc_n, -jnp.inf)
    keyp_scr[...] = _sort_key(isc_p)
    keyn_scr[...] = jnp.where(lane < tq, _sort_key(isc_n), INT_MIN)
    bias_p, bias_n = _topk_select([lambda: keyp_scr[...], lambda: keyn_scr[...]], [ppos, npos], tq, topk,
                                  int(past + LANES).bit_length())
    bias_p = _head_rows(bias_p, N_HEADS)
    bias_n = _head_rows(bias_n, N_HEADS)

    qbd, diag = _block_diag_q(q_ref[...], tq)
    kn = _pad_rows(kn_ref[...], LANES).astype(BF16)
    vn = _pad_rows(vn_ref[...], LANES).astype(BF16)
    s_p = _dot(qbd, kp_ref[...].astype(BF16)) + bias_p
    s_n = _dot_nt(qbd, kn) + bias_n
    m = jnp.maximum(jnp.max(s_p, axis=-1, keepdims=True), jnp.max(s_n, axis=-1, keepdims=True))
    e_p = jnp.exp(s_p - m)
    e_n = jnp.exp(s_n - m)
    l = jnp.sum(e_p, axis=-1, keepdims=True) + jnp.sum(e_n, axis=-1, keepdims=True)
    o = (_dot_nt(e_p.astype(BF16), vp_ref[...].astype(BF16)) + _dot(e_n.astype(BF16), vn)) / l
    o_ref[...] = _diag_out(o, diag, tq).astype(o_ref.dtype)


def _sample_attention(layer, q_sb, k_sb, v_sb, q_fx, k_fx, v_fx, q_ds, k_ds, v_ds, qidx, misc,
                      c_sb_k, c_sb_v, c_fx_k, c_fx_v, lf_t, c_ds_k, c_ds_v, c_kidx):
    b, tq, _ = q_sb.shape
    past = c_sb_k.shape[3]
    tk = min(SAMPLE_KEYS, past)
    new = lambda w: pl.BlockSpec((None, tq, w), lambda bb: (bb, 0, 0))
    cache = lambda w: pl.BlockSpec((None, None, w, past), lambda bb: (layer, bb, 0, 0))
    out = jax.ShapeDtypeStruct((b, tq, MIX_W), BF16)
    cp = _cparams(("arbitrary",))
    o_sb = pl.pallas_call(
        functools.partial(_sb_sample_kernel, tk=tk), grid=(b,),
        in_specs=[new(MIX_W), new(MIX_W), new(MIX_W), cache(MIX_W), cache(MIX_W)],
        out_specs=new(MIX_W), out_shape=out, compiler_params=cp, name="sb_sample",
    )(q_sb, k_sb, v_sb, c_sb_k, c_sb_v)
    hp = lf_t.shape[2]
    o_fx = pl.pallas_call(
        functools.partial(_fox_sample_kernel, tk=tk), grid=(b,),
        in_specs=[new(MIX_W), new(MIX_W), new(MIX_W), new(LANES), cache(MIX_W), cache(MIX_W),
                  cache(hp)],
        out_specs=new(MIX_W), out_shape=out, compiler_params=cp, name="fox_sample",
    )(q_fx, k_fx, v_fx, misc, c_fx_k, c_fx_v, lf_t)
    topk = min(TOPK_MAX, (past + tq) // 4)
    o_ds = pl.pallas_call(
        functools.partial(_dsa_sample_kernel, topk=topk, pos0=past), grid=(b,),
        in_specs=[new(IDX_HEADS * IDX_DIM), new(LANES), new(LANES), new(MIX_W), new(MIX_W), new(MIX_W),
                  cache(IDX_DIM), cache(MIX_W), cache(MIX_W)],
        out_specs=new(MIX_W), out_shape=out,
        scratch_shapes=[pltpu.VMEM((tq, past), jnp.int32), pltpu.VMEM((tq, LANES), jnp.int32)],
        compiler_params=cp, name="dsa_sample",
    )(qidx, misc, misc, q_ds, k_ds, v_ds, c_kidx, c_ds_k, c_ds_v)
    return o_sb, o_fx, o_ds


def _rope_tables(pos):
    half = HEAD_DIM // 2
    inv_freq = ROPE_THETA ** (-jnp.arange(half, dtype=F32) / half)
    ang = pos.astype(F32)[:, None] * inv_freq[None, :]
    cos = jnp.cos(ang)
    sin = jnp.sin(ang)
    cos_t = jnp.concatenate([cos, cos, cos, cos], axis=1)
    sin_t = jnp.concatenate([-sin, sin, -sin, sin], axis=1)
    return cos_t, sin_t


def _prep_weights(w_in, b_forget, ffn_w_in, ffn_w_out, w_branch, w_out):
    w3 = 3 * MIX_W
    o = np.cumsum([0, w3, w3, N_HEADS, w3, IDX_HEADS * IDX_DIM, IDX_DIM, IDX_HEADS]).tolist()
    sb, fx, fl, ds, qi, ki, wi = (w_in[:, :, o[k]:o[k + 1]] for k in range(7))
    pad = jnp.zeros(w_in.shape[:2] + (LANES - IDX_DIM - N_HEADS - IDX_HEADS,), w_in.dtype)
    w_main = jnp.concatenate([sb, fx, ds, qi, ki, fl, wi, pad], axis=2).astype(BF16)
    w_gate = w_in[:, :, o[7]:].astype(BF16)
    depth = w_in.shape[0]
    bf_row = jnp.zeros((depth, 1, LANES), F32).at[:, 0, LF_LANE:LF_LANE + N_HEADS].set(b_forget)
    return (w_main, w_gate, bf_row, ffn_w_in.astype(BF16), ffn_w_out.astype(BF16),
            w_branch.astype(BF16), w_out.astype(BF16))


def _trunk(x3, mod_l, gdiv, tm, pos, caches, norm_g, weights, *, tq_attn):
    w_main, w_gate, bf_row, w_up, w_dn, w_branch, w_out = weights
    b, t, d = x3.shape
    n = b * t
    depth = w_main.shape[0]
    x = x3.reshape(n, d)
    cos_t, sin_t = _rope_tables(pos)
    if caches is not None:
        cos_t = jnp.tile(cos_t, (tm // t, 1))
        sin_t = jnp.tile(sin_t, (tm // t, 1))
        lf_t = jnp.swapaxes(caches[4], 2, 3)
        lf_t = jnp.pad(lf_t, ((0, 0), (0, 0), (0, 16 - lf_t.shape[2]), (0, 0)))
    states = []
    for l in range(depth):
        g = lambda k: norm_g[l, k][None, :]
        mod = mod_l[l]
        x = _ffn(x, mod, 0, g(0), g(1), w_up[l, 0], w_dn[l, 0], tm=tm, gdiv=gdiv)
        (q_sb, k_sb, v_sb, q_fx, k_fx, v_fx, q_ds, k_ds, v_ds, qidx, kidx2, misc, *state_t) = _proj(
            x, mod, g(2), w_main[l], bf_row[l], cos_t, sin_t, tm=tm, gdiv=gdiv,
            batch_t=b if caches is None else None)
        r3 = lambda a: a.reshape(b, t, a.shape[-1])
        if caches is None:
            qa, ka = _fox_prep(r3(q_fx), r3(k_fx), r3(misc), tm=min(256, t))
            o_sb = _sb_prompt(r3(q_sb), r3(k_sb), r3(v_sb), tq=tq_attn, tk=min(512, t))
            o_fx = _fox_prompt(qa, ka, r3(v_fx), tq=tq_attn, tk=min(512, t))
            o_ds = _dsa_prompt(r3(qidx), r3(kidx2), r3(misc), r3(q_ds), r3(k_ds), r3(v_ds), tq=tq_attn)
        else:
            o_sb, o_fx, o_ds = _sample_attention(
                l, r3(q_sb), r3(k_sb), r3(v_sb), r3(q_fx), r3(k_fx), r3(v_fx), r3(q_ds), r3(k_ds), r3(v_ds),
                r3(qidx), r3(misc), caches[0], caches[1], caches[2], caches[3], lf_t,
                caches[5], caches[6], caches[7])
        o2 = lambda a: a.reshape(n, MIX_W)
        x = _merge(x, mod, g(2), g(3), o2(o_sb), o2(o_fx), o2(o_ds), w_gate[l], w_branch[l], w_out[l],
                   tm=tm, gdiv=gdiv)
        x = _ffn(x, mod, 6, g(4), g(5), w_up[l, 1], w_dn[l, 1], tm=tm, gdiv=gdiv)
        if state_t:
            hd = lambda a: jnp.transpose(a.reshape(b, N_HEADS, HEAD_DIM, t), (0, 3, 1, 2))
            kv = [hd(a) for a in state_t]
        else:
            hd = lambda a: a.reshape(b, t, N_HEADS, HEAD_DIM)
            kv = [hd(a) for a in (k_sb, v_sb, k_fx, v_fx, k_ds, v_ds)]
        states.append((kv[0], kv[1], kv[2], kv[3],
                       misc[:, LF_LANE:LF_LANE + N_HEADS].reshape(b, t, N_HEADS),
                       kv[4], kv[5], misc[:, 0:IDX_DIM].reshape(b, t, IDX_DIM)))
    return x.reshape(b, t, d), tuple(jnp.stack(s, axis=0) for s in zip(*states))


def kernel(x_prompt, x_sample, cache_sb_k, cache_sb_v, cache_fox_k, cache_fox_v, cache_fox_logf, cache_dsa_k, cache_dsa_v, cache_dsa_kidx, c_prompt, c_sample, norm_g, w_ada, b_ada, ffn_w_in, ffn_w_out, w_in, b_forget, w_branch, w_out):
    bp, tp, d = x_prompt.shape
    bs, ts, _ = x_sample.shape
    depth = w_in.shape[0]
    past = cache_sb_k.shape[2]
    mix = lambda a: jnp.transpose(a, (0, 1, 3, 4, 2)).reshape(a.shape[:2] + (MIX_W, a.shape[2]))
    caches = (mix(cache_sb_k), mix(cache_sb_v), mix(cache_fox_k), mix(cache_fox_v), cache_fox_logf,
              mix(cache_dsa_k), mix(cache_dsa_v), jnp.swapaxes(cache_dsa_kidx, 2, 3))
    weights = _prep_weights(w_in, b_forget, ffn_w_in, ffn_w_out, w_branch, w_out)

    rows = bp + bs
    rows_pad = -(-rows // 8) * 8
    c_all = jnp.concatenate([c_prompt, c_sample, jnp.zeros((rows_pad - rows, d), F32)], axis=0)
    mod = _ada(c_all, w_ada, b_ada)
    mod_p = [mod[l, :bp].reshape(bp, N_MOD, 1, d) for l in range(depth)]
    ns = bs * ts
    mod_s = [jnp.repeat(mod[l, bp:rows].reshape(bs, N_MOD, d), ts, axis=0)
             .reshape(1, ns, N_MOD, d).transpose(0, 2, 1, 3) for l in range(depth)]

    tm_p = min(512, tp)
    pos_p = jnp.arange(tp, dtype=jnp.int32)
    pos_s = past + jnp.arange(ts, dtype=jnp.int32)
    y_p, st_p = _trunk(x_prompt, mod_p, tp // tm_p, tm_p, pos_p, None, norm_g, weights, tq_attn=min(256, tp))
    y_s, st_s = _trunk(x_sample, mod_s, 1, ns, pos_s, caches, norm_g, weights, tq_attn=ts)
    return (y_p, y_s) + st_p + st_s
```

```python
import functools

import numpy as np
import jax
import jax.numpy as jnp
from jax import lax
from jax.experimental import pallas as pl
from jax.experimental.pallas import tpu as pltpu

F32 = jnp.float32
BF16 = jnp.bfloat16

HEAD_DIM = 64
N_HEADS = 6
MIX_W = N_HEADS * HEAD_DIM
N_PAIR = MIX_W // 128
IDX_HEADS = 4
IDX_DIM = 64
CHUNK = 64
CHUNK_SHIFT = 6
TOPK_MAX = 256
N_BRANCH = 3
N_MOD = 9
ROPE_THETA = 10000.0
EPS = 1e-6
FFN_RES = 0.5
QK_SCALE = HEAD_DIM ** -0.5
IDX_SCALE = IDX_DIM ** -0.5
IDX_HEAD_SCALE = IDX_HEADS ** -0.5
LANES = 128
LF_LANE = IDX_DIM
WI_LANE = IDX_DIM + N_HEADS
PROJ_W = 3 * 3 * MIX_W + IDX_HEADS * IDX_DIM + LANES
VMEM_LIMIT = 56 * 1024 * 1024
SEARCH_UNROLL = 4
SAMPLE_KEYS = 2048
INT_MIN = np.int32(-2 ** 31)
KEY_NEG_INF = np.int32(-2139095041)
NEG_BIG = -1e30


def _cparams(sem):
    return pltpu.CompilerParams(dimension_semantics=sem, vmem_limit_bytes=VMEM_LIMIT)


def _dot(a, b):
    return jnp.dot(a, b, preferred_element_type=F32)


def _dot_nt(a, b):
    return lax.dot_general(a, b, (((1,), (1,)), ((), ())), preferred_element_type=F32)


def _split2(x):
    hi = x.astype(BF16)
    lo = (x - hi.astype(F32)).astype(BF16)
    return hi, lo


def _split3(x):
    p1 = x.astype(BF16)
    r1 = x - p1.astype(F32)
    p2 = r1.astype(BF16)
    p3 = (r1 - p2.astype(F32)).astype(BF16)
    return p1, p2, p3


def _log_sigmoid(x):
    return jnp.minimum(x, 0.0) - jnp.log1p(jnp.exp(-jnp.abs(x)))


def _rms(x, g):
    return x * lax.rsqrt(jnp.mean(x * x, axis=-1, keepdims=True) + EPS) * g


def _norm_mod(x, g, scale, shift):
    return _rms(x, g) * (1.0 + scale) + shift


def _strict_upper(n):
    r = lax.broadcasted_iota(jnp.int32, (n, n), 0)
    c = lax.broadcasted_iota(jnp.int32, (n, n), 1)
    return jnp.where(r > c, 1.0, 0.0).astype(BF16)


def _strict_lower(n):
    r = lax.broadcasted_iota(jnp.int32, (n, n), 0)
    c = lax.broadcasted_iota(jnp.int32, (n, n), 1)
    return jnp.where(c > r, 1.0, 0.0).astype(BF16)


def _ada_kernel(c_ref, w_ref, b_ref, o_ref):
    c = c_ref[...]
    s = (c * jax.nn.sigmoid(c)).astype(BF16)
    o_ref[0] = _dot(s, w_ref[0].astype(BF16)) + b_ref[0]


def _ada(c_all, w_ada, b_ada):
    depth, d, n = w_ada.shape
    rows = c_all.shape[0]
    tn = n // 8
    return pl.pallas_call(
        _ada_kernel,
        grid=(depth, n // tn),
        in_specs=[
            pl.BlockSpec((rows, d), lambda l, j: (0, 0)),
            pl.BlockSpec((1, d, tn), lambda l, j: (l, 0, j)),
            pl.BlockSpec((1, 1, tn), lambda l, j: (l, 0, j)),
        ],
        out_specs=pl.BlockSpec((1, rows, tn), lambda l, j: (l, 0, j)),
        out_shape=jax.ShapeDtypeStruct((depth, rows, n), F32),
        compiler_params=_cparams(("arbitrary", "arbitrary")),
        name="ada",
    )(c_all, w_ada, b_ada.reshape(depth, 1, n))


def _ffn_kernel(x_ref, shift_ref, scale_ref, gate_ref, gpre_ref, gpost_ref, wg_ref, wu_ref, wd_ref,
                o_ref, acc_scr, *, nj):
    j = pl.program_id(1)
    for jj in range(nj):
        @pl.when(j == jj)
        def _():
            x = x_ref[...]
            h = _norm_mod(x, gpre_ref[...], scale_ref[0, 0], shift_ref[0, 0]).astype(BF16)
            g = _dot(h, wg_ref[...])
            u = _dot(h, wu_ref[...])
            a = (g * jax.nn.sigmoid(g) * u).astype(BF16)
            d = _dot(a, wd_ref[...])
            acc = d if jj == 0 else acc_scr[...] + d
            if jj < nj - 1:
                acc_scr[...] = acc
            else:
                o_ref[...] = x + FFN_RES * gate_ref[0, 0] * _rms(acc, gpost_ref[...])


def _mod_spec(r, d, gdiv, comp):
    return pl.BlockSpec((1, 1, r, d), lambda i, *_: (i // gdiv, comp, 0, 0))


def _ffn(x, mod, comp0, g_pre, g_post, w_up, w_dn, *, tm, gdiv):
    n, d = x.shape
    ff = w_dn.shape[0]
    nj = 2 if (ff // 2) % LANES == 0 else 1
    tf = ff // nj
    r = mod.shape[2]
    return pl.pallas_call(
        functools.partial(_ffn_kernel, nj=nj),
        grid=(n // tm, nj),
        in_specs=[
            pl.BlockSpec((tm, d), lambda i, j: (i, 0)),
            _mod_spec(r, d, gdiv, comp0),
            _mod_spec(r, d, gdiv, comp0 + 1),
            _mod_spec(r, d, gdiv, comp0 + 2),
            pl.BlockSpec((1, d), lambda i, j: (0, 0)),
            pl.BlockSpec((1, d), lambda i, j: (0, 0)),
            pl.BlockSpec((d, tf), lambda i, j: (0, j)),
            pl.BlockSpec((d, tf), lambda i, j: (0, nj + j)),
            pl.BlockSpec((tf, d), lambda i, j: (j, 0)),
        ],
        out_specs=pl.BlockSpec((tm, d), lambda i, j: (i, 0)),
        out_shape=jax.ShapeDtypeStruct((n, d), F32),
        scratch_shapes=[pltpu.VMEM((tm, d), F32)],
        compiler_params=_cparams(("arbitrary", "arbitrary")),
        name="ffn",
    )(x, mod, mod, mod, g_pre, g_post, w_up, w_up, w_dn)


def _rope(x, cos, sin, first_half):
    outs = []
    for c in range(x.shape[1] // LANES):
        xs = x[:, c * LANES:(c + 1) * LANES]
        below = pltpu.roll(xs, 32, 1)
        above = pltpu.roll(xs, LANES - 32, 1)
        outs.append(xs * cos + jnp.where(first_half, above, below) * sin)
    return outs[0] if len(outs) == 1 else jnp.concatenate(outs, axis=1)


def _proj_kernel(x_ref, shift_ref, scale_ref, gpre_ref, w_ref, bf_ref, cos_ref, sin_ref,
                 qsb, ksb, vsb, qfx, kfx, vfx, qds, kds, vds, qidx, kidx2, misc, *state_t):
    h = _norm_mod(x_ref[...], gpre_ref[...], scale_ref[0, 0], shift_ref[0, 0]).astype(BF16)
    cos = cos_ref[...]
    sin = sin_ref[...]
    lane = lax.broadcasted_iota(jnp.int32, (1, LANES), 1)
    first_half = (lane & (HEAD_DIM - 1)) < (HEAD_DIM // 2)
    w3 = 3 * MIX_W

    def put_kv(k, v, k_ref, v_ref, branch):
        k_ref[...] = k.astype(k_ref.dtype)
        v_ref[...] = v.astype(v_ref.dtype)
        if state_t:
            state_t[2 * branch][...] = k.T
            state_t[2 * branch + 1][...] = v.T

    y = _dot(h, w_ref[:, 0:w3])
    qsb[...] = (y[:, 0:MIX_W] * QK_SCALE).astype(BF16)
    put_kv(y[:, MIX_W:2 * MIX_W], y[:, 2 * MIX_W:w3], ksb, vsb, 0)

    y = _dot(h, w_ref[:, w3:2 * w3])
    qfx[...] = (y[:, 0:MIX_W] * QK_SCALE).astype(BF16)
    put_kv(y[:, MIX_W:2 * MIX_W], y[:, 2 * MIX_W:w3], kfx, vfx, 1)

    y = _dot(h, w_ref[:, 2 * w3:3 * w3])
    qds[...] = (_rope(y[:, 0:MIX_W], cos, sin, first_half) * QK_SCALE).astype(BF16)
    put_kv(_rope(y[:, MIX_W:2 * MIX_W], cos, sin, first_half), y[:, 2 * MIX_W:w3], kds, vds, 2)

    y = _dot(h, w_ref[:, 3 * w3:PROJ_W])
    nq = IDX_HEADS * IDX_DIM
    qidx[...] = (_rope(y[:, 0:nq], cos, sin, first_half) * IDX_SCALE).astype(BF16)
    m = y[:, nq:nq + LANES]
    m_rot = _rope(m, cos, sin, first_half)
    lf = _log_sigmoid(m + bf_ref[...])
    misc[...] = jnp.where(lane < LF_LANE, m_rot,
                          jnp.where(lane < WI_LANE, lf,
                                    jnp.where(lane < WI_LANE + IDX_HEADS, m, 0.0)))
    kidx2[...] = jnp.where(lane < IDX_DIM, m_rot, pltpu.roll(m_rot, IDX_DIM, 1)).astype(BF16)


def _proj(x, mod, g_pre, w_main, bf_row, cos_t, sin_t, *, tm, gdiv, batch_t=None):
    n, d = x.shape
    r = mod.shape[2]
    tab_tiles = cos_t.shape[0] // tm
    row = lambda w: pl.BlockSpec((tm, w), lambda i: (i, 0))
    tab = pl.BlockSpec((tm, LANES), lambda i: (i % tab_tiles, 0))
    shp = lambda w, dt: jax.ShapeDtypeStruct((n, w), dt)
    nq = IDX_HEADS * IDX_DIM
    kv_dt = F32 if batch_t is None else BF16
    out_specs = [row(MIX_W)] * 9 + [row(nq), row(LANES), row(LANES)]
    out_shape = [shp(MIX_W, BF16), shp(MIX_W, kv_dt), shp(MIX_W, kv_dt)] * 3 + [
        shp(nq, BF16), shp(LANES, BF16), shp(LANES, F32)]
    if batch_t is not None:
        t = n // batch_t
        out_specs += [pl.BlockSpec((None, MIX_W, tm), lambda i: (i // gdiv, 0, i % gdiv))] * 6
        out_shape += [jax.ShapeDtypeStruct((batch_t, MIX_W, t), F32)] * 6
    return pl.pallas_call(
        _proj_kernel,
        grid=(n // tm,),
        in_specs=[
            row(d),
            _mod_spec(r, d, gdiv, 3),
            _mod_spec(r, d, gdiv, 4),
            pl.BlockSpec((1, d), lambda i: (0, 0)),
            pl.BlockSpec((d, PROJ_W), lambda i: (0, 0)),
            pl.BlockSpec((1, LANES), lambda i: (0, 0)),
            tab, tab,
        ],
        out_specs=out_specs,
        out_shape=out_shape,
        compiler_params=_cparams(("arbitrary",)),
        name="proj",
    )(x, mod, mod, g_pre, w_main, bf_row, cos_t, sin_t)


def _merge_kernel(x_ref, shift_ref, scale_ref, gate_ref, gpre_ref, gpost_ref,
                  osb_ref, ofx_ref, ods_ref, wg_ref, wb_ref, wo_ref, o_ref):
    x = x_ref[...]
    d = x.shape[1]
    h = _norm_mod(x, gpre_ref[...], scale_ref[0, 0], shift_ref[0, 0]).astype(BF16)
    merged = None
    for nb, o_br in enumerate((osb_ref, ofx_ref, ods_ref)):
        gl = _dot(h, wg_ref[:, nb * d:(nb + 1) * d])
        y = _dot(o_br[...], wb_ref[nb])
        t = jax.nn.sigmoid(gl) * y
        merged = t if merged is None else merged + t
    out = _dot(merged.astype(BF16), wo_ref[...])
    o_ref[...] = x + gate_ref[0, 0] * _rms(out, gpost_ref[...])


def _merge(x, mod, g_pre, g_post, o_sb, o_fx, o_ds, w_gate, w_branch, w_out, *, tm, gdiv):
    n, d = x.shape
    r = mod.shape[2]
    row = lambda w: pl.BlockSpec((tm, w), lambda i: (i, 0))
    return pl.pallas_call(
        _merge_kernel,
        grid=(n // tm,),
        in_specs=[
            row(d),
            _mod_spec(r, d, gdiv, 3),
            _mod_spec(r, d, gdiv, 4),
            _mod_spec(r, d, gdiv, 5),
            pl.BlockSpec((1, d), lambda i: (0, 0)),
            pl.BlockSpec((1, d), lambda i: (0, 0)),
            row(MIX_W), row(MIX_W), row(MIX_W),
            pl.BlockSpec((d, N_BRANCH * d), lambda i: (0, 0)),
            pl.BlockSpec((N_BRANCH, MIX_W, d), lambda i: (0, 0, 0)),
            pl.BlockSpec((d, d), lambda i: (0, 0)),
        ],
        out_specs=row(d),
        out_shape=jax.ShapeDtypeStruct((n, d), F32),
        compiler_params=_cparams(("arbitrary",)),
        name="merge",
    )(x, mod, mod, mod, g_pre, g_post, o_sb, o_fx, o_ds, w_gate, w_branch, w_out)


def _aug_lanes(lane, base, ones_first, parts):
    one_lo, val_lo = (base, base + 3) if ones_first else (base + 3, base)
    out = jnp.where((lane >= one_lo) & (lane < one_lo + 3), 1.0, 0.0)
    for k, p in enumerate(parts):
        out = jnp.where(lane == val_lo + k, p, out)
    return out


def _fox_prep_kernel(q_ref, k_ref, misc_ref, qa_ref, ka_ref, carry):
    j = pl.program_id(1)

    @pl.when(j == 0)
    def _():
        carry[...] = jnp.zeros_like(carry)

    lf = misc_ref[...]
    tm = lf.shape[0]
    low = _strict_lower(tm)
    p1, p2, p3 = _split3(lf)
    g = _dot(low, p1) + _dot(low, p2) + _dot(low, p3) + carry[...]
    carry[...] += jnp.sum(lf, axis=0, keepdims=True)

    lane = lax.broadcasted_iota(jnp.int32, (1, LANES), 1)
    lo_half = lane < HEAD_DIM
    for p in range(N_PAIR):
        qp = q_ref[:, p * LANES:(p + 1) * LANES].astype(F32)
        kp = k_ref[:, p * LANES:(p + 1) * LANES]
        for a in range(2):
            hd = 2 * p + a
            gcol = g[:, LF_LANE + hd:LF_LANE + hd + 1]
            g1, g2, g3 = (t.astype(F32) for t in _split3(gcol))
            own = lo_half if a == 0 else jnp.logical_not(lo_half)
            base = HEAD_DIM if a == 0 else 0
            q_aug = jnp.where(own, qp, _aug_lanes(lane, base, True, (-g1, -g2, -g3)))
            k_aug = jnp.where(own, kp, _aug_lanes(lane, base, False, (g1, g2, g3)))
            qa_ref[:, hd * LANES:(hd + 1) * LANES] = q_aug.astype(BF16)
            ka_ref[:, hd * LANES:(hd + 1) * LANES] = k_aug.astype(BF16)


def _fox_prep(q, k, misc, *, tm):
    b, t, _ = q.shape
    nt = t // tm
    rev = lambda w: pl.BlockSpec((None, tm, w), lambda bb, j: (bb, nt - 1 - j, 0))
    return pl.pallas_call(
        _fox_prep_kernel,
        grid=(b, nt),
        in_specs=[rev(MIX_W), rev(MIX_W), rev(LANES)],
        out_specs=[rev(N_HEADS * LANES), rev(N_HEADS * LANES)],
        out_shape=[jax.ShapeDtypeStruct((b, t, N_HEADS * LANES), BF16)] * 2,
        scratch_shapes=[pltpu.VMEM((1, LANES), F32)],
        compiler_params=_cparams(("arbitrary", "arbitrary")),
        name="fox_prep",
    )(q, k, misc)


def _sb_block(qh, kblk, vblk, mask, upper, acc, c, kv_t=False):
    z = _dot(qh, kblk) if kv_t else _dot_nt(qh, kblk)
    neg_abs = lax.bitcast_convert_type(lax.bitcast_convert_type(z, jnp.int32) | INT_MIN, F32)
    ls = jnp.minimum(z, 0.0) - jnp.log(1.0 + jnp.exp(neg_abs))
    lk = ls - z
    if mask is not None:
        lk = jnp.where(mask, lk, 0.0)
    hi, lo = _split2(lk)
    sub = upper.shape[0]
    tot = []
    for j in range(z.shape[1] // sub - 1, -1, -1):
        sl = slice(j * sub, (j + 1) * sub)
        tot.append(_dot(hi[:, sl], upper) + _dot(lo[:, sl], upper) + c)
        c = c + jnp.sum(lk[:, sl], axis=-1, keepdims=True)
    tot = tot[0] if len(tot) == 1 else jnp.concatenate(tot[::-1], axis=1)
    a = jnp.exp(ls + tot)
    if mask is not None:
        a = jnp.where(mask, a, 0.0)
    a = a.astype(BF16)
    acc = acc + (_dot_nt(a, vblk) if kv_t else _dot(a, vblk))
    return acc, c


def _sb_kernel(q_ref, k_ref, v_ref, o_ref, *, tq, tk):
    i = pl.program_id(1)
    lane = lax.broadcasted_iota(jnp.int32, (1, LANES), 1)
    lo_half = lane < HEAD_DIM
    q2 = []
    for p in range(N_PAIR):
        q = q_ref[:, p * LANES:(p + 1) * LANES]
        zero = jnp.zeros_like(q)
        q2.append(jnp.concatenate([jnp.where(lo_half, q, zero), jnp.where(lo_half, zero, q)], axis=0))
    qpos = i * tq + (lax.broadcasted_iota(jnp.int32, (2 * tq, 1), 0) & (tq - 1))
    upper = _strict_upper(LANES)
    nkb = jnp.right_shift((i + 1) * tq + (tk - 1), tk.bit_length() - 1)

    def body(it, carry):
        start = pl.multiple_of((nkb - 1 - it) * tk, tk)
        mask = (start + lax.broadcasted_iota(jnp.int32, (1, tk), 1)) < qpos
        out = []
        for p in range(N_PAIR):
            kblk = k_ref[pl.ds(start, tk), p * LANES:(p + 1) * LANES].astype(BF16)
            vblk = v_ref[pl.ds(start, tk), p * LANES:(p + 1) * LANES].astype(BF16)
            out.extend(_sb_block(q2[p], kblk, vblk, mask, upper, carry[2 * p], carry[2 * p + 1]))
        return tuple(out)

    init = (jnp.zeros((2 * tq, LANES), F32), jnp.zeros((2 * tq, 1), F32)) * N_PAIR
    fin = lax.fori_loop(0, nkb, body, init)
    for p in range(N_PAIR):
        acc = fin[2 * p]
        o_ref[:, p * LANES:(p + 1) * LANES] = jnp.where(lo_half, acc[0:tq], acc[tq:2 * tq]).astype(o_ref.dtype)


def _sb_prompt(q, k, v, *, tq, tk):
    b, t, _ = q.shape
    full = pl.BlockSpec((None, t, MIX_W), lambda bb, i: (bb, 0, 0))
    rows = pl.BlockSpec((None, tq, MIX_W), lambda bb, i: (bb, i, 0))
    return pl.pallas_call(
        functools.partial(_sb_kernel, tq=tq, tk=tk),
        grid=(b, t // tq),
        in_specs=[rows, full, full],
        out_specs=rows,
        out_shape=jax.ShapeDtypeStruct((b, t, MIX_W), BF16),
        compiler_params=_cparams(("arbitrary", "arbitrary")),
        name="sb_prompt",
    )(q, k, v)


def _softmax_step(s, vblk, m, l, acc, v_t=False):
    m_new = jnp.maximum(m, jnp.max(s, axis=-1, keepdims=True))
    alpha = jnp.exp(m - m_new)
    p = jnp.exp(s - m_new)
    l = alpha * l + jnp.sum(p, axis=-1, keepdims=True)
    p = p.astype(BF16)
    acc = alpha * acc + (_dot_nt(p, vblk) if v_t else _dot(p, vblk))
    return m_new, l, acc


def _fox_kernel(qa_ref, ka_ref, v_ref, o_ref, *, tq, tk):
    i = pl.program_id(1)
    qpos = i * tq + lax.broadcasted_iota(jnp.int32, (tq, 1), 0)
    nkb = jnp.right_shift((i + 1) * tq + (tk - 1), tk.bit_length() - 1)

    def body(kb, carry):
        start = pl.multiple_of(kb * tk, tk)
        kpos = start + lax.broadcasted_iota(jnp.int32, (1, tk), 1)
        mask = kpos <= qpos
        out = []
        for p in range(N_PAIR):
            vblk = v_ref[pl.ds(start, tk), p * LANES:(p + 1) * LANES].astype(BF16)
            for a in range(2):
                hd = 2 * p + a
                m, l, acc = carry[3 * hd:3 * hd + 3]
                s = _dot_nt(qa_ref[:, hd * LANES:(hd + 1) * LANES],
                            ka_ref[pl.ds(start, tk), hd * LANES:(hd + 1) * LANES])
                s = jnp.where(mask, s, -jnp.inf)
                out.extend(_softmax_step(s, vblk, m, l, acc))
        return tuple(out)

    init = (jnp.full((tq, 1), NEG_BIG, F32), jnp.zeros((tq, 1), F32), jnp.zeros((tq, LANES), F32)) * N_HEADS
    fin = lax.fori_loop(0, nkb, body, init)
    lane = lax.broadcasted_iota(jnp.int32, (1, LANES), 1)
    for p in range(N_PAIR):
        o0 = fin[6 * p + 2] / fin[6 * p + 1]
        o1 = fin[6 * p + 5] / fin[6 * p + 4]
        o_ref[:, p * LANES:(p + 1) * LANES] = jnp.where(lane < HEAD_DIM, o0, o1).astype(o_ref.dtype)


def _fox_prompt(qa, ka, v, *, tq, tk):
    b, t, _ = v.shape
    return pl.pallas_call(
        functools.partial(_fox_kernel, tq=tq, tk=tk),
        grid=(b, t // tq),
        in_specs=[
            pl.BlockSpec((None, tq, N_HEADS * LANES), lambda bb, i: (bb, i, 0)),
            pl.BlockSpec((None, t, N_HEADS * LANES), lambda bb, i: (bb, 0, 0)),
            pl.BlockSpec((None, t, MIX_W), lambda bb, i: (bb, 0, 0)),
        ],
        out_specs=pl.BlockSpec((None, tq, MIX_W), lambda bb, i: (bb, i, 0)),
        out_shape=jax.ShapeDtypeStruct((b, t, MIX_W), BF16),
        compiler_params=_cparams(("arbitrary", "arbitrary")),
        name="fox_prompt",
    )(qa, ka, v)


def _sort_key(x):
    bits = lax.bitcast_convert_type(x, jnp.int32)
    key = jnp.where(bits < 0, bits ^ jnp.int32(0x7FFFFFFF), bits)
    return jnp.where(x == 0.0, 0, key)


def _count(pred_parts):
    tot = None
    for preds in pred_parts:
        ind = jnp.where(preds[-1], 1.0, 0.0)
        for p in preds[-2::-1]:
            ind = jnp.where(p, ind, 0.0)
        c = jnp.sum(ind, axis=-1, keepdims=True)
        tot = c if tot is None else tot + c
    return tot


def _kth_threshold(key_gets, rows, kf):
    def step(it, thr):
        cand = thr + jnp.left_shift(jnp.int32(1), 31 - it)
        cnt = _count([(kg() >= cand,) for kg in key_gets])
        return jnp.where(cnt >= kf, cand, thr)

    return lax.fori_loop(0, 32, step, jnp.full((rows, 1), INT_MIN, jnp.int32), unroll=SEARCH_UNROLL)


def _topk_select(key_gets, pos_list, rows, topk, idx_bits):
    kf = float(topk)
    thr = _kth_threshold(key_gets, rows, kf)
    need = kf - _count([(kg() > thr,) for kg in key_gets])
    n_eq = _count([(kg() == thr,) for kg in key_gets])

    def ibody(it, bound):
        cand = bound + jnp.left_shift(jnp.int32(1), idx_bits - 1 - it)
        cnt = _count([(kg() == thr, pos < cand) for kg, pos in zip(key_gets, pos_list)])
        return jnp.where(cnt <= need, cand, bound)

    some_partial = jnp.max(jnp.where(n_eq > need, 1.0, 0.0)) > 0.0
    bound = lax.cond(
        some_partial,
        lambda: lax.fori_loop(0, idx_bits, ibody, jnp.zeros((rows, 1), jnp.int32)),
        lambda: jnp.full((rows, 1), 1 << idx_bits, jnp.int32))
    tie = jnp.where(thr > KEY_NEG_INF, 0.0, -jnp.inf)
    out = []
    for kg, pos in zip(key_gets, pos_list):
        key = kg()
        at_thr = jnp.where(pos < bound, tie, -jnp.inf)
        out.append(jnp.where(key > thr, 0.0, jnp.where(key == thr, at_thr, -jnp.inf)))
    return out


def _dsa_body(i, n_keys, qidx_ref, kidx2_ref, miscq_ref, q_ref, k_ref, v_ref, o_ref, key_scr, bias_scr, *, tq, topk):
    lane = lax.broadcasted_iota(jnp.int32, (1, LANES), 1)
    lo_half = lane < HEAD_DIM
    kidx2 = kidx2_ref[0:n_keys, :]
    wq = miscq_ref[...]
    isc = jnp.zeros((tq, n_keys), F32)
    for h in range(IDX_HEADS):
        qp = qidx_ref[:, (h // 2) * LANES:(h // 2 + 1) * LANES]
        own = lo_half if h % 2 == 0 else jnp.logical_not(lo_half)
        qh = jnp.where(own, qp, jnp.zeros_like(qp))
        s = _dot_nt(qh, kidx2)
        isc = isc + jnp.maximum(s, 0.0) * wq[:, WI_LANE + h:WI_LANE + h + 1]
    isc = isc * IDX_HEAD_SCALE
    qpos = i * tq + lax.broadcasted_iota(jnp.int32, (tq, 1), 0)
    kpos = lax.broadcasted_iota(jnp.int32, (1, n_keys), 1)
    allowed = jnp.right_shift(kpos, CHUNK_SHIFT) <= jnp.right_shift(qpos, CHUNK_SHIFT)
    isc = jnp.where(allowed, isc, -jnp.inf)
    key_scr[:, 0:n_keys] = _sort_key(isc)
    (bias,) = _topk_select([lambda: key_scr[:, 0:n_keys]], [kpos], tq, topk, int(n_keys).bit_length())
    bias_scr[:, 0:n_keys] = bias

    for p in range(N_PAIR):
        qp = q_ref[:, p * LANES:(p + 1) * LANES]
        kp = k_ref[0:n_keys, p * LANES:(p + 1) * LANES].astype(BF16)
        vp = v_ref[0:n_keys, p * LANES:(p + 1) * LANES].astype(BF16)
        outs = []
        for a in range(2):
            own = lo_half if a == 0 else jnp.logical_not(lo_half)
            qh = jnp.where(own, qp, jnp.zeros_like(qp))
            s = _dot_nt(qh, kp) + bias_scr[:, 0:n_keys]
            m = jnp.max(s, axis=-1, keepdims=True)
            e = jnp.exp(s - m)
            l = jnp.sum(e, axis=-1, keepdims=True)
            outs.append(_dot(e.astype(BF16), vp) / l)
        o_ref[:, p * LANES:(p + 1) * LANES] = jnp.where(lo_half, outs[0], outs[1]).astype(o_ref.dtype)


def _dsa_kernel(*refs, tq, topk, n_groups):
    i = pl.program_id(1)
    t = refs[4].shape[0]
    glen = t // n_groups
    group = jnp.right_shift(i * tq, glen.bit_length() - 1)
    for g in range(n_groups):
        @pl.when(group == g)
        def _():
            _dsa_body(i, (g + 1) * glen, *refs, tq=tq, topk=topk)


def _dsa_prompt(qidx, kidx2, misc, q, k, v, *, tq):
    b, t, _ = q.shape
    topk = min(TOPK_MAX, t // 4)
    n_groups = 1
    for cand in (8, 4, 2):
        if (t // cand) % tq == 0 and t // cand >= topk:
            n_groups = cand
            break
    qrow = lambda w: pl.BlockSpec((None, tq, w), lambda bb, i: (bb, i, 0))
    full = lambda w: pl.BlockSpec((None, t, w), lambda bb, i: (bb, 0, 0))
    return pl.pallas_call(
        functools.partial(_dsa_kernel, tq=tq, topk=topk, n_groups=n_groups),
        grid=(b, t // tq),
        in_specs=[qrow(IDX_HEADS * IDX_DIM), full(LANES), qrow(LANES), qrow(MIX_W), full(MIX_W), full(MIX_W)],
        out_specs=qrow(MIX_W),
        out_shape=jax.ShapeDtypeStruct((b, t, MIX_W), BF16),
        scratch_shapes=[pltpu.VMEM((tq, t), jnp.int32), pltpu.VMEM((tq, t), F32)],
        compiler_params=_cparams(("arbitrary", "arbitrary")),
        name="dsa_prompt",
    )(qidx, kidx2, misc, q, k, v)


def _head_rows(x, n_rep):
    return jnp.concatenate([x] * n_rep, axis=0)


def _block_diag_q(q, tq):
    rows = N_HEADS * tq
    rhead = jnp.right_shift(lax.broadcasted_iota(jnp.int32, (rows, 1), 0), tq.bit_length() - 1)
    lhead = jnp.right_shift(lax.broadcasted_iota(jnp.int32, (1, MIX_W), 1), HEAD_DIM.bit_length() - 1)
    qq = _head_rows(q, N_HEADS)
    return jnp.where(rhead == lhead, qq, jnp.zeros_like(qq)), rhead == lhead


def _diag_out(o_bd, diag, tq):
    o = jnp.where(diag, o_bd, 0.0)
    out = o[0:tq]
    for h in range(1, N_HEADS):
        out = out + o[h * tq:(h + 1) * tq]
    return out


def _pad_rows(x, rows):
    return jnp.concatenate([x, jnp.zeros((rows - x.shape[0], x.shape[1]), x.dtype)], axis=0)


def _sb_sample_kernel(q_ref, kn_ref, vn_ref, kp_ref, vp_ref, o_ref, *, tk):
    tq = q_ref.shape[0]
    past = kp_ref.shape[1]
    rows = N_HEADS * tq
    qbd, diag = _block_diag_q(q_ref[...], tq)
    trow = lax.broadcasted_iota(jnp.int32, (rows, 1), 0) & (tq - 1)
    kn = _pad_rows(kn_ref[...], LANES).astype(BF16)
    vn = _pad_rows(vn_ref[...], LANES).astype(BF16)
    mask_n = lax.broadcasted_iota(jnp.int32, (1, LANES), 1) < trow
    acc = jnp.zeros((rows, MIX_W), F32)
    c = jnp.zeros((rows, 1), F32)
    upper = _strict_upper(LANES)
    acc, c = _sb_block(qbd, kn, vn, mask_n, upper, acc, c)
    for kb in range(past // tk - 1, -1, -1):
        kblk = kp_ref[:, kb * tk:(kb + 1) * tk].astype(BF16)
        vblk = vp_ref[:, kb * tk:(kb + 1) * tk].astype(BF16)
        acc, c = _sb_block(qbd, kblk, vblk, None, upper, acc, c, kv_t=True)
    o_ref[...] = _diag_out(acc, diag, tq).astype(o_ref.dtype)


def _fox_sample_kernel(q_ref, kn_ref, vn_ref, miscn_ref, kp_ref, vp_ref, lfp_ref, o_ref, *, tk):
    tq = q_ref.shape[0]
    past = kp_ref.shape[1]
    rows = N_HEADS * tq
    hp = lfp_ref.shape[0]
    qbd, diag = _block_diag_q(q_ref[...], tq)
    trow = lax.broadcasted_iota(jnp.int32, (rows, 1), 0) & (tq - 1)
    lane = lax.broadcasted_iota(jnp.int32, (1, LANES), 1)

    misc_n = _pad_rows(miscn_ref[...], LANES)
    pick = jnp.where(lax.broadcasted_iota(jnp.int32, (hp, LANES), 1)
                     == LF_LANE + lax.broadcasted_iota(jnp.int32, (hp, LANES), 0), 1.0, 0.0).astype(BF16)
    lf_new = sum(_dot_nt(pick, part) for part in _split3(misc_n))
    up_n = _strict_upper(LANES)
    g_new = sum(_dot(part, up_n) for part in _split3(lf_new))
    tot_new = jnp.sum(lf_new, axis=-1, keepdims=True)

    def expand(g):
        return jnp.concatenate([jnp.broadcast_to(g[h:h + 1], (tq, g.shape[1])) for h in range(N_HEADS)], axis=0)

    g_new_x = expand(g_new)
    g_q = jnp.sum(jnp.where(lane == trow, g_new_x, 0.0), axis=-1, keepdims=True)

    kn = _pad_rows(kn_ref[...], LANES).astype(BF16)
    vn = _pad_rows(vn_ref[...], LANES).astype(BF16)
    s_n = _dot_nt(qbd, kn) + g_new_x - g_q
    s_n = jnp.where(lane <= trow, s_n, -jnp.inf)
    m, l, acc = _softmax_step(s_n, vn, jnp.full((rows, 1), NEG_BIG, F32), jnp.zeros((rows, 1), F32),
                              jnp.zeros((rows, MIX_W), F32))
    sub = min(2 * LANES, tk)
    upper = _strict_upper(sub)
    carry = tot_new
    for kb in range(past // tk - 1, -1, -1):
        g_parts = []
        for j in range(tk // sub - 1, -1, -1):
            lf_sub = lfp_ref[:, kb * tk + j * sub:kb * tk + (j + 1) * sub]
            g_parts.append(sum(_dot(part, upper) for part in _split3(lf_sub)) + carry)
            carry = carry + jnp.sum(lf_sub, axis=-1, keepdims=True)
        g_blk = g_parts[0] if len(g_parts) == 1 else jnp.concatenate(g_parts[::-1], axis=1)
        kblk = kp_ref[:, kb * tk:(kb + 1) * tk].astype(BF16)
        vblk = vp_ref[:, kb * tk:(kb + 1) * tk].astype(BF16)
        s = _dot(qbd, kblk) + expand(g_blk) - g_q
        m, l, acc = _softmax_step(s, vblk, m, l, acc, v_t=True)
    o_ref[...] = _diag_out(acc / l, diag, tq).astype(o_ref.dtype)


def _dsa_sample_kernel(qidx_ref, miscn_ref, kin_ref, q_ref, kn_ref, vn_ref, kip_ref, kp_ref, vp_ref, o_ref,
                       keyp_scr, keyn_scr, *, topk, pos0):
    tq = q_ref.shape[0]
    past = kp_ref.shape[1]
    lane = lax.broadcasted_iota(jnp.int32, (1, LANES), 1)
    qi = qidx_ref[...].astype(F32)
    qs = jnp.concatenate([qi[:, h * IDX_DIM:(h + 1) * IDX_DIM] for h in range(IDX_HEADS)], axis=0).astype(BF16)
    wq = miscn_ref[...]
    ws = jnp.concatenate([wq[:, WI_LANE + h:WI_LANE + h + 1] for h in range(IDX_HEADS)], axis=0)
    kip = kip_ref[...].astype(BF16)
    kin = _pad_rows(kin_ref[...][:, 0:IDX_DIM], LANES).astype(BF16)

    def head_sum(s):
        s = jnp.maximum(s, 0.0) * ws
        out = s[0:tq]
        for h in range(1, IDX_HEADS):
            out = out + s[h * tq:(h + 1) * tq]
        return out * IDX_HEAD_SCALE

    isc_p = head_sum(_dot(qs, kip))
    isc_n = head_sum(_dot_nt(qs, kin))
    qchunk = jnp.right_shift(pos0 + lax.broadcasted_iota(jnp.int32, (tq, 1), 0), CHUNK_SHIFT)
    ppos = lax.broadcasted_iota(jnp.int32, (1, past), 1)
    npos = pos0 + lane
    ok_p = jnp.right_shift(ppos, CHUNK_SHIFT) <= qchunk
    ok_n = (jnp.right_shift(npos, CHUNK_SHIFT) <= qchunk) & (lane < tq)
    isc_p = jnp.where(ok_p, isc_p, -jnp.inf)
    isc_n = jnp.where(ok_n, isc_n, -jnp.inf)
    keyp_scr[...] = _sort_key(isc_p)
    keyn_scr[...] = jnp.where(lane < tq, _sort_key(isc_n), INT_MIN)
    bias_p, bias_n = _topk_select([lambda: keyp_scr[...], lambda: keyn_scr[...]], [ppos, npos], tq, topk,
                                  int(past + LANES).bit_length())
    bias_p = _head_rows(bias_p, N_HEADS)
    bias_n = _head_rows(bias_n, N_HEADS)

    qbd, diag = _block_diag_q(q_ref[...], tq)
    kn = _pad_rows(kn_ref[...], LANES).astype(BF16)
    vn = _pad_rows(vn_ref[...], LANES).astype(BF16)
    s_p = _dot(qbd, kp_ref[...].astype(BF16)) + bias_p
    s_n = _dot_nt(qbd, kn) + bias_n
    m = jnp.maximum(jnp.max(s_p, axis=-1, keepdims=True), jnp.max(s_n, axis=-1, keepdims=True))
    e_p = jnp.exp(s_p - m)
    e_n = jnp.exp(s_n - m)
    l = jnp.sum(e_p, axis=-1, keepdims=True) + jnp.sum(e_n, axis=-1, keepdims=True)
    o = (_dot_nt(e_p.astype(BF16), vp_ref[...].astype(BF16)) + _dot(e_n.astype(BF16), vn)) / l
    o_ref[...] = _diag_out(o, diag, tq).astype(o_ref.dtype)


def _sample_attention(layer, q_sb, k_sb, v_sb, q_fx, k_fx, v_fx, q_ds, k_ds, v_ds, qidx, misc,
                      c_sb_k, c_sb_v, c_fx_k, c_fx_v, lf_t, c_ds_k, c_ds_v, c_kidx):
    b, tq, _ = q_sb.shape
    past = c_sb_k.shape[3]
    tk = min(SAMPLE_KEYS, past)
    new = lambda w: pl.BlockSpec((None, tq, w), lambda bb: (bb, 0, 0))
    cache = lambda w: pl.BlockSpec((None, None, w, past), lambda bb: (layer, bb, 0, 0))
    out = jax.ShapeDtypeStruct((b, tq, MIX_W), BF16)
    cp = _cparams(("arbitrary",))
    o_sb = pl.pallas_call(
        functools.partial(_sb_sample_kernel, tk=tk), grid=(b,),
        in_specs=[new(MIX_W), new(MIX_W), new(MIX_W), cache(MIX_W), cache(MIX_W)],
        out_specs=new(MIX_W), out_shape=out, compiler_params=cp, name="sb_sample",
    )(q_sb, k_sb, v_sb, c_sb_k, c_sb_v)
    hp = lf_t.shape[2]
    o_fx = pl.pallas_call(
        functools.partial(_fox_sample_kernel, tk=tk), grid=(b,),
        in_specs=[new(MIX_W), new(MIX_W), new(MIX_W), new(LANES), cache(MIX_W), cache(MIX_W),
                  cache(hp)],
        out_specs=new(MIX_W), out_shape=out, compiler_params=cp, name="fox_sample",
    )(q_fx, k_fx, v_fx, misc, c_fx_k, c_fx_v, lf_t)
    topk = min(TOPK_MAX, (past + tq) // 4)
    o_ds = pl.pallas_call(
        functools.partial(_dsa_sample_kernel, topk=topk, pos0=past), grid=(b,),
        in_specs=[new(IDX_HEADS * IDX_DIM), new(LANES), new(LANES), new(MIX_W), new(MIX_W), new(MIX_W),
                  cache(IDX_DIM), cache(MIX_W), cache(MIX_W)],
        out_specs=new(MIX_W), out_shape=out,
        scratch_shapes=[pltpu.VMEM((tq, past), jnp.int32), pltpu.VMEM((tq, LANES), jnp.int32)],
        compiler_params=cp, name="dsa_sample",
    )(qidx, misc, misc, q_ds, k_ds, v_ds, c_kidx, c_ds_k, c_ds_v)
    return o_sb, o_fx, o_ds


def _rope_tables(pos):
    half = HEAD_DIM // 2
    inv_freq = ROPE_THETA ** (-jnp.arange(half, dtype=F32) / half)
    ang = pos.astype(F32)[:, None] * inv_freq[None, :]
    cos = jnp.cos(ang)
    sin = jnp.sin(ang)
    cos_t = jnp.concatenate([cos, cos, cos, cos], axis=1)
    sin_t = jnp.concatenate([-sin, sin, -sin, sin], axis=1)
    return cos_t, sin_t


def _prep_weights(w_in, b_forget, ffn_w_in, ffn_w_out, w_branch, w_out):
    w3 = 3 * MIX_W
    o = np.cumsum([0, w3, w3, N_HEADS, w3, IDX_HEADS * IDX_DIM, IDX_DIM, IDX_HEADS]).tolist()
    sb, fx, fl, ds, qi, ki, wi = (w_in[:, :, o[k]:o[k + 1]] for k in range(7))
    pad = jnp.zeros(w_in.shape[:2] + (LANES - IDX_DIM - N_HEADS - IDX_HEADS,), w_in.dtype)
    w_main = jnp.concatenate([sb, fx, ds, qi, ki, fl, wi, pad], axis=2).astype(BF16)
    w_gate = w_in[:, :, o[7]:].astype(BF16)
    depth = w_in.shape[0]
    bf_row = jnp.zeros((depth, 1, LANES), F32).at[:, 0, LF_LANE:LF_LANE + N_HEADS].set(b_forget)
    return (w_main, w_gate, bf_row, ffn_w_in.astype(BF16), ffn_w_out.astype(BF16),
            w_branch.astype(BF16), w_out.astype(BF16))


def _trunk(x3, mod_l, gdiv, tm, pos, caches, norm_g, weights, *, tq_attn):
    w_main, w_gate, bf_row, w_up, w_dn, w_branch, w_out = weights
    b, t, d = x3.shape
    n = b * t
    depth = w_main.shape[0]
    x = x3.reshape(n, d)
    cos_t, sin_t = _rope_tables(pos)
    if caches is not None:
        cos_t = jnp.tile(cos_t, (tm // t, 1))
        sin_t = jnp.tile(sin_t, (tm // t, 1))
        lf_t = jnp.swapaxes(caches[4], 2, 3)
        lf_t = jnp.pad(lf_t, ((0, 0), (0, 0), (0, 16 - lf_t.shape[2]), (0, 0)))
    states = []
    for l in range(depth):
        g = lambda k: norm_g[l, k][None, :]
        mod = mod_l[l]
        x = _ffn(x, mod, 0, g(0), g(1), w_up[l, 0], w_dn[l, 0], tm=tm, gdiv=gdiv)
        (q_sb, k_sb, v_sb, q_fx, k_fx, v_fx, q_ds, k_ds, v_ds, qidx, kidx2, misc, *state_t) = _proj(
            x, mod, g(2), w_main[l], bf_row[l], cos_t, sin_t, tm=tm, gdiv=gdiv,
            batch_t=b if caches is None else None)
        r3 = lambda a: a.reshape(b, t, a.shape[-1])
        if caches is None:
            qa, ka = _fox_prep(r3(q_fx), r3(k_fx), r3(misc), tm=min(256, t))
            o_sb = _sb_prompt(r3(q_sb), r3(k_sb), r3(v_sb), tq=tq_attn, tk=min(512, t))
            o_fx = _fox_prompt(qa, ka, r3(v_fx), tq=tq_attn, tk=min(512, t))
            o_ds = _dsa_prompt(r3(qidx), r3(kidx2), r3(misc), r3(q_ds), r3(k_ds), r3(v_ds), tq=tq_attn)
        else:
            o_sb, o_fx, o_ds = _sample_attention(
                l, r3(q_sb), r3(k_sb), r3(v_sb), r3(q_fx), r3(k_fx), r3(v_fx), r3(q_ds), r3(k_ds), r3(v_ds),
                r3(qidx), r3(misc), caches[0], caches[1], caches[2], caches[3], lf_t,
                caches[5], caches[6], caches[7])
        o2 = lambda a: a.reshape(n, MIX_W)
        x = _merge(x, mod, g(2), g(3), o2(o_sb), o2(o_fx), o2(o_ds), w_gate[l], w_branch[l], w_out[l],
                   tm=tm, gdiv=gdiv)
        x = _ffn(x, mod, 6, g(4), g(5), w_up[l, 1], w_dn[l, 1], tm=tm, gdiv=gdiv)
        if state_t:
            hd = lambda a: jnp.transpose(a.reshape(b, N_HEADS, HEAD_DIM, t), (0, 3, 1, 2))
            kv = [hd(a) for a in state_t]
        else:
            hd = lambda a: a.reshape(b, t, N_HEADS, HEAD_DIM)
            kv = [hd(a) for a in (k_sb, v_sb, k_fx, v_fx, k_ds, v_ds)]
        states.append((kv[0], kv[1], kv[2], kv[3],
                       misc[:, LF_LANE:LF_LANE + N_HEADS].reshape(b, t, N_HEADS),
                       kv[4], kv[5], misc[:, 0:IDX_DIM].reshape(b, t, IDX_DIM)))
    return x.reshape(b, t, d), tuple(jnp.stack(s, axis=0) for s in zip(*states))


def kernel(x_prompt, x_sample, cache_sb_k, cache_sb_v, cache_fox_k, cache_fox_v, cache_fox_logf, cache_dsa_k, cache_dsa_v, cache_dsa_kidx, c_prompt, c_sample, norm_g, w_ada, b_ada, ffn_w_in, ffn_w_out, w_in, b_forget, w_branch, w_out):
    bp, tp, d = x_prompt.shape
    bs, ts, _ = x_sample.shape
    depth = w_in.shape[0]
    past = cache_sb_k.shape[2]
    mix = lambda a: jnp.transpose(a, (0, 1, 3, 4, 2)).reshape(a.shape[:2] + (MIX_W, a.shape[2]))
    caches = (mix(cache_sb_k), mix(cache_sb_v), mix(cache_fox_k), mix(cache_fox_v), cache_fox_logf,
              mix(cache_dsa_k), mix(cache_dsa_v), jnp.swapaxes(cache_dsa_kidx, 2, 3))
    weights = _prep_weights(w_in, b_forget, ffn_w_in, ffn_w_out, w_branch, w_out)

    rows = bp + bs
    rows_pad = -(-rows // 8) * 8
    c_all = jnp.concatenate([c_prompt, c_sample, jnp.zeros((rows_pad - rows, d), F32)], axis=0)
    mod = _ada(c_all, w_ada, b_ada)
    mod_p = [mod[l, :bp].reshape(bp, N_MOD, 1, d) for l in range(depth)]
    ns = bs * ts
    mod_s = [jnp.repeat(mod[l, bp:rows].reshape(bs, N_MOD, d), ts, axis=0)
             .reshape(1, ns, N_MOD, d).transpose(0, 2, 1, 3) for l in range(depth)]

    tm_p = min(512, tp)
    pos_p = jnp.arange(tp, dtype=jnp.int32)
    pos_s = past + jnp.arange(ts, dtype=jnp.int32)
    y_p, st_p = _trunk(x_prompt, mod_p, tp // tm_p, tm_p, pos_p, None, norm_g, weights, tq_attn=min(256, tp))
    y_s, st_s = _trunk(x_sample, mod_s, 1, ns, pos_s, caches, norm_g, weights, tq_attn=ts)
    return (y_p, y_s) + st_p + st_s
```

```python
import functools

import numpy as np
import jax
import jax.numpy as jnp
from jax import lax
from jax.experimental import pallas as pl
from jax.experimental.pallas import tpu as pltpu

F32 = jnp.float32
BF16 = jnp.bfloat16

HEAD_DIM = 64
N_HEADS = 6
MIX_W = N_HEADS * HEAD_DIM
N_PAIR = MIX_W // 128
IDX_HEADS = 4
IDX_DIM = 64
CHUNK = 64
CHUNK_SHIFT = 6
TOPK_MAX = 256
N_BRANCH = 3
N_MOD = 9
ROPE_THETA = 10000.0
EPS = 1e-6
FFN_RES = 0.5
QK_SCALE = HEAD_DIM ** -0.5
IDX_SCALE = IDX_DIM ** -0.5
IDX_HEAD_SCALE = IDX_HEADS ** -0.5
LANES = 128
LF_LANE = IDX_DIM
WI_LANE = IDX_DIM + N_HEADS
PROJ_W = 3 * 3 * MIX_W + IDX_HEADS * IDX_DIM + LANES
VMEM_LIMIT = 56 * 1024 * 1024
SEARCH_UNROLL = 4
SAMPLE_KEYS = 2048
INT_MIN = np.int32(-2 ** 31)
KEY_NEG_INF = np.int32(-2139095041)
NEG_BIG = -1e30


def _cparams(sem):
    return pltpu.CompilerParams(dimension_semantics=sem, vmem_limit_bytes=VMEM_LIMIT)


def _dot(a, b):
    return jnp.dot(a, b, preferred_element_type=F32)


def _dot_nt(a, b):
    return lax.dot_general(a, b, (((1,), (1,)), ((), ())), preferred_element_type=F32)


def _split2(x):
    hi = x.astype(BF16)
    lo = (x - hi.astype(F32)).astype(BF16)
    return hi, lo


def _split3(x):
    p1 = x.astype(BF16)
    r1 = x - p1.astype(F32)
    p2 = r1.astype(BF16)
    p3 = (r1 - p2.astype(F32)).astype(BF16)
    return p1, p2, p3


def _log_sigmoid(x):
    return jnp.minimum(x, 0.0) - jnp.log1p(jnp.exp(-jnp.abs(x)))


def _rms(x, g):
    return x * lax.rsqrt(jnp.mean(x * x, axis=-1, keepdims=True) + EPS) * g


def _norm_mod(x, g, scale, shift):
    return _rms(x, g) * (1.0 + scale) + shift


def _strict_upper(n):
    r = lax.broadcasted_iota(jnp.int32, (n, n), 0)
    c = lax.broadcasted_iota(jnp.int32, (n, n), 1)
    return jnp.where(r > c, 1.0, 0.0).astype(BF16)


def _strict_lower(n):
    r = lax.broadcasted_iota(jnp.int32, (n, n), 0)
    c = lax.broadcasted_iota(jnp.int32, (n, n), 1)
    return jnp.where(c > r, 1.0, 0.0).astype(BF16)


def _ada_kernel(c_ref, w_ref, b_ref, o_ref):
    c = c_ref[...]
    s = (c * jax.nn.sigmoid(c)).astype(BF16)
    o_ref[0] = _dot(s, w_ref[0].astype(BF16)) + b_ref[0]


def _ada(c_all, w_ada, b_ada):
    depth, d, n = w_ada.shape
    rows = c_all.shape[0]
    tn = n // 8
    return pl.pallas_call(
        _ada_kernel,
        grid=(depth, n // tn),
        in_specs=[
            pl.BlockSpec((rows, d), lambda l, j: (0, 0)),
            pl.BlockSpec((1, d, tn), lambda l, j: (l, 0, j)),
            pl.BlockSpec((1, 1, tn), lambda l, j: (l, 0, j)),
        ],
        out_specs=pl.BlockSpec((1, rows, tn), lambda l, j: (l, 0, j)),
        out_shape=jax.ShapeDtypeStruct((depth, rows, n), F32),
        compiler_params=_cparams(("arbitrary", "arbitrary")),
        name="ada",
    )(c_all, w_ada, b_ada.reshape(depth, 1, n))


def _ffn_kernel(x_ref, shift_ref, scale_ref, gate_ref, gpre_ref, gpost_ref, wg_ref, wu_ref, wd_ref,
                o_ref, acc_scr, *, nj):
    j = pl.program_id(1)
    for jj in range(nj):
        @pl.when(j == jj)
        def _():
            x = x_ref[...]
            h = _norm_mod(x, gpre_ref[...], scale_ref[0, 0], shift_ref[0, 0]).astype(BF16)
            g = _dot(h, wg_ref[...])
            u = _dot(h, wu_ref[...])
            a = (g * jax.nn.sigmoid(g) * u).astype(BF16)
            d = _dot(a, wd_ref[...])
            acc = d if jj == 0 else acc_scr[...] + d
            if jj < nj - 1:
                acc_scr[...] = acc
            else:
                o_ref[...] = x + FFN_RES * gate_ref[0, 0] * _rms(acc, gpost_ref[...])


def _mod_spec(r, d, gdiv, comp):
    return pl.BlockSpec((1, 1, r, d), lambda i, *_: (i // gdiv, comp, 0, 0))


def _ffn(x, mod, comp0, g_pre, g_post, w_up, w_dn, *, tm, gdiv):
    n, d = x.shape
    ff = w_dn.shape[0]
    nj = 2 if (ff // 2) % LANES == 0 else 1
    tf = ff // nj
    r = mod.shape[2]
    return pl.pallas_call(
        functools.partial(_ffn_kernel, nj=nj),
        grid=(n // tm, nj),
        in_specs=[
            pl.BlockSpec((tm, d), lambda i, j: (i, 0)),
            _mod_spec(r, d, gdiv, comp0),
            _mod_spec(r, d, gdiv, comp0 + 1),
            _mod_spec(r, d, gdiv, comp0 + 2),
            pl.BlockSpec((1, d), lambda i, j: (0, 0)),
            pl.BlockSpec((1, d), lambda i, j: (0, 0)),
            pl.BlockSpec((d, tf), lambda i, j: (0, j)),
            pl.BlockSpec((d, tf), lambda i, j: (0, nj + j)),
            pl.BlockSpec((tf, d), lambda i, j: (j, 0)),
        ],
        out_specs=pl.BlockSpec((tm, d), lambda i, j: (i, 0)),
        out_shape=jax.ShapeDtypeStruct((n, d), F32),
        scratch_shapes=[pltpu.VMEM((tm, d), F32)],
        compiler_params=_cparams(("arbitrary", "arbitrary")),
        name="ffn",
    )(x, mod, mod, mod, g_pre, g_post, w_up, w_up, w_dn)


def _rope(x, cos, sin, first_half):
    outs = []
    for c in range(x.shape[1] // LANES):
        xs = x[:, c * LANES:(c + 1) * LANES]
        below = pltpu.roll(xs, 32, 1)
        above = pltpu.roll(xs, LANES - 32, 1)
        outs.append(xs * cos + jnp.where(first_half, above, below) * sin)
    return outs[0] if len(outs) == 1 else jnp.concatenate(outs, axis=1)


def _proj_kernel(x_ref, shift_ref, scale_ref, gpre_ref, w_ref, bf_ref, cos_ref, sin_ref,
                 qsb, ksb, vsb, qfx, kfx, vfx, qds, kds, vds, qidx, kidx2, misc, *state_t):
    h = _norm_mod(x_ref[...], gpre_ref[...], scale_ref[0, 0], shift_ref[0, 0]).astype(BF16)
    cos = cos_ref[...]
    sin = sin_ref[...]
    lane = lax.broadcasted_iota(jnp.int32, (1, LANES), 1)
    first_half = (lane & (HEAD_DIM - 1)) < (HEAD_DIM // 2)
    w3 = 3 * MIX_W

    def put_kv(k, v, k_ref, v_ref, branch):
        k_ref[...] = k.astype(k_ref.dtype)
        v_ref[...] = v.astype(v_ref.dtype)
        if state_t:
            state_t[2 * branch][...] = k.T
            state_t[2 * branch + 1][...] = v.T

    y = _dot(h, w_ref[:, 0:w3])
    qsb[...] = (y[:, 0:MIX_W] * QK_SCALE).astype(BF16)
    put_kv(y[:, MIX_W:2 * MIX_W], y[:, 2 * MIX_W:w3], ksb, vsb, 0)

    y = _dot(h, w_ref[:, w3:2 * w3])
    qfx[...] = (y[:, 0:MIX_W] * QK_SCALE).astype(BF16)
    put_kv(y[:, MIX_W:2 * MIX_W], y[:, 2 * MIX_W:w3], kfx, vfx, 1)

    y = _dot(h, w_ref[:, 2 * w3:3 * w3])
    qds[...] = (_rope(y[:, 0:MIX_W], cos, sin, first_half) * QK_SCALE).astype(BF16)
    put_kv(_rope(y[:, MIX_W:2 * MIX_W], cos, sin, first_half), y[:, 2 * MIX_W:w3], kds, vds, 2)

    y = _dot(h, w_ref[:, 3 * w3:PROJ_W])
    nq = IDX_HEADS * IDX_DIM
    qidx[...] = (_rope(y[:, 0:nq], cos, sin, first_half) * IDX_SCALE).astype(BF16)
    m = y[:, nq:nq + LANES]
    m_rot = _rope(m, cos, sin, first_half)
    lf = _log_sigmoid(m + bf_ref[...])
    misc[...] = jnp.where(lane < LF_LANE, m_rot,
                          jnp.where(lane < WI_LANE, lf,
                                    jnp.where(lane < WI_LANE + IDX_HEADS, m, 0.0)))
    kidx2[...] = jnp.where(lane < IDX_DIM, m_rot, pltpu.roll(m_rot, IDX_DIM, 1)).astype(BF16)


def _proj(x, mod, g_pre, w_main, bf_row, cos_t, sin_t, *, tm, gdiv, batch_t=None):
    n, d = x.shape
    r = mod.shape[2]
    tab_tiles = cos_t.shape[0] // tm
    row = lambda w: pl.BlockSpec((tm, w), lambda i: (i, 0))
    tab = pl.BlockSpec((tm, LANES), lambda i: (i % tab_tiles, 0))
    shp = lambda w, dt: jax.ShapeDtypeStruct((n, w), dt)
    nq = IDX_HEADS * IDX_DIM
    kv_dt = F32 if batch_t is None else BF16
    out_specs = [row(MIX_W)] * 9 + [row(nq), row(LANES), row(LANES)]
    out_shape = [shp(MIX_W, BF16), shp(MIX_W, kv_dt), shp(MIX_W, kv_dt)] * 3 + [
        shp(nq, BF16), shp(LANES, BF16), shp(LANES, F32)]
    if batch_t is not None:
        t = n // batch_t
        out_specs += [pl.BlockSpec((None, MIX_W, tm), lambda i: (i // gdiv, 0, i % gdiv))] * 6
        out_shape += [jax.ShapeDtypeStruct((batch_t, MIX_W, t), F32)] * 6
    return pl.pallas_call(
        _proj_kernel,
        grid=(n // tm,),
        in_specs=[
            row(d),
            _mod_spec(r, d, gdiv, 3),
            _mod_spec(r, d, gdiv, 4),
            pl.BlockSpec((1, d), lambda i: (0, 0)),
            pl.BlockSpec((d, PROJ_W), lambda i: (0, 0)),
            pl.BlockSpec((1, LANES), lambda i: (0, 0)),
            tab, tab,
        ],
        out_specs=out_specs,
        out_shape=out_shape,
        compiler_params=_cparams(("arbitrary",)),
        name="proj",
    )(x, mod, mod, g_pre, w_main, bf_row, cos_t, sin_t)


def _merge_kernel(x_ref, shift_ref, scale_ref, gate_ref, gpre_ref, gpost_ref,
                  osb_ref, ofx_ref, ods_ref, wg_ref, wb_ref, wo_ref, o_ref):
    x = x_ref[...]
    d = x.shape[1]
    h = _norm_mod(x, gpre_ref[...], scale_ref[0, 0], shift_ref[0, 0]).astype(BF16)
    merged = None
    for nb, o_br in enumerate((osb_ref, ofx_ref, ods_ref)):
        gl = _dot(h, wg_ref[:, nb * d:(nb + 1) * d])
        y = _dot(o_br[...], wb_ref[nb])
        t = jax.nn.sigmoid(gl) * y
        merged = t if merged is None else merged + t
    out = _dot(merged.astype(BF16), wo_ref[...])
    o_ref[...] = x + gate_ref[0, 0] * _rms(out, gpost_ref[...])


def _merge(x, mod, g_pre, g_post, o_sb, o_fx, o_ds, w_gate, w_branch, w_out, *, tm, gdiv):
    n, d = x.shape
    r = mod.shape[2]
    row = lambda w: pl.BlockSpec((tm, w), lambda i: (i, 0))
    return pl.pallas_call(
        _merge_kernel,
        grid=(n // tm,),
        in_specs=[
            row(d),
            _mod_spec(r, d, gdiv, 3),
            _mod_spec(r, d, gdiv, 4),
            _mod_spec(r, d, gdiv, 5),
            pl.BlockSpec((1, d), lambda i: (0, 0)),
            pl.BlockSpec((1, d), lambda i: (0, 0)),
            row(MIX_W), row(MIX_W), row(MIX_W),
            pl.BlockSpec((d, N_BRANCH * d), lambda i: (0, 0)),
            pl.BlockSpec((N_BRANCH, MIX_W, d), lambda i: (0, 0, 0)),
            pl.BlockSpec((d, d), lambda i: (0, 0)),
        ],
        out_specs=row(d),
        out_shape=jax.ShapeDtypeStruct((n, d), F32),
        compiler_params=_cparams(("arbitrary",)),
        name="merge",
    )(x, mod, mod, mod, g_pre, g_post, o_sb, o_fx, o_ds, w_gate, w_branch, w_out)


def _aug_lanes(lane, base, ones_first, parts):
    one_lo, val_lo = (base, base + 3) if ones_first else (base + 3, base)
    out = jnp.where((lane >= one_lo) & (lane < one_lo + 3), 1.0, 0.0)
    for k, p in enumerate(parts):
        out = jnp.where(lane == val_lo + k, p, out)
    return out


def _fox_prep_kernel(q_ref, k_ref, misc_ref, qa_ref, ka_ref, carry):
    j = pl.program_id(1)

    @pl.when(j == 0)
    def _():
        carry[...] = jnp.zeros_like(carry)

    lf = misc_ref[...]
    tm = lf.shape[0]
    low = _strict_lower(tm)
    p1, p2, p3 = _split3(lf)
    g = _dot(low, p1) + _dot(low, p2) + _dot(low, p3) + carry[...]
    carry[...] += jnp.sum(lf, axis=0, keepdims=True)

    lane = lax.broadcasted_iota(jnp.int32, (1, LANES), 1)
    lo_half = lane < HEAD_DIM
    for p in range(N_PAIR):
        qp = q_ref[:, p * LANES:(p + 1) * LANES].astype(F32)
        kp = k_ref[:, p * LANES:(p + 1) * LANES]
        for a in range(2):
            hd = 2 * p + a
            gcol = g[:, LF_LANE + hd:LF_LANE + hd + 1]
            g1, g2, g3 = (t.astype(F32) for t in _split3(gcol))
            own = lo_half if a == 0 else jnp.logical_not(lo_half)
            base = HEAD_DIM if a == 0 else 0
            q_aug = jnp.where(own, qp, _aug_lanes(lane, base, True, (-g1, -g2, -g3)))
            k_aug = jnp.where(own, kp, _aug_lanes(lane, base, False, (g1, g2, g3)))
            qa_ref[:, hd * LANES:(hd + 1) * LANES] = q_aug.astype(BF16)
            ka_ref[:, hd * LANES:(hd + 1) * LANES] = k_aug.astype(BF16)


def _fox_prep(q, k, misc, *, tm):
    b, t, _ = q.shape
    nt = t // tm
    rev = lambda w: pl.BlockSpec((None, tm, w), lambda bb, j: (bb, nt - 1 - j, 0))
    return pl.pallas_call(
        _fox_prep_kernel,
        grid=(b, nt),
        in_specs=[rev(MIX_W), rev(MIX_W), rev(LANES)],
        out_specs=[rev(N_HEADS * LANES), rev(N_HEADS * LANES)],
        out_shape=[jax.ShapeDtypeStruct((b, t, N_HEADS * LANES), BF16)] * 2,
        scratch_shapes=[pltpu.VMEM((1, LANES), F32)],
        compiler_params=_cparams(("arbitrary", "arbitrary")),
        name="fox_prep",
    )(q, k, misc)


def _sb_block(qh, kblk, vblk, mask, upper, acc, c, kv_t=False):
    z = _dot(qh, kblk) if kv_t else _dot_nt(qh, kblk)
    neg_abs = lax.bitcast_convert_type(lax.bitcast_convert_type(z, jnp.int32) | INT_MIN, F32)
    ls = jnp.minimum(z, 0.0) - jnp.log(1.0 + jnp.exp(neg_abs))
    lk = ls - z
    if mask is not None:
        lk = jnp.where(mask, lk, 0.0)
    hi, lo = _split2(lk)
    sub = upper.shape[0]
    tot = []
    for j in range(z.shape[1] // sub - 1, -1, -1):
        sl = slice(j * sub, (j + 1) * sub)
        tot.append(_dot(hi[:, sl], upper) + _dot(lo[:, sl], upper) + c)
        c = c + jnp.sum(lk[:, sl], axis=-1, keepdims=True)
    tot = tot[0] if len(tot) == 1 else jnp.concatenate(tot[::-1], axis=1)
    a = jnp.exp(ls + tot)
    if mask is not None:
        a = jnp.where(mask, a, 0.0)
    a = a.astype(BF16)
    acc = acc + (_dot_nt(a, vblk) if kv_t else _dot(a, vblk))
    return acc, c


def _sb_kernel(q_ref, k_ref, v_ref, o_ref, *, tq, tk):
    i = pl.program_id(1)
    lane = lax.broadcasted_iota(jnp.int32, (1, LANES), 1)
    lo_half = lane < HEAD_DIM
    q2 = []
    for p in range(N_PAIR):
        q = q_ref[:, p * LANES:(p + 1) * LANES]
        zero = jnp.zeros_like(q)
        q2.append(jnp.concatenate([jnp.where(lo_half, q, zero), jnp.where(lo_half, zero, q)], axis=0))
    qpos = i * tq + (lax.broadcasted_iota(jnp.int32, (2 * tq, 1), 0) & (tq - 1))
    upper = _strict_upper(LANES)
    nkb = jnp.right_shift((i + 1) * tq + (tk - 1), tk.bit_length() - 1)

    def body(it, carry):
        start = pl.multiple_of((nkb - 1 - it) * tk, tk)
        mask = (start + lax.broadcasted_iota(jnp.int32, (1, tk), 1)) < qpos
        out = []
        for p in range(N_PAIR):
            kblk = k_ref[pl.ds(start, tk), p * LANES:(p + 1) * LANES].astype(BF16)
            vblk = v_ref[pl.ds(start, tk), p * LANES:(p + 1) * LANES].astype(BF16)
            out.extend(_sb_block(q2[p], kblk, vblk, mask, upper, carry[2 * p], carry[2 * p + 1]))
        return tuple(out)

    init = (jnp.zeros((2 * tq, LANES), F32), jnp.zeros((2 * tq, 1), F32)) * N_PAIR
    fin = lax.fori_loop(0, nkb, body, init)
    for p in range(N_PAIR):
        acc = fin[2 * p]
        o_ref[:, p * LANES:(p + 1) * LANES] = jnp.where(lo_half, acc[0:tq], acc[tq:2 * tq]).astype(o_ref.dtype)


def _sb_prompt(q, k, v, *, tq, tk):
    b, t, _ = q.shape
    full = pl.BlockSpec((None, t, MIX_W), lambda bb, i: (bb, 0, 0))
    rows = pl.BlockSpec((None, tq, MIX_W), lambda bb, i: (bb, i, 0))
    return pl.pallas_call(
        functools.partial(_sb_kernel, tq=tq, tk=tk),
        grid=(b, t // tq),
        in_specs=[rows, full, full],
        out_specs=rows,
        out_shape=jax.ShapeDtypeStruct((b, t, MIX_W), BF16),
        compiler_params=_cparams(("arbitrary", "arbitrary")),
        name="sb_prompt",
    )(q, k, v)


def _softmax_step(s, vblk, m, l, acc, v_t=False):
    m_new = jnp.maximum(m, jnp.max(s, axis=-1, keepdims=True))
    alpha = jnp.exp(m - m_new)
    p = jnp.exp(s - m_new)
    l = alpha * l + jnp.sum(p, axis=-1, keepdims=True)
    p = p.astype(BF16)
    acc = alpha * acc + (_dot_nt(p, vblk) if v_t else _dot(p, vblk))
    return m_new, l, acc


def _fox_kernel(qa_ref, ka_ref, v_ref, o_ref, *, tq, tk):
    i = pl.program_id(1)
    qpos = i * tq + lax.broadcasted_iota(jnp.int32, (tq, 1), 0)
    nkb = jnp.right_shift((i + 1) * tq + (tk - 1), tk.bit_length() - 1)

    def body(kb, carry):
        start = pl.multiple_of(kb * tk, tk)
        kpos = start + lax.broadcasted_iota(jnp.int32, (1, tk), 1)
        mask = kpos <= qpos
        out = []
        for p in range(N_PAIR):
            vblk = v_ref[pl.ds(start, tk), p * LANES:(p + 1) * LANES].astype(BF16)
            for a in range(2):
                hd = 2 * p + a
                m, l, acc = carry[3 * hd:3 * hd + 3]
                s = _dot_nt(qa_ref[:, hd * LANES:(hd + 1) * LANES],
                            ka_ref[pl.ds(start, tk), hd * LANES:(hd + 1) * LANES])
                s = jnp.where(mask, s, -jnp.inf)
                out.extend(_softmax_step(s, vblk, m, l, acc))
        return tuple(out)

    init = (jnp.full((tq, 1), NEG_BIG, F32), jnp.zeros((tq, 1), F32), jnp.zeros((tq, LANES), F32)) * N_HEADS
    fin = lax.fori_loop(0, nkb, body, init)
    lane = lax.broadcasted_iota(jnp.int32, (1, LANES), 1)
    for p in range(N_PAIR):
        o0 = fin[6 * p + 2] / fin[6 * p + 1]
        o1 = fin[6 * p + 5] / fin[6 * p + 4]
        o_ref[:, p * LANES:(p + 1) * LANES] = jnp.where(lane < HEAD_DIM, o0, o1).astype(o_ref.dtype)


def _fox_prompt(qa, ka, v, *, tq, tk):
    b, t, _ = v.shape
    return pl.pallas_call(
        functools.partial(_fox_kernel, tq=tq, tk=tk),
        grid=(b, t // tq),
        in_specs=[
            pl.BlockSpec((None, tq, N_HEADS * LANES), lambda bb, i: (bb, i, 0)),
            pl.BlockSpec((None, t, N_HEADS * LANES), lambda bb, i: (bb, 0, 0)),
            pl.BlockSpec((None, t, MIX_W), lambda bb, i: (bb, 0, 0)),
        ],
        out_specs=pl.BlockSpec((None, tq, MIX_W), lambda bb, i: (bb, i, 0)),
        out_shape=jax.ShapeDtypeStruct((b, t, MIX_W), BF16),
        compiler_params=_cparams(("arbitrary", "arbitrary")),
        name="fox_prompt",
    )(qa, ka, v)


def _sort_key(x):
    bits = lax.bitcast_convert_type(x, jnp.int32)
    key = jnp.where(bits < 0, bits ^ jnp.int32(0x7FFFFFFF), bits)
    return jnp.where(x == 0.0, 0, key)


def _count(pred_parts):
    tot = None
    for preds in pred_parts:
        ind = jnp.where(preds[-1], 1.0, 0.0)
        for p in preds[-2::-1]:
            ind = jnp.where(p, ind, 0.0)
        c = jnp.sum(ind, axis=-1, keepdims=True)
        tot = c if tot is None else tot + c
    return tot


def _kth_threshold(key_gets, rows, kf):
    def step(it, thr):
        cand = thr + jnp.left_shift(jnp.int32(1), 31 - it)
        cnt = _count([(kg() >= cand,) for kg in key_gets])
        return jnp.where(cnt >= kf, cand, thr)

    return lax.fori_loop(0, 32, step, jnp.full((rows, 1), INT_MIN, jnp.int32), unroll=SEARCH_UNROLL)


def _topk_select(key_gets, pos_list, rows, topk, idx_bits):
    kf = float(topk)
    thr = _kth_threshold(key_gets, rows, kf)
    need = kf - _count([(kg() > thr,) for kg in key_gets])
    n_eq = _count([(kg() == thr,) for kg in key_gets])

    def ibody(it, bound):
        cand = bound + jnp.left_shift(jnp.int32(1), idx_bits - 1 - it)
        cnt = _count([(kg() == thr, pos < cand) for kg, pos in zip(key_gets, pos_list)])
        return jnp.where(cnt <= need, cand, bound)

    some_partial = jnp.max(jnp.where(n_eq > need, 1.0, 0.0)) > 0.0
    bound = lax.cond(
        some_partial,
        lambda: lax.fori_loop(0, idx_bits, ibody, jnp.zeros((rows, 1), jnp.int32)),
        lambda: jnp.full((rows, 1), 1 << idx_bits, jnp.int32))
    tie = jnp.where(thr > KEY_NEG_INF, 0.0, -jnp.inf)
    out = []
    for kg, pos in zip(key_gets, pos_list):
        key = kg()
        at_thr = jnp.where(pos < bound, tie, -jnp.inf)
        out.append(jnp.where(key > thr, 0.0, jnp.where(key == thr, at_thr, -jnp.inf)))
    return out


def _dsa_body(i, n_keys, qidx_ref, kidx2_ref, miscq_ref, q_ref, k_ref, v_ref, o_ref, key_scr, bias_scr, *, tq, topk):
    lane = lax.broadcasted_iota(jnp.int32, (1, LANES), 1)
    lo_half = lane < HEAD_DIM
    kidx2 = kidx2_ref[0:n_keys, :]
    wq = miscq_ref[...]
    isc = jnp.zeros((tq, n_keys), F32)
    for h in range(IDX_HEADS):
        qp = qidx_ref[:, (h // 2) * LANES:(h // 2 + 1) * LANES]
        own = lo_half if h % 2 == 0 else jnp.logical_not(lo_half)
        qh = jnp.where(own, qp, jnp.zeros_like(qp))
        s = _dot_nt(qh, kidx2)
        isc = isc + jnp.maximum(s, 0.0) * wq[:, WI_LANE + h:WI_LANE + h + 1]
    isc = isc * IDX_HEAD_SCALE
    qpos = i * tq + lax.broadcasted_iota(jnp.int32, (tq, 1), 0)
    kpos = lax.broadcasted_iota(jnp.int32, (1, n_keys), 1)
    allowed = jnp.right_shift(kpos, CHUNK_SHIFT) <= jnp.right_shift(qpos, CHUNK_SHIFT)
    isc = jnp.where(allowed, isc, -jnp.inf)
    key_scr[:, 0:n_keys] = _sort_key(isc)
    (bias,) = _topk_select([lambda: key_scr[:, 0:n_keys]], [kpos], tq, topk, int(n_keys).bit_length())
    bias_scr[:, 0:n_keys] = bias

    for p in range(N_PAIR):
        qp = q_ref[:, p * LANES:(p + 1) * LANES]
        kp = k_ref[0:n_keys, p * LANES:(p + 1) * LANES].astype(BF16)
        vp = v_ref[0:n_keys, p * LANES:(p + 1) * LANES].astype(BF16)
        outs = []
        for a in range(2):
            own = lo_half if a == 0 else jnp.logical_not(lo_half)
            qh = jnp.where(own, qp, jnp.zeros_like(qp))
            s = _dot_nt(qh, kp) + bias_scr[:, 0:n_keys]
            m = jnp.max(s, axis=-1, keepdims=True)
            e = jnp.exp(s - m)
            l = jnp.sum(e, axis=-1, keepdims=True)
            outs.append(_dot(e.astype(BF16), vp) / l)
        o_ref[:, p * LANES:(p + 1) * LANES] = jnp.where(lo_half, outs[0], outs[1]).astype(o_ref.dtype)


def _dsa_kernel(*refs, tq, topk, first_block):
    n_keys = refs[4].shape[0]
    _dsa_body(first_block + pl.program_id(1), n_keys, *refs, tq=tq, topk=topk)


def _dsa_prompt(qidx, kidx2, misc, q, k, v, *, tq):
    b, t, _ = q.shape
    topk = min(TOPK_MAX, t // 4)
    n_groups = 1
    for cand in (8, 4, 2):
        if (t // cand) % tq == 0 and t // cand >= topk:
            n_groups = cand
            break
    glen = t // n_groups
    outs = []
    for g in range(n_groups):
        n_keys = (g + 1) * glen
        first = g * (glen // tq)
        qrow = lambda w: pl.BlockSpec((None, tq, w), lambda bb, i, first=first: (bb, first + i, 0))
        keys = lambda w: pl.BlockSpec((None, n_keys, w), lambda bb, i: (bb, 0, 0))
        outs.append(pl.pallas_call(
            functools.partial(_dsa_kernel, tq=tq, topk=topk, first_block=first),
            grid=(b, glen // tq),
            in_specs=[qrow(IDX_HEADS * IDX_DIM), keys(LANES), qrow(LANES), qrow(MIX_W), keys(MIX_W), keys(MIX_W)],
            out_specs=pl.BlockSpec((None, tq, MIX_W), lambda bb, i: (bb, i, 0)),
            out_shape=jax.ShapeDtypeStruct((b, glen, MIX_W), BF16),
            scratch_shapes=[pltpu.VMEM((tq, n_keys), jnp.int32), pltpu.VMEM((tq, n_keys), F32)],
            compiler_params=_cparams(("arbitrary", "arbitrary")),
            name="dsa_prompt",
        )(qidx, kidx2, misc, q, k, v))
    return outs[0] if n_groups == 1 else jnp.concatenate(outs, axis=1)


def _head_rows(x, n_rep):
    return jnp.concatenate([x] * n_rep, axis=0)


def _block_diag_q(q, tq):
    rows = N_HEADS * tq
    rhead = jnp.right_shift(lax.broadcasted_iota(jnp.int32, (rows, 1), 0), tq.bit_length() - 1)
    lhead = jnp.right_shift(lax.broadcasted_iota(jnp.int32, (1, MIX_W), 1), HEAD_DIM.bit_length() - 1)
    qq = _head_rows(q, N_HEADS)
    return jnp.where(rhead == lhead, qq, jnp.zeros_like(qq)), rhead == lhead


def _diag_out(o_bd, diag, tq):
    o = jnp.where(diag, o_bd, 0.0)
    out = o[0:tq]
    for h in range(1, N_HEADS):
        out = out + o[h * tq:(h + 1) * tq]
    return out


def _pad_rows(x, rows):
    return jnp.concatenate([x, jnp.zeros((rows - x.shape[0], x.shape[1]), x.dtype)], axis=0)


def _sb_sample_kernel(q_ref, kn_ref, vn_ref, kp_ref, vp_ref, o_ref, *, tk):
    tq = q_ref.shape[0]
    past = kp_ref.shape[1]
    rows = N_HEADS * tq
    qbd, diag = _block_diag_q(q_ref[...], tq)
    trow = lax.broadcasted_iota(jnp.int32, (rows, 1), 0) & (tq - 1)
    kn = _pad_rows(kn_ref[...], LANES).astype(BF16)
    vn = _pad_rows(vn_ref[...], LANES).astype(BF16)
    mask_n = lax.broadcasted_iota(jnp.int32, (1, LANES), 1) < trow
    acc = jnp.zeros((rows, MIX_W), F32)
    c = jnp.zeros((rows, 1), F32)
    upper = _strict_upper(LANES)
    acc, c = _sb_block(qbd, kn, vn, mask_n, upper, acc, c)
    for kb in range(past // tk - 1, -1, -1):
        kblk = kp_ref[:, kb * tk:(kb + 1) * tk].astype(BF16)
        vblk = vp_ref[:, kb * tk:(kb + 1) * tk].astype(BF16)
        acc, c = _sb_block(qbd, kblk, vblk, None, upper, acc, c, kv_t=True)
    o_ref[...] = _diag_out(acc, diag, tq).astype(o_ref.dtype)


def _fox_sample_kernel(q_ref, kn_ref, vn_ref, miscn_ref, kp_ref, vp_ref, lfp_ref, o_ref, *, tk):
    tq = q_ref.shape[0]
    past = kp_ref.shape[1]
    rows = N_HEADS * tq
    hp = lfp_ref.shape[0]
    qbd, diag = _block_diag_q(q_ref[...], tq)
    trow = lax.broadcasted_iota(jnp.int32, (rows, 1), 0) & (tq - 1)
    lane = lax.broadcasted_iota(jnp.int32, (1, LANES), 1)

    misc_n = _pad_rows(miscn_ref[...], LANES)
    pick = jnp.where(lax.broadcasted_iota(jnp.int32, (hp, LANES), 1)
                     == LF_LANE + lax.broadcasted_iota(jnp.int32, (hp, LANES), 0), 1.0, 0.0).astype(BF16)
    lf_new = sum(_dot_nt(pick, part) for part in _split3(misc_n))
    up_n = _strict_upper(LANES)
    g_new = sum(_dot(part, up_n) for part in _split3(lf_new))
    tot_new = jnp.sum(lf_new, axis=-1, keepdims=True)

    def expand(g):
        return jnp.concatenate([jnp.broadcast_to(g[h:h + 1], (tq, g.shape[1])) for h in range(N_HEADS)], axis=0)

    g_new_x = expand(g_new)
    g_q = jnp.sum(jnp.where(lane == trow, g_new_x, 0.0), axis=-1, keepdims=True)

    kn = _pad_rows(kn_ref[...], LANES).astype(BF16)
    vn = _pad_rows(vn_ref[...], LANES).astype(BF16)
    s_n = _dot_nt(qbd, kn) + g_new_x - g_q
    s_n = jnp.where(lane <= trow, s_n, -jnp.inf)
    m, l, acc = _softmax_step(s_n, vn, jnp.full((rows, 1), NEG_BIG, F32), jnp.zeros((rows, 1), F32),
                              jnp.zeros((rows, MIX_W), F32))
    sub = min(2 * LANES, tk)
    upper = _strict_upper(sub)
    carry = tot_new
    for kb in range(past // tk - 1, -1, -1):
        g_parts = []
        for j in range(tk // sub - 1, -1, -1):
            lf_sub = lfp_ref[:, kb * tk + j * sub:kb * tk + (j + 1) * sub]
            g_parts.append(sum(_dot(part, upper) for part in _split3(lf_sub)) + carry)
            carry = carry + jnp.sum(lf_sub, axis=-1, keepdims=True)
        g_blk = g_parts[0] if len(g_parts) == 1 else jnp.concatenate(g_parts[::-1], axis=1)
        kblk = kp_ref[:, kb * tk:(kb + 1) * tk].astype(BF16)
        vblk = vp_ref[:, kb * tk:(kb + 1) * tk].astype(BF16)
        s = _dot(qbd, kblk) + expand(g_blk) - g_q
        m, l, acc = _softmax_step(s, vblk, m, l, acc, v_t=True)
    o_ref[...] = _diag_out(acc / l, diag, tq).astype(o_ref.dtype)


def _dsa_sample_kernel(qidx_ref, miscn_ref, kin_ref, q_ref, kn_ref, vn_ref, kip_ref, kp_ref, vp_ref, o_ref,
                       keyp_scr, keyn_scr, *, topk, pos0):
    tq = q_ref.shape[0]
    past = kp_ref.shape[1]
    lane = lax.broadcasted_iota(jnp.int32, (1, LANES), 1)
    qi = qidx_ref[...].astype(F32)
    qs = jnp.concatenate([qi[:, h * IDX_DIM:(h + 1) * IDX_DIM] for h in range(IDX_HEADS)], axis=0).astype(BF16)
    wq = miscn_ref[...]
    ws = jnp.concatenate([wq[:, WI_LANE + h:WI_LANE + h + 1] for h in range(IDX_HEADS)], axis=0)
    kip = kip_ref[...].astype(BF16)
    kin = _pad_rows(kin_ref[...][:, 0:IDX_DIM], LANES).astype(BF16)

    def head_sum(s):
        s = jnp.maximum(s, 0.0) * ws
        out = s[0:tq]
        for h in range(1, IDX_HEADS):
            out = out + s[h * tq:(h + 1) * tq]
        return out * IDX_HEAD_SCALE

    isc_p = head_sum(_dot(qs, kip))
    isc_n = head_sum(_dot_nt(qs, kin))
    qchunk = jnp.right_shift(pos0 + lax.broadcasted_iota(jnp.int32, (tq, 1), 0), CHUNK_SHIFT)
    ppos = lax.broadcasted_iota(jnp.int32, (1, past), 1)
    npos = pos0 + lane
    ok_p = jnp.right_shift(ppos, CHUNK_SHIFT) <= qchunk
    ok_n = (jnp.right_shift(npos, CHUNK_SHIFT) <= qchunk) & (lane < tq)
    isc_p = jnp.where(ok_p, isc_p, -jnp.inf)
    isc_n = jnp.where(ok_n, isc_n, -jnp.inf)
    keyp_scr[...] = _sort_key(isc_p)
    keyn_scr[...] = jnp.where(lane < tq, _sort_key(isc_n), INT_MIN)
    bias_p, bias_n = _topk_select([lambda: keyp_scr[...], lambda: keyn_scr[...]], [ppos, npos], tq, topk,
                                  int(past + LANES).bit_length())
    bias_p = _head_rows(bias_p, N_HEADS)
    bias_n = _head_rows(bias_n, N_HEADS)

    qbd, diag = _block_diag_q(q_ref[...], tq)
    kn = _pad_rows(kn_ref[...], LANES).astype(BF16)
    vn = _pad_rows(vn_ref[...], LANES).astype(BF16)
    s_p = _dot(qbd, kp_ref[...].astype(BF16)) + bias_p
    s_n = _dot_nt(qbd, kn) + bias_n
    m = jnp.maximum(jnp.max(s_p, axis=-1, keepdims=True), jnp.max(s_n, axis=-1, keepdims=True))
    e_p = jnp.exp(s_p - m)
    e_n = jnp.exp(s_n - m)
    l = jnp.sum(e_p, axis=-1, keepdims=True) + jnp.sum(e_n, axis=-1, keepdims=True)
    o = (_dot_nt(e_p.astype(BF16), vp_ref[...].astype(BF16)) + _dot(e_n.astype(BF16), vn)) / l
    o_ref[...] = _diag_out(o, diag, tq).astype(o_ref.dtype)


def _sample_attention(layer, q_sb, k_sb, v_sb, q_fx, k_fx, v_fx, q_ds, k_ds, v_ds, qidx, misc,
                      c_sb_k, c_sb_v, c_fx_k, c_fx_v, lf_t, c_ds_k, c_ds_v, c_kidx):
    b, tq, _ = q_sb.shape
    past = c_sb_k.shape[3]
    tk = min(SAMPLE_KEYS, past)
    new = lambda w: pl.BlockSpec((None, tq, w), lambda bb: (bb, 0, 0))
    cache = lambda w: pl.BlockSpec((None, None, w, past), lambda bb: (layer, bb, 0, 0))
    out = jax.ShapeDtypeStruct((b, tq, MIX_W), BF16)
    cp = _cparams(("arbitrary",))
    o_sb = pl.pallas_call(
        functools.partial(_sb_sample_kernel, tk=tk), grid=(b,),
        in_specs=[new(MIX_W), new(MIX_W), new(MIX_W), cache(MIX_W), cache(MIX_W)],
        out_specs=new(MIX_W), out_shape=out, compiler_params=cp, name="sb_sample",
    )(q_sb, k_sb, v_sb, c_sb_k, c_sb_v)
    hp = lf_t.shape[2]
    o_fx = pl.pallas_call(
        functools.partial(_fox_sample_kernel, tk=tk), grid=(b,),
        in_specs=[new(MIX_W), new(MIX_W), new(MIX_W), new(LANES), cache(MIX_W), cache(MIX_W),
                  cache(hp)],
        out_specs=new(MIX_W), out_shape=out, compiler_params=cp, name="fox_sample",
    )(q_fx, k_fx, v_fx, misc, c_fx_k, c_fx_v, lf_t)
    topk = min(TOPK_MAX, (past + tq) // 4)
    o_ds = pl.pallas_call(
        functools.partial(_dsa_sample_kernel, topk=topk, pos0=past), grid=(b,),
        in_specs=[new(IDX_HEADS * IDX_DIM), new(LANES), new(LANES), new(MIX_W), new(MIX_W), new(MIX_W),
                  cache(IDX_DIM), cache(MIX_W), cache(MIX_W)],
        out_specs=new(MIX_W), out_shape=out,
        scratch_shapes=[pltpu.VMEM((tq, past), jnp.int32), pltpu.VMEM((tq, LANES), jnp.int32)],
        compiler_params=cp, name="dsa_sample",
    )(qidx, misc, misc, q_ds, k_ds, v_ds, c_kidx, c_ds_k, c_ds_v)
    return o_sb, o_fx, o_ds


def _rope_tables(pos):
    half = HEAD_DIM // 2
    inv_freq = ROPE_THETA ** (-jnp.arange(half, dtype=F32) / half)
    ang = pos.astype(F32)[:, None] * inv_freq[None, :]
    cos = jnp.cos(ang)
    sin = jnp.sin(ang)
    cos_t = jnp.concatenate([cos, cos, cos, cos], axis=1)
    sin_t = jnp.concatenate([-sin, sin, -sin, sin], axis=1)
    return cos_t, sin_t


def _prep_weights(w_in, b_forget, ffn_w_in, ffn_w_out, w_branch, w_out):
    w3 = 3 * MIX_W
    o = np.cumsum([0, w3, w3, N_HEADS, w3, IDX_HEADS * IDX_DIM, IDX_DIM, IDX_HEADS]).tolist()
    sb, fx, fl, ds, qi, ki, wi = (w_in[:, :, o[k]:o[k + 1]] for k in range(7))
    pad = jnp.zeros(w_in.shape[:2] + (LANES - IDX_DIM - N_HEADS - IDX_HEADS,), w_in.dtype)
    w_main = jnp.concatenate([sb, fx, ds, qi, ki, fl, wi, pad], axis=2).astype(BF16)
    w_gate = w_in[:, :, o[7]:].astype(BF16)
    depth = w_in.shape[0]
    bf_row = jnp.zeros((depth, 1, LANES), F32).at[:, 0, LF_LANE:LF_LANE + N_HEADS].set(b_forget)
    return (w_main, w_gate, bf_row, ffn_w_in.astype(BF16), ffn_w_out.astype(BF16),
            w_branch.astype(BF16), w_out.astype(BF16))


def _trunk(x3, mod_l, gdiv, tm, pos, caches, norm_g, weights, *, tq_attn):
    w_main, w_gate, bf_row, w_up, w_dn, w_branch, w_out = weights
    b, t, d = x3.shape
    n = b * t
    depth = w_main.shape[0]
    x = x3.reshape(n, d)
    cos_t, sin_t = _rope_tables(pos)
    if caches is not None:
        cos_t = jnp.tile(cos_t, (tm // t, 1))
        sin_t = jnp.tile(sin_t, (tm // t, 1))
        lf_t = jnp.swapaxes(caches[4], 2, 3)
        lf_t = jnp.pad(lf_t, ((0, 0), (0, 0), (0, 16 - lf_t.shape[2]), (0, 0)))
    states = []
    for l in range(depth):
        g = lambda k: norm_g[l, k][None, :]
        mod = mod_l[l]
        x = _ffn(x, mod, 0, g(0), g(1), w_up[l, 0], w_dn[l, 0], tm=tm, gdiv=gdiv)
        (q_sb, k_sb, v_sb, q_fx, k_fx, v_fx, q_ds, k_ds, v_ds, qidx, kidx2, misc, *state_t) = _proj(
            x, mod, g(2), w_main[l], bf_row[l], cos_t, sin_t, tm=tm, gdiv=gdiv,
            batch_t=b if caches is None else None)
        r3 = lambda a: a.reshape(b, t, a.shape[-1])
        if caches is None:
            qa, ka = _fox_prep(r3(q_fx), r3(k_fx), r3(misc), tm=min(256, t))
            o_sb = _sb_prompt(r3(q_sb), r3(k_sb), r3(v_sb), tq=tq_attn, tk=min(512, t))
            o_fx = _fox_prompt(qa, ka, r3(v_fx), tq=tq_attn, tk=min(512, t))
            o_ds = _dsa_prompt(r3(qidx), r3(kidx2), r3(misc), r3(q_ds), r3(k_ds), r3(v_ds), tq=tq_attn)
        else:
            o_sb, o_fx, o_ds = _sample_attention(
                l, r3(q_sb), r3(k_sb), r3(v_sb), r3(q_fx), r3(k_fx), r3(v_fx), r3(q_ds), r3(k_ds), r3(v_ds),
                r3(qidx), r3(misc), caches[0], caches[1], caches[2], caches[3], lf_t,
                caches[5], caches[6], caches[7])
        o2 = lambda a: a.reshape(n, MIX_W)
        x = _merge(x, mod, g(2), g(3), o2(o_sb), o2(o_fx), o2(o_ds), w_gate[l], w_branch[l], w_out[l],
                   tm=tm, gdiv=gdiv)
        x = _ffn(x, mod, 6, g(4), g(5), w_up[l, 1], w_dn[l, 1], tm=tm, gdiv=gdiv)
        if state_t:
            hd = lambda a: jnp.transpose(a.reshape(b, N_HEADS, HEAD_DIM, t), (0, 3, 1, 2))
            kv = [hd(a) for a in state_t]
        else:
            hd = lambda a: a.reshape(b, t, N_HEADS, HEAD_DIM)
            kv = [hd(a) for a in (k_sb, v_sb, k_fx, v_fx, k_ds, v_ds)]
        states.append((kv[0], kv[1], kv[2], kv[3],
                       misc[:, LF_LANE:LF_LANE + N_HEADS].reshape(b, t, N_HEADS),
                       kv[4], kv[5], misc[:, 0:IDX_DIM].reshape(b, t, IDX_DIM)))
    return x.reshape(b, t, d), tuple(jnp.stack(s, axis=0) for s in zip(*states))


def kernel(x_prompt, x_sample, cache_sb_k, cache_sb_v, cache_fox_k, cache_fox_v, cache_fox_logf, cache_dsa_k, cache_dsa_v, cache_dsa_kidx, c_prompt, c_sample, norm_g, w_ada, b_ada, ffn_w_in, ffn_w_out, w_in, b_forget, w_branch, w_out):
    bp, tp, d = x_prompt.shape
    bs, ts, _ = x_sample.shape
    depth = w_in.shape[0]
    past = cache_sb_k.shape[2]
    mix = lambda a: jnp.transpose(a, (0, 1, 3, 4, 2)).reshape(a.shape[:2] + (MIX_W, a.shape[2]))
    caches = (mix(cache_sb_k), mix(cache_sb_v), mix(cache_fox_k), mix(cache_fox_v), cache_fox_logf,
              mix(cache_dsa_k), mix(cache_dsa_v), jnp.swapaxes(cache_dsa_kidx, 2, 3))
    weights = _prep_weights(w_in, b_forget, ffn_w_in, ffn_w_out, w_branch, w_out)

    rows = bp + bs
    rows_pad = -(-rows // 8) * 8
    c_all = jnp.concatenate([c_prompt, c_sample, jnp.zeros((rows_pad - rows, d), F32)], axis=0)
    mod = _ada(c_all, w_ada, b_ada)
    mod_p = [mod[l, :bp].reshape(bp, N_MOD, 1, d) for l in range(depth)]
    ns = bs * ts
    mod_s = [jnp.repeat(mod[l, bp:rows].reshape(bs, N_MOD, d), ts, axis=0)
             .reshape(1, ns, N_MOD, d).transpose(0, 2, 1, 3) for l in range(depth)]

    tm_p = min(512, tp)
    pos_p = jnp.arange(tp, dtype=jnp.int32)
    pos_s = past + jnp.arange(ts, dtype=jnp.int32)
    y_p, st_p = _trunk(x_prompt, mod_p, tp // tm_p, tm_p, pos_p, None, norm_g, weights, tq_attn=min(256, tp))
    y_s, st_s = _trunk(x_sample, mod_s, 1, ns, pos_s, caches, norm_g, weights, tq_attn=ts)
    return (y_p, y_s) + st_p + st_s
```

```python
import functools

import numpy as np
import jax
import jax.numpy as jnp
from jax import lax
from jax.experimental import pallas as pl
from jax.experimental.pallas import tpu as pltpu

F32 = jnp.float32
BF16 = jnp.bfloat16

HEAD_DIM = 64
N_HEADS = 6
MIX_W = N_HEADS * HEAD_DIM
N_PAIR = MIX_W // 128
IDX_HEADS = 4
IDX_DIM = 64
CHUNK = 64
CHUNK_SHIFT = 6
TOPK_MAX = 256
N_BRANCH = 3
N_MOD = 9
ROPE_THETA = 10000.0
EPS = 1e-6
FFN_RES = 0.5
QK_SCALE = HEAD_DIM ** -0.5
IDX_SCALE = IDX_DIM ** -0.5
IDX_HEAD_SCALE = IDX_HEADS ** -0.5
LANES = 128
LF_LANE = IDX_DIM
WI_LANE = IDX_DIM + N_HEADS
PROJ_W = 3 * 3 * MIX_W + IDX_HEADS * IDX_DIM + LANES
VMEM_LIMIT = 56 * 1024 * 1024
SELECT_BATCH = 8
SEARCH_UNROLL = 4
SAMPLE_KEYS = 2048
INT_MIN = np.int32(-2 ** 31)
KEY_NEG_INF = np.int32(-2139095041)
NEG_BIG = -1e30


def _cparams(sem):
    return pltpu.CompilerParams(dimension_semantics=sem, vmem_limit_bytes=VMEM_LIMIT)


def _dot(a, b):
    return jnp.dot(a, b, preferred_element_type=F32)


def _dot_nt(a, b):
    return lax.dot_general(a, b, (((1,), (1,)), ((), ())), preferred_element_type=F32)


def _split2(x):
    hi = x.astype(BF16)
    lo = (x - hi.astype(F32)).astype(BF16)
    return hi, lo


def _split3(x):
    p1 = x.astype(BF16)
    r1 = x - p1.astype(F32)
    p2 = r1.astype(BF16)
    p3 = (r1 - p2.astype(F32)).astype(BF16)
    return p1, p2, p3


def _log_sigmoid(x):
    return jnp.minimum(x, 0.0) - jnp.log1p(jnp.exp(-jnp.abs(x)))


def _rms(x, g):
    return x * lax.rsqrt(jnp.mean(x * x, axis=-1, keepdims=True) + EPS) * g


def _norm_mod(x, g, scale, shift):
    return _rms(x, g) * (1.0 + scale) + shift


def _strict_upper(n):
    r = lax.broadcasted_iota(jnp.int32, (n, n), 0)
    c = lax.broadcasted_iota(jnp.int32, (n, n), 1)
    return jnp.where(r > c, 1.0, 0.0).astype(BF16)


def _strict_lower(n):
    r = lax.broadcasted_iota(jnp.int32, (n, n), 0)
    c = lax.broadcasted_iota(jnp.int32, (n, n), 1)
    return jnp.where(c > r, 1.0, 0.0).astype(BF16)


def _ada_kernel(c_ref, w_ref, b_ref, o_ref):
    c = c_ref[...]
    s = (c * jax.nn.sigmoid(c)).astype(BF16)
    o_ref[0] = _dot(s, w_ref[0].astype(BF16)) + b_ref[0]


def _ada(c_all, w_ada, b_ada):
    depth, d, n = w_ada.shape
    rows = c_all.shape[0]
    tn = n // 8
    return pl.pallas_call(
        _ada_kernel,
        grid=(depth, n // tn),
        in_specs=[
            pl.BlockSpec((rows, d), lambda l, j: (0, 0)),
            pl.BlockSpec((1, d, tn), lambda l, j: (l, 0, j)),
            pl.BlockSpec((1, 1, tn), lambda l, j: (l, 0, j)),
        ],
        out_specs=pl.BlockSpec((1, rows, tn), lambda l, j: (l, 0, j)),
        out_shape=jax.ShapeDtypeStruct((depth, rows, n), F32),
        compiler_params=_cparams(("arbitrary", "arbitrary")),
        name="ada",
    )(c_all, w_ada, b_ada.reshape(depth, 1, n))


def _ffn_kernel(x_ref, shift_ref, scale_ref, gate_ref, gpre_ref, gpost_ref, wg_ref, wu_ref, wd_ref,
                o_ref, acc_scr, *, nj):
    j = pl.program_id(1)
    for jj in range(nj):
        @pl.when(j == jj)
        def _():
            x = x_ref[...]
            h = _norm_mod(x, gpre_ref[...], scale_ref[0, 0], shift_ref[0, 0]).astype(BF16)
            g = _dot(h, wg_ref[...])
            u = _dot(h, wu_ref[...])
            a = (g * jax.nn.sigmoid(g) * u).astype(BF16)
            d = _dot(a, wd_ref[...])
            acc = d if jj == 0 else acc_scr[...] + d
            if jj < nj - 1:
                acc_scr[...] = acc
            else:
                o_ref[...] = x + FFN_RES * gate_ref[0, 0] * _rms(acc, gpost_ref[...])


def _mod_spec(r, d, gdiv, comp):
    return pl.BlockSpec((1, 1, r, d), lambda i, *_: (i // gdiv, comp, 0, 0))


def _ffn(x, mod, comp0, g_pre, g_post, w_up, w_dn, *, tm, gdiv):
    n, d = x.shape
    ff = w_dn.shape[0]
    nj = 2 if (ff // 2) % LANES == 0 else 1
    tf = ff // nj
    r = mod.shape[2]
    return pl.pallas_call(
        functools.partial(_ffn_kernel, nj=nj),
        grid=(n // tm, nj),
        in_specs=[
            pl.BlockSpec((tm, d), lambda i, j: (i, 0)),
            _mod_spec(r, d, gdiv, comp0),
            _mod_spec(r, d, gdiv, comp0 + 1),
            _mod_spec(r, d, gdiv, comp0 + 2),
            pl.BlockSpec((1, d), lambda i, j: (0, 0)),
            pl.BlockSpec((1, d), lambda i, j: (0, 0)),
            pl.BlockSpec((d, tf), lambda i, j: (0, j)),
            pl.BlockSpec((d, tf), lambda i, j: (0, nj + j)),
            pl.BlockSpec((tf, d), lambda i, j: (j, 0)),
        ],
        out_specs=pl.BlockSpec((tm, d), lambda i, j: (i, 0)),
        out_shape=jax.ShapeDtypeStruct((n, d), F32),
        scratch_shapes=[pltpu.VMEM((tm, d), F32)],
        compiler_params=_cparams(("arbitrary", "arbitrary")),
        name="ffn",
    )(x, mod, mod, mod, g_pre, g_post, w_up, w_up, w_dn)


def _rope(x, cos, sin, first_half):
    outs = []
    for c in range(x.shape[1] // LANES):
        xs = x[:, c * LANES:(c + 1) * LANES]
        below = pltpu.roll(xs, 32, 1)
        above = pltpu.roll(xs, LANES - 32, 1)
        outs.append(xs * cos + jnp.where(first_half, above, below) * sin)
    return outs[0] if len(outs) == 1 else jnp.concatenate(outs, axis=1)


def _proj_kernel(x_ref, shift_ref, scale_ref, gpre_ref, w_ref, bf_ref, cos_ref, sin_ref,
                 qsb, ksb, vsb, qfx, kfx, vfx, qds, kds, vds, qidx, kidx2, misc, *state_t):
    h = _norm_mod(x_ref[...], gpre_ref[...], scale_ref[0, 0], shift_ref[0, 0]).astype(BF16)
    cos = cos_ref[...]
    sin = sin_ref[...]
    lane = lax.broadcasted_iota(jnp.int32, (1, LANES), 1)
    first_half = (lane & (HEAD_DIM - 1)) < (HEAD_DIM // 2)
    w3 = 3 * MIX_W

    def put_kv(k, v, k_ref, v_ref, branch):
        k_ref[...] = k.astype(k_ref.dtype)
        v_ref[...] = v.astype(v_ref.dtype)
        if state_t:
            state_t[2 * branch][...] = k.T
            state_t[2 * branch + 1][...] = v.T

    y = _dot(h, w_ref[:, 0:w3])
    qsb[...] = (y[:, 0:MIX_W] * QK_SCALE).astype(BF16)
    put_kv(y[:, MIX_W:2 * MIX_W], y[:, 2 * MIX_W:w3], ksb, vsb, 0)

    y = _dot(h, w_ref[:, w3:2 * w3])
    qfx[...] = (y[:, 0:MIX_W] * QK_SCALE).astype(BF16)
    put_kv(y[:, MIX_W:2 * MIX_W], y[:, 2 * MIX_W:w3], kfx, vfx, 1)

    y = _dot(h, w_ref[:, 2 * w3:3 * w3])
    qds[...] = (_rope(y[:, 0:MIX_W], cos, sin, first_half) * QK_SCALE).astype(BF16)
    put_kv(_rope(y[:, MIX_W:2 * MIX_W], cos, sin, first_half), y[:, 2 * MIX_W:w3], kds, vds, 2)

    y = _dot(h, w_ref[:, 3 * w3:PROJ_W])
    nq = IDX_HEADS * IDX_DIM
    qidx[...] = (_rope(y[:, 0:nq], cos, sin, first_half) * IDX_SCALE).astype(BF16)
    m = y[:, nq:nq + LANES]
    m_rot = _rope(m, cos, sin, first_half)
    lf = _log_sigmoid(m + bf_ref[...])
    misc[...] = jnp.where(lane < LF_LANE, m_rot,
                          jnp.where(lane < WI_LANE, lf,
                                    jnp.where(lane < WI_LANE + IDX_HEADS, m, 0.0)))
    kidx2[...] = jnp.where(lane < IDX_DIM, m_rot, pltpu.roll(m_rot, IDX_DIM, 1)).astype(BF16)


def _proj(x, mod, g_pre, w_main, bf_row, cos_t, sin_t, *, tm, gdiv, batch_t=None):
    n, d = x.shape
    r = mod.shape[2]
    tab_tiles = cos_t.shape[0] // tm
    row = lambda w: pl.BlockSpec((tm, w), lambda i: (i, 0))
    tab = pl.BlockSpec((tm, LANES), lambda i: (i % tab_tiles, 0))
    shp = lambda w, dt: jax.ShapeDtypeStruct((n, w), dt)
    nq = IDX_HEADS * IDX_DIM
    kv_dt = F32 if batch_t is None else BF16
    out_specs = [row(MIX_W)] * 9 + [row(nq), row(LANES), row(LANES)]
    out_shape = [shp(MIX_W, BF16), shp(MIX_W, kv_dt), shp(MIX_W, kv_dt)] * 3 + [
        shp(nq, BF16), shp(LANES, BF16), shp(LANES, F32)]
    if batch_t is not None:
        t = n // batch_t
        out_specs += [pl.BlockSpec((None, MIX_W, tm), lambda i: (i // gdiv, 0, i % gdiv))] * 6
        out_shape += [jax.ShapeDtypeStruct((batch_t, MIX_W, t), F32)] * 6
    return pl.pallas_call(
        _proj_kernel,
        grid=(n // tm,),
        in_specs=[
            row(d),
            _mod_spec(r, d, gdiv, 3),
            _mod_spec(r, d, gdiv, 4),
            pl.BlockSpec((1, d), lambda i: (0, 0)),
            pl.BlockSpec((d, PROJ_W), lambda i: (0, 0)),
            pl.BlockSpec((1, LANES), lambda i: (0, 0)),
            tab, tab,
        ],
        out_specs=out_specs,
        out_shape=out_shape,
        compiler_params=_cparams(("arbitrary",)),
        name="proj",
    )(x, mod, mod, g_pre, w_main, bf_row, cos_t, sin_t)


def _merge_kernel(x_ref, shift_ref, scale_ref, gate_ref, gpre_ref, gpost_ref,
                  osb_ref, ofx_ref, ods_ref, wg_ref, wb_ref, wo_ref, o_ref):
    x = x_ref[...]
    d = x.shape[1]
    h = _norm_mod(x, gpre_ref[...], scale_ref[0, 0], shift_ref[0, 0]).astype(BF16)
    merged = None
    for nb, o_br in enumerate((osb_ref, ofx_ref, ods_ref)):
        gl = _dot(h, wg_ref[:, nb * d:(nb + 1) * d])
        y = _dot(o_br[...], wb_ref[nb])
        t = jax.nn.sigmoid(gl) * y
        merged = t if merged is None else merged + t
    out = _dot(merged.astype(BF16), wo_ref[...])
    o_ref[...] = x + gate_ref[0, 0] * _rms(out, gpost_ref[...])


def _merge(x, mod, g_pre, g_post, o_sb, o_fx, o_ds, w_gate, w_branch, w_out, *, tm, gdiv):
    n, d = x.shape
    r = mod.shape[2]
    row = lambda w: pl.BlockSpec((tm, w), lambda i: (i, 0))
    return pl.pallas_call(
        _merge_kernel,
        grid=(n // tm,),
        in_specs=[
            row(d),
            _mod_spec(r, d, gdiv, 3),
            _mod_spec(r, d, gdiv, 4),
            _mod_spec(r, d, gdiv, 5),
            pl.BlockSpec((1, d), lambda i: (0, 0)),
            pl.BlockSpec((1, d), lambda i: (0, 0)),
            row(MIX_W), row(MIX_W), row(MIX_W),
            pl.BlockSpec((d, N_BRANCH * d), lambda i: (0, 0)),
            pl.BlockSpec((N_BRANCH, MIX_W, d), lambda i: (0, 0, 0)),
            pl.BlockSpec((d, d), lambda i: (0, 0)),
        ],
        out_specs=row(d),
        out_shape=jax.ShapeDtypeStruct((n, d), F32),
        compiler_params=_cparams(("arbitrary",)),
        name="merge",
    )(x, mod, mod, mod, g_pre, g_post, o_sb, o_fx, o_ds, w_gate, w_branch, w_out)


def _aug_lanes(lane, base, ones_first, parts):
    one_lo, val_lo = (base, base + 3) if ones_first else (base + 3, base)
    out = jnp.where((lane >= one_lo) & (lane < one_lo + 3), 1.0, 0.0)
    for k, p in enumerate(parts):
        out = jnp.where(lane == val_lo + k, p, out)
    return out


def _fox_prep_kernel(q_ref, k_ref, misc_ref, qa_ref, ka_ref, carry):
    j = pl.program_id(1)

    @pl.when(j == 0)
    def _():
        carry[...] = jnp.zeros_like(carry)

    lf = misc_ref[...]
    tm = lf.shape[0]
    low = _strict_lower(tm)
    p1, p2, p3 = _split3(lf)
    g = _dot(low, p1) + _dot(low, p2) + _dot(low, p3) + carry[...]
    carry[...] += jnp.sum(lf, axis=0, keepdims=True)

    lane = lax.broadcasted_iota(jnp.int32, (1, LANES), 1)
    lo_half = lane < HEAD_DIM
    for p in range(N_PAIR):
        qp = q_ref[:, p * LANES:(p + 1) * LANES].astype(F32)
        kp = k_ref[:, p * LANES:(p + 1) * LANES]
        for a in range(2):
            hd = 2 * p + a
            gcol = g[:, LF_LANE + hd:LF_LANE + hd + 1]
            g1, g2, g3 = (t.astype(F32) for t in _split3(gcol))
            own = lo_half if a == 0 else jnp.logical_not(lo_half)
            base = HEAD_DIM if a == 0 else 0
            q_aug = jnp.where(own, qp, _aug_lanes(lane, base, True, (-g1, -g2, -g3)))
            k_aug = jnp.where(own, kp, _aug_lanes(lane, base, False, (g1, g2, g3)))
            qa_ref[:, hd * LANES:(hd + 1) * LANES] = q_aug.astype(BF16)
            ka_ref[:, hd * LANES:(hd + 1) * LANES] = k_aug.astype(BF16)


def _fox_prep(q, k, misc, *, tm):
    b, t, _ = q.shape
    nt = t // tm
    rev = lambda w: pl.BlockSpec((None, tm, w), lambda bb, j: (bb, nt - 1 - j, 0))
    return pl.pallas_call(
        _fox_prep_kernel,
        grid=(b, nt),
        in_specs=[rev(MIX_W), rev(MIX_W), rev(LANES)],
        out_specs=[rev(N_HEADS * LANES), rev(N_HEADS * LANES)],
        out_shape=[jax.ShapeDtypeStruct((b, t, N_HEADS * LANES), BF16)] * 2,
        scratch_shapes=[pltpu.VMEM((1, LANES), F32)],
        compiler_params=_cparams(("arbitrary", "arbitrary")),
        name="fox_prep",
    )(q, k, misc)


def _sb_block(qh, kblk, vblk, mask, upper, acc, c, kv_t=False):
    z = _dot(qh, kblk) if kv_t else _dot_nt(qh, kblk)
    neg_abs = lax.bitcast_convert_type(lax.bitcast_convert_type(z, jnp.int32) | INT_MIN, F32)
    ls = jnp.minimum(z, 0.0) - jnp.log(1.0 + jnp.exp(neg_abs))
    lk = ls - z
    if mask is not None:
        lk = jnp.where(mask, lk, 0.0)
    hi, lo = _split2(lk)
    sub = upper.shape[0]
    tot = []
    for j in range(z.shape[1] // sub - 1, -1, -1):
        sl = slice(j * sub, (j + 1) * sub)
        tot.append(_dot(hi[:, sl], upper) + _dot(lo[:, sl], upper) + c)
        c = c + jnp.sum(lk[:, sl], axis=-1, keepdims=True)
    tot = tot[0] if len(tot) == 1 else jnp.concatenate(tot[::-1], axis=1)
    a = jnp.exp(ls + tot)
    if mask is not None:
        a = jnp.where(mask, a, 0.0)
    a = a.astype(BF16)
    acc = acc + (_dot_nt(a, vblk) if kv_t else _dot(a, vblk))
    return acc, c


def _sb_kernel(q_ref, k_ref, v_ref, o_ref, *, tq, tk):
    i = pl.program_id(1)
    lane = lax.broadcasted_iota(jnp.int32, (1, LANES), 1)
    lo_half = lane < HEAD_DIM
    q2 = []
    for p in range(N_PAIR):
        q = q_ref[:, p * LANES:(p + 1) * LANES]
        zero = jnp.zeros_like(q)
        q2.append(jnp.concatenate([jnp.where(lo_half, q, zero), jnp.where(lo_half, zero, q)], axis=0))
    qpos = i * tq + (lax.broadcasted_iota(jnp.int32, (2 * tq, 1), 0) & (tq - 1))
    upper = _strict_upper(LANES)
    nkb = jnp.right_shift((i + 1) * tq + (tk - 1), tk.bit_length() - 1)

    def body(it, carry):
        start = pl.multiple_of((nkb - 1 - it) * tk, tk)
        mask = (start + lax.broadcasted_iota(jnp.int32, (1, tk), 1)) < qpos
        out = []
        for p in range(N_PAIR):
            kblk = k_ref[pl.ds(start, tk), p * LANES:(p + 1) * LANES].astype(BF16)
            vblk = v_ref[pl.ds(start, tk), p * LANES:(p + 1) * LANES].astype(BF16)
            out.extend(_sb_block(q2[p], kblk, vblk, mask, upper, carry[2 * p], carry[2 * p + 1]))
        return tuple(out)

    init = (jnp.zeros((2 * tq, LANES), F32), jnp.zeros((2 * tq, 1), F32)) * N_PAIR
    fin = lax.fori_loop(0, nkb, body, init)
    for p in range(N_PAIR):
        acc = fin[2 * p]
        o_ref[:, p * LANES:(p + 1) * LANES] = jnp.where(lo_half, acc[0:tq], acc[tq:2 * tq]).astype(o_ref.dtype)


def _sb_prompt(q, k, v, *, tq, tk):
    b, t, _ = q.shape
    full = pl.BlockSpec((None, t, MIX_W), lambda bb, i: (bb, 0, 0))
    rows = pl.BlockSpec((None, tq, MIX_W), lambda bb, i: (bb, i, 0))
    return pl.pallas_call(
        functools.partial(_sb_kernel, tq=tq, tk=tk),
        grid=(b, t // tq),
        in_specs=[rows, full, full],
        out_specs=rows,
        out_shape=jax.ShapeDtypeStruct((b, t, MIX_W), BF16),
        compiler_params=_cparams(("arbitrary", "arbitrary")),
        name="sb_prompt",
    )(q, k, v)


def _softmax_step(s, vblk, m, l, acc, v_t=False):
    m_new = jnp.maximum(m, jnp.max(s, axis=-1, keepdims=True))
    alpha = jnp.exp(m - m_new)
    p = jnp.exp(s - m_new)
    l = alpha * l + jnp.sum(p, axis=-1, keepdims=True)
    p = p.astype(BF16)
    acc = alpha * acc + (_dot_nt(p, vblk) if v_t else _dot(p, vblk))
    return m_new, l, acc


def _fox_kernel(qa_ref, ka_ref, v_ref, o_ref, *, tq, tk):
    i = pl.program_id(1)
    qpos = i * tq + lax.broadcasted_iota(jnp.int32, (tq, 1), 0)
    nkb = jnp.right_shift((i + 1) * tq + (tk - 1), tk.bit_length() - 1)

    def body(kb, carry):
        start = pl.multiple_of(kb * tk, tk)
        kpos = start + lax.broadcasted_iota(jnp.int32, (1, tk), 1)
        mask = kpos <= qpos
        out = []
        for p in range(N_PAIR):
            vblk = v_ref[pl.ds(start, tk), p * LANES:(p + 1) * LANES].astype(BF16)
            for a in range(2):
                hd = 2 * p + a
                m, l, acc = carry[3 * hd:3 * hd + 3]
                s = _dot_nt(qa_ref[:, hd * LANES:(hd + 1) * LANES],
                            ka_ref[pl.ds(start, tk), hd * LANES:(hd + 1) * LANES])
                s = jnp.where(mask, s, -jnp.inf)
                out.extend(_softmax_step(s, vblk, m, l, acc))
        return tuple(out)

    init = (jnp.full((tq, 1), NEG_BIG, F32), jnp.zeros((tq, 1), F32), jnp.zeros((tq, LANES), F32)) * N_HEADS
    fin = lax.fori_loop(0, nkb, body, init)
    lane = lax.broadcasted_iota(jnp.int32, (1, LANES), 1)
    for p in range(N_PAIR):
        o0 = fin[6 * p + 2] / fin[6 * p + 1]
        o1 = fin[6 * p + 5] / fin[6 * p + 4]
        o_ref[:, p * LANES:(p + 1) * LANES] = jnp.where(lane < HEAD_DIM, o0, o1).astype(o_ref.dtype)


def _fox_prompt(qa, ka, v, *, tq, tk):
    b, t, _ = v.shape
    return pl.pallas_call(
        functools.partial(_fox_kernel, tq=tq, tk=tk),
        grid=(b, t // tq),
        in_specs=[
            pl.BlockSpec((None, tq, N_HEADS * LANES), lambda bb, i: (bb, i, 0)),
            pl.BlockSpec((None, t, N_HEADS * LANES), lambda bb, i: (bb, 0, 0)),
            pl.BlockSpec((None, t, MIX_W), lambda bb, i: (bb, 0, 0)),
        ],
        out_specs=pl.BlockSpec((None, tq, MIX_W), lambda bb, i: (bb, i, 0)),
        out_shape=jax.ShapeDtypeStruct((b, t, MIX_W), BF16),
        compiler_params=_cparams(("arbitrary", "arbitrary")),
        name="fox_prompt",
    )(qa, ka, v)


def _sort_key(x):
    bits = lax.bitcast_convert_type(x, jnp.int32)
    key = jnp.where(bits < 0, bits ^ jnp.int32(0x7FFFFFFF), bits)
    return jnp.where(x == 0.0, 0, key)


def _count(pred_parts):
    tot = None
    for preds in pred_parts:
        ind = jnp.where(preds[-1], 1.0, 0.0)
        for p in preds[-2::-1]:
            ind = jnp.where(p, ind, 0.0)
        c = jnp.sum(ind, axis=-1, keepdims=True)
        tot = c if tot is None else tot + c
    return tot


def _kth_threshold(key_gets, rows, kf):
    def step(it, thr):
        cand = thr + jnp.left_shift(jnp.int32(1), 31 - it)
        cnt = _count([(kg() >= cand,) for kg in key_gets])
        return jnp.where(cnt >= kf, cand, thr)

    return lax.fori_loop(0, 32, step, jnp.full((rows, 1), INT_MIN, jnp.int32), unroll=SEARCH_UNROLL)


def _topk_select(key_gets, pos_list, rows, topk, idx_bits):
    kf = float(topk)
    thr = _kth_threshold(key_gets, rows, kf)
    need = kf - _count([(kg() > thr,) for kg in key_gets])
    n_eq = _count([(kg() == thr,) for kg in key_gets])

    def ibody(it, bound):
        cand = bound + jnp.left_shift(jnp.int32(1), idx_bits - 1 - it)
        cnt = _count([(kg() == thr, pos < cand) for kg, pos in zip(key_gets, pos_list)])
        return jnp.where(cnt <= need, cand, bound)

    some_partial = jnp.max(jnp.where(n_eq > need, 1.0, 0.0)) > 0.0
    bound = lax.cond(
        some_partial,
        lambda: lax.fori_loop(0, idx_bits, ibody, jnp.zeros((rows, 1), jnp.int32)),
        lambda: jnp.full((rows, 1), 1 << idx_bits, jnp.int32))
    tie = jnp.where(thr > KEY_NEG_INF, 0.0, -jnp.inf)
    out = []
    for kg, pos in zip(key_gets, pos_list):
        key = kg()
        at_thr = jnp.where(pos < bound, tie, -jnp.inf)
        out.append(jnp.where(key > thr, 0.0, jnp.where(key == thr, at_thr, -jnp.inf)))
    return out


def _dsa_body(i, n_keys, qidx_ref, kidx2_ref, miscq_ref, q_ref, k_ref, v_ref, o_ref, key_scr, bias_scr, *, tq, topk):
    lane = lax.broadcasted_iota(jnp.int32, (1, LANES), 1)
    lo_half = lane < HEAD_DIM
    kidx2 = kidx2_ref[0:n_keys, :]
    wq = miscq_ref[...]
    isc = jnp.zeros((tq, n_keys), F32)
    for h in range(IDX_HEADS):
        qp = qidx_ref[:, (h // 2) * LANES:(h // 2 + 1) * LANES]
        own = lo_half if h % 2 == 0 else jnp.logical_not(lo_half)
        qh = jnp.where(own, qp, jnp.zeros_like(qp))
        s = _dot_nt(qh, kidx2)
        isc = isc + jnp.maximum(s, 0.0) * wq[:, WI_LANE + h:WI_LANE + h + 1]
    isc = isc * IDX_HEAD_SCALE
    qpos = i * tq + lax.broadcasted_iota(jnp.int32, (tq, 1), 0)
    kpos = lax.broadcasted_iota(jnp.int32, (1, n_keys), 1)
    allowed = jnp.right_shift(kpos, CHUNK_SHIFT) <= jnp.right_shift(qpos, CHUNK_SHIFT)
    isc = jnp.where(allowed, isc, -jnp.inf)
    key_scr[:, 0:n_keys] = _sort_key(isc)
    (bias,) = _topk_select([lambda: key_scr[:, 0:n_keys]], [kpos], tq, topk, int(n_keys).bit_length())
    bias_scr[:, 0:n_keys] = bias

    for p in range(N_PAIR):
        qp = q_ref[:, p * LANES:(p + 1) * LANES]
        kp = k_ref[0:n_keys, p * LANES:(p + 1) * LANES].astype(BF16)
        vp = v_ref[0:n_keys, p * LANES:(p + 1) * LANES].astype(BF16)
        outs = []
        for a in range(2):
            own = lo_half if a == 0 else jnp.logical_not(lo_half)
            qh = jnp.where(own, qp, jnp.zeros_like(qp))
            s = _dot_nt(qh, kp) + bias_scr[:, 0:n_keys]
            m = jnp.max(s, axis=-1, keepdims=True)
            e = jnp.exp(s - m)
            l = jnp.sum(e, axis=-1, keepdims=True)
            outs.append(_dot(e.astype(BF16), vp) / l)
        o_ref[:, p * LANES:(p + 1) * LANES] = jnp.where(lo_half, outs[0], outs[1]).astype(o_ref.dtype)


def _dsa_kernel(*refs, tq, topk, first_block):
    n_keys = refs[4].shape[0]
    _dsa_body(first_block + pl.program_id(1), n_keys, *refs, tq=tq, topk=topk)


def _dsa_prompt(qidx, kidx2, misc, q, k, v, *, tq):
    b, t, _ = q.shape
    topk = min(TOPK_MAX, t // 4)
    n_groups = 1
    for cand in (8, 4, 2):
        if (t // cand) % tq == 0 and t // cand >= topk:
            n_groups = cand
            break
    glen = t // n_groups
    outs = []
    for g in range(n_groups):
        n_keys = (g + 1) * glen
        first = g * (glen // tq)
        qrow = lambda w: pl.BlockSpec((None, tq, w), lambda bb, i, first=first: (bb, first + i, 0))
        keys = lambda w: pl.BlockSpec((None, n_keys, w), lambda bb, i: (bb, 0, 0))
        outs.append(pl.pallas_call(
            functools.partial(_dsa_kernel, tq=tq, topk=topk, first_block=first),
            grid=(b, glen // tq),
            in_specs=[qrow(IDX_HEADS * IDX_DIM), keys(LANES), qrow(LANES), qrow(MIX_W), keys(MIX_W), keys(MIX_W)],
            out_specs=pl.BlockSpec((None, tq, MIX_W), lambda bb, i: (bb, i, 0)),
            out_shape=jax.ShapeDtypeStruct((b, glen, MIX_W), BF16),
            scratch_shapes=[pltpu.VMEM((tq, n_keys), jnp.int32), pltpu.VMEM((tq, n_keys), F32)],
            compiler_params=_cparams(("arbitrary", "arbitrary")),
            name="dsa_prompt",
        )(qidx, kidx2, misc, q, k, v))
    return outs[0] if n_groups == 1 else jnp.concatenate(outs, axis=1)


def _head_rows(x, n_rep):
    return jnp.concatenate([x] * n_rep, axis=0)


def _block_diag_q(q, tq):
    rows = N_HEADS * tq
    rhead = jnp.right_shift(lax.broadcasted_iota(jnp.int32, (rows, 1), 0), tq.bit_length() - 1)
    lhead = jnp.right_shift(lax.broadcasted_iota(jnp.int32, (1, MIX_W), 1), HEAD_DIM.bit_length() - 1)
    qq = _head_rows(q, N_HEADS)
    return jnp.where(rhead == lhead, qq, jnp.zeros_like(qq)), rhead == lhead


def _diag_out(o_bd, diag, tq):
    o = jnp.where(diag, o_bd, 0.0)
    out = o[0:tq]
    for h in range(1, N_HEADS):
        out = out + o[h * tq:(h + 1) * tq]
    return out


def _pad_rows(x, rows):
    return jnp.concatenate([x, jnp.zeros((rows - x.shape[0], x.shape[1]), x.dtype)], axis=0)


def _sb_sample_kernel(q_ref, kn_ref, vn_ref, kp_ref, vp_ref, o_ref, *, tk):
    tq = q_ref.shape[0]
    past = kp_ref.shape[1]
    rows = N_HEADS * tq
    qbd, diag = _block_diag_q(q_ref[...], tq)
    trow = lax.broadcasted_iota(jnp.int32, (rows, 1), 0) & (tq - 1)
    kn = _pad_rows(kn_ref[...], LANES).astype(BF16)
    vn = _pad_rows(vn_ref[...], LANES).astype(BF16)
    mask_n = lax.broadcasted_iota(jnp.int32, (1, LANES), 1) < trow
    acc = jnp.zeros((rows, MIX_W), F32)
    c = jnp.zeros((rows, 1), F32)
    upper = _strict_upper(LANES)
    acc, c = _sb_block(qbd, kn, vn, mask_n, upper, acc, c)
    for kb in range(past // tk - 1, -1, -1):
        kblk = kp_ref[:, kb * tk:(kb + 1) * tk].astype(BF16)
        vblk = vp_ref[:, kb * tk:(kb + 1) * tk].astype(BF16)
        acc, c = _sb_block(qbd, kblk, vblk, None, upper, acc, c, kv_t=True)
    o_ref[...] = _diag_out(acc, diag, tq).astype(o_ref.dtype)


def _fox_sample_kernel(q_ref, kn_ref, vn_ref, miscn_ref, kp_ref, vp_ref, lfp_ref, o_ref, *, tk):
    tq = q_ref.shape[0]
    past = kp_ref.shape[1]
    rows = N_HEADS * tq
    hp = lfp_ref.shape[0]
    qbd, diag = _block_diag_q(q_ref[...], tq)
    trow = lax.broadcasted_iota(jnp.int32, (rows, 1), 0) & (tq - 1)
    lane = lax.broadcasted_iota(jnp.int32, (1, LANES), 1)

    misc_n = _pad_rows(miscn_ref[...], LANES)
    pick = jnp.where(lax.broadcasted_iota(jnp.int32, (hp, LANES), 1)
                     == LF_LANE + lax.broadcasted_iota(jnp.int32, (hp, LANES), 0), 1.0, 0.0).astype(BF16)
    lf_new = sum(_dot_nt(pick, part) for part in _split3(misc_n))
    up_n = _strict_upper(LANES)
    g_new = sum(_dot(part, up_n) for part in _split3(lf_new))
    tot_new = jnp.sum(lf_new, axis=-1, keepdims=True)

    def expand(g):
        return jnp.concatenate([jnp.broadcast_to(g[h:h + 1], (tq, g.shape[1])) for h in range(N_HEADS)], axis=0)

    g_new_x = expand(g_new)
    g_q = jnp.sum(jnp.where(lane == trow, g_new_x, 0.0), axis=-1, keepdims=True)

    kn = _pad_rows(kn_ref[...], LANES).astype(BF16)
    vn = _pad_rows(vn_ref[...], LANES).astype(BF16)
    s_n = _dot_nt(qbd, kn) + g_new_x - g_q
    s_n = jnp.where(lane <= trow, s_n, -jnp.inf)
    m, l, acc = _softmax_step(s_n, vn, jnp.full((rows, 1), NEG_BIG, F32), jnp.zeros((rows, 1), F32),
                              jnp.zeros((rows, MIX_W), F32))
    sub = min(2 * LANES, tk)
    upper = _strict_upper(sub)
    carry = tot_new
    for kb in range(past // tk - 1, -1, -1):
        g_parts = []
        for j in range(tk // sub - 1, -1, -1):
            lf_sub = lfp_ref[:, kb * tk + j * sub:kb * tk + (j + 1) * sub]
            g_parts.append(sum(_dot(part, upper) for part in _split3(lf_sub)) + carry)
            carry = carry + jnp.sum(lf_sub, axis=-1, keepdims=True)
        g_blk = g_parts[0] if len(g_parts) == 1 else jnp.concatenate(g_parts[::-1], axis=1)
        kblk = kp_ref[:, kb * tk:(kb + 1) * tk].astype(BF16)
        vblk = vp_ref[:, kb * tk:(kb + 1) * tk].astype(BF16)
        s = _dot(qbd, kblk) + expand(g_blk) - g_q
        m, l, acc = _softmax_step(s, vblk, m, l, acc, v_t=True)
    o_ref[...] = _diag_out(acc / l, diag, tq).astype(o_ref.dtype)


def _dsa_sample_select_kernel(qidx_ref, miscn_ref, kip_ref, biasp_ref, biasn_ref, keyp_scr, keyn_scr, *, topk, pos0):
    nb, tq, _ = qidx_ref.shape
    past = kip_ref.shape[2]
    lane = lax.broadcasted_iota(jnp.int32, (1, LANES), 1)
    qchunk = jnp.right_shift(pos0 + lax.broadcasted_iota(jnp.int32, (tq, 1), 0), CHUNK_SHIFT)
    ppos = lax.broadcasted_iota(jnp.int32, (1, past), 1)
    npos = pos0 + lane
    ok_p = jnp.right_shift(ppos, CHUNK_SHIFT) <= qchunk
    ok_n = (jnp.right_shift(npos, CHUNK_SHIFT) <= qchunk) & (lane < tq)
    for bi in range(nb):
        qi = qidx_ref[bi].astype(F32)
        qs = jnp.concatenate([qi[:, h * IDX_DIM:(h + 1) * IDX_DIM] for h in range(IDX_HEADS)], axis=0).astype(BF16)
        wq = miscn_ref[bi]
        ws = jnp.concatenate([wq[:, WI_LANE + h:WI_LANE + h + 1] for h in range(IDX_HEADS)], axis=0)
        kip = kip_ref[bi].astype(BF16)
        kin = _pad_rows(wq[:, 0:IDX_DIM], LANES).astype(BF16)

        def head_sum(s):
            s = jnp.maximum(s, 0.0) * ws
            out = s[0:tq]
            for h in range(1, IDX_HEADS):
                out = out + s[h * tq:(h + 1) * tq]
            return out * IDX_HEAD_SCALE

        isc_p = jnp.where(ok_p, head_sum(_dot(qs, kip)), -jnp.inf)
        isc_n = jnp.where(ok_n, head_sum(_dot_nt(qs, kin)), -jnp.inf)
        keyp_scr[bi * tq:(bi + 1) * tq, :] = _sort_key(isc_p)
        keyn_scr[bi * tq:(bi + 1) * tq, :] = jnp.where(lane < tq, _sort_key(isc_n), INT_MIN)
    bias_p, bias_n = _topk_select([lambda: keyp_scr[...], lambda: keyn_scr[...]], [ppos, npos], nb * tq, topk,
                                  int(past + LANES).bit_length())
    for bi in range(nb):
        biasp_ref[bi] = bias_p[bi * tq:(bi + 1) * tq]
        biasn_ref[bi] = bias_n[bi * tq:(bi + 1) * tq]


def _dsa_sample_kernel(biasp_ref, biasn_ref, q_ref, kn_ref, vn_ref, kp_ref, vp_ref, o_ref):
    tq = q_ref.shape[0]
    bias_p = _head_rows(biasp_ref[...], N_HEADS)
    bias_n = _head_rows(biasn_ref[...], N_HEADS)

    qbd, diag = _block_diag_q(q_ref[...], tq)
    kn = _pad_rows(kn_ref[...], LANES).astype(BF16)
    vn = _pad_rows(vn_ref[...], LANES).astype(BF16)
    s_p = _dot(qbd, kp_ref[...].astype(BF16)) + bias_p
    s_n = _dot_nt(qbd, kn) + bias_n
    m = jnp.maximum(jnp.max(s_p, axis=-1, keepdims=True), jnp.max(s_n, axis=-1, keepdims=True))
    e_p = jnp.exp(s_p - m)
    e_n = jnp.exp(s_n - m)
    l = jnp.sum(e_p, axis=-1, keepdims=True) + jnp.sum(e_n, axis=-1, keepdims=True)
    o = (_dot_nt(e_p.astype(BF16), vp_ref[...].astype(BF16)) + _dot(e_n.astype(BF16), vn)) / l
    o_ref[...] = _diag_out(o, diag, tq).astype(o_ref.dtype)


def _sample_attention(layer, q_sb, k_sb, v_sb, q_fx, k_fx, v_fx, q_ds, k_ds, v_ds, qidx, misc,
                      c_sb_k, c_sb_v, c_fx_k, c_fx_v, lf_t, c_ds_k, c_ds_v, c_kidx):
    b, tq, _ = q_sb.shape
    past = c_sb_k.shape[3]
    tk = min(SAMPLE_KEYS, past)
    new = lambda w: pl.BlockSpec((None, tq, w), lambda bb: (bb, 0, 0))
    cache = lambda w: pl.BlockSpec((None, None, w, past), lambda bb: (layer, bb, 0, 0))
    out = jax.ShapeDtypeStruct((b, tq, MIX_W), BF16)
    cp = _cparams(("arbitrary",))
    o_sb = pl.pallas_call(
        functools.partial(_sb_sample_kernel, tk=tk), grid=(b,),
        in_specs=[new(MIX_W), new(MIX_W), new(MIX_W), cache(MIX_W), cache(MIX_W)],
        out_specs=new(MIX_W), out_shape=out, compiler_params=cp, name="sb_sample",
    )(q_sb, k_sb, v_sb, c_sb_k, c_sb_v)
    hp = lf_t.shape[2]
    o_fx = pl.pallas_call(
        functools.partial(_fox_sample_kernel, tk=tk), grid=(b,),
        in_specs=[new(MIX_W), new(MIX_W), new(MIX_W), new(LANES), cache(MIX_W), cache(MIX_W),
                  cache(hp)],
        out_specs=new(MIX_W), out_shape=out, compiler_params=cp, name="fox_sample",
    )(q_fx, k_fx, v_fx, misc, c_fx_k, c_fx_v, lf_t)
    topk = min(TOPK_MAX, (past + tq) // 4)
    nb = SELECT_BATCH if b % SELECT_BATCH == 0 else 1
    grp = lambda w: pl.BlockSpec((nb, tq, w), lambda g: (g, 0, 0))
    bias_p, bias_n = pl.pallas_call(
        functools.partial(_dsa_sample_select_kernel, topk=topk, pos0=past), grid=(b // nb,),
        in_specs=[grp(IDX_HEADS * IDX_DIM), grp(LANES),
                  pl.BlockSpec((None, nb, IDX_DIM, past), lambda g: (layer, g, 0, 0))],
        out_specs=[grp(past), grp(LANES)],
        out_shape=[jax.ShapeDtypeStruct((b, tq, past), F32), jax.ShapeDtypeStruct((b, tq, LANES), F32)],
        scratch_shapes=[pltpu.VMEM((nb * tq, past), jnp.int32), pltpu.VMEM((nb * tq, LANES), jnp.int32)],
        compiler_params=cp, name="dsa_sample_select",
    )(qidx, misc, c_kidx)
    o_ds = pl.pallas_call(
        _dsa_sample_kernel, grid=(b,),
        in_specs=[new(past), new(LANES), new(MIX_W), new(MIX_W), new(MIX_W), cache(MIX_W), cache(MIX_W)],
        out_specs=new(MIX_W), out_shape=out, compiler_params=cp, name="dsa_sample",
    )(bias_p, bias_n, q_ds, k_ds, v_ds, c_ds_k, c_ds_v)
    return o_sb, o_fx, o_ds


def _rope_tables(pos):
    half = HEAD_DIM // 2
    inv_freq = ROPE_THETA ** (-jnp.arange(half, dtype=F32) / half)
    ang = pos.astype(F32)[:, None] * inv_freq[None, :]
    cos = jnp.cos(ang)
    sin = jnp.sin(ang)
    cos_t = jnp.concatenate([cos, cos, cos, cos], axis=1)
    sin_t = jnp.concatenate([-sin, sin, -sin, sin], axis=1)
    return cos_t, sin_t


def _prep_weights(w_in, b_forget, ffn_w_in, ffn_w_out, w_branch, w_out):
    w3 = 3 * MIX_W
    o = np.cumsum([0, w3, w3, N_HEADS, w3, IDX_HEADS * IDX_DIM, IDX_DIM, IDX_HEADS]).tolist()
    sb, fx, fl, ds, qi, ki, wi = (w_in[:, :, o[k]:o[k + 1]] for k in range(7))
    pad = jnp.zeros(w_in.shape[:2] + (LANES - IDX_DIM - N_HEADS - IDX_HEADS,), w_in.dtype)
    w_main = jnp.concatenate([sb, fx, ds, qi, ki, fl, wi, pad], axis=2).astype(BF16)
    w_gate = w_in[:, :, o[7]:].astype(BF16)
    depth = w_in.shape[0]
    bf_row = jnp.zeros((depth, 1, LANES), F32).at[:, 0, LF_LANE:LF_LANE + N_HEADS].set(b_forget)
    return (w_main, w_gate, bf_row, ffn_w_in.astype(BF16), ffn_w_out.astype(BF16),
            w_branch.astype(BF16), w_out.astype(BF16))


def _trunk(x3, mod_l, gdiv, tm, pos, caches, norm_g, weights, *, tq_attn):
    w_main, w_gate, bf_row, w_up, w_dn, w_branch, w_out = weights
    b, t, d = x3.shape
    n = b * t
    depth = w_main.shape[0]
    x = x3.reshape(n, d)
    cos_t, sin_t = _rope_tables(pos)
    if caches is not None:
        cos_t = jnp.tile(cos_t, (tm // t, 1))
        sin_t = jnp.tile(sin_t, (tm // t, 1))
        lf_t = jnp.swapaxes(caches[4], 2, 3)
        lf_t = jnp.pad(lf_t, ((0, 0), (0, 0), (0, 16 - lf_t.shape[2]), (0, 0)))
    states = []
    for l in range(depth):
        g = lambda k: norm_g[l, k][None, :]
        mod = mod_l[l]
        x = _ffn(x, mod, 0, g(0), g(1), w_up[l, 0], w_dn[l, 0], tm=tm, gdiv=gdiv)
        (q_sb, k_sb, v_sb, q_fx, k_fx, v_fx, q_ds, k_ds, v_ds, qidx, kidx2, misc, *state_t) = _proj(
            x, mod, g(2), w_main[l], bf_row[l], cos_t, sin_t, tm=tm, gdiv=gdiv,
            batch_t=b if caches is None else None)
        r3 = lambda a: a.reshape(b, t, a.shape[-1])
        if caches is None:
            qa, ka = _fox_prep(r3(q_fx), r3(k_fx), r3(misc), tm=min(256, t))
            o_sb = _sb_prompt(r3(q_sb), r3(k_sb), r3(v_sb), tq=tq_attn, tk=min(512, t))
            o_fx = _fox_prompt(qa, ka, r3(v_fx), tq=tq_attn, tk=min(512, t))
            o_ds = _dsa_prompt(r3(qidx), r3(kidx2), r3(misc), r3(q_ds), r3(k_ds), r3(v_ds), tq=tq_attn)
        else:
            o_sb, o_fx, o_ds = _sample_attention(
                l, r3(q_sb), r3(k_sb), r3(v_sb), r3(q_fx), r3(k_fx), r3(v_fx), r3(q_ds), r3(k_ds), r3(v_ds),
                r3(qidx), r3(misc), caches[0], caches[1], caches[2], caches[3], lf_t,
                caches[5], caches[6], caches[7])
        o2 = lambda a: a.reshape(n, MIX_W)
        x = _merge(x, mod, g(2), g(3), o2(o_sb), o2(o_fx), o2(o_ds), w_gate[l], w_branch[l], w_out[l],
                   tm=tm, gdiv=gdiv)
        x = _ffn(x, mod, 6, g(4), g(5), w_up[l, 1], w_dn[l, 1], tm=tm, gdiv=gdiv)
        if state_t:
            hd = lambda a: jnp.transpose(a.reshape(b, N_HEADS, HEAD_DIM, t), (0, 3, 1, 2))
            kv = [hd(a) for a in state_t]
        else:
            hd = lambda a: a.reshape(b, t, N_HEADS, HEAD_DIM)
            kv = [hd(a) for a in (k_sb, v_sb, k_fx, v_fx, k_ds, v_ds)]
        states.append((kv[0], kv[1], kv[2], kv[3],
                       misc[:, LF_LANE:LF_LANE + N_HEADS].reshape(b, t, N_HEADS),
                       kv[4], kv[5], misc[:, 0:IDX_DIM].reshape(b, t, IDX_DIM)))
    return x.reshape(b, t, d), tuple(jnp.stack(s, axis=0) for s in zip(*states))


def kernel(x_prompt, x_sample, cache_sb_k, cache_sb_v, cache_fox_k, cache_fox_v, cache_fox_logf, cache_dsa_k, cache_dsa_v, cache_dsa_kidx, c_prompt, c_sample, norm_g, w_ada, b_ada, ffn_w_in, ffn_w_out, w_in, b_forget, w_branch, w_out):
    bp, tp, d = x_prompt.shape
    bs, ts, _ = x_sample.shape
    depth = w_in.shape[0]
    past = cache_sb_k.shape[2]
    mix = lambda a: jnp.transpose(a, (0, 1, 3, 4, 2)).reshape(a.shape[:2] + (MIX_W, a.shape[2]))
    caches = (mix(cache_sb_k), mix(cache_sb_v), mix(cache_fox_k), mix(cache_fox_v), cache_fox_logf,
              mix(cache_dsa_k), mix(cache_dsa_v), jnp.swapaxes(cache_dsa_kidx, 2, 3))
    weights = _prep_weights(w_in, b_forget, ffn_w_in, ffn_w_out, w_branch, w_out)

    rows = bp + bs
    rows_pad = -(-rows // 8) * 8
    c_all = jnp.concatenate([c_prompt, c_sample, jnp.zeros((rows_pad - rows, d), F32)], axis=0)
    mod = _ada(c_all, w_ada, b_ada)
    mod_p = [mod[l, :bp].reshape(bp, N_MOD, 1, d) for l in range(depth)]
    ns = bs * ts
    mod_s = [jnp.repeat(mod[l, bp:rows].reshape(bs, N_MOD, d), ts, axis=0)
             .reshape(1, ns, N_MOD, d).transpose(0, 2, 1, 3) for l in range(depth)]

    tm_p = min(512, tp)
    pos_p = jnp.arange(tp, dtype=jnp.int32)
    pos_s = past + jnp.arange(ts, dtype=jnp.int32)
    y_p, st_p = _trunk(x_prompt, mod_p, tp // tm_p, tm_p, pos_p, None, norm_g, weights, tq_attn=min(256, tp))
    y_s, st_s = _trunk(x_sample, mod_s, 1, ns, pos_s, caches, norm_g, weights, tq_attn=ts)
    return (y_p, y_s) + st_p + st_s
```

```python
import functools

import numpy as np
import jax
import jax.numpy as jnp
from jax import lax
from jax.experimental import pallas as pl
from jax.experimental.pallas import tpu as pltpu

F32 = jnp.float32
BF16 = jnp.bfloat16

HEAD_DIM = 64
N_HEADS = 6
MIX_W = N_HEADS * HEAD_DIM
N_PAIR = MIX_W // 128
IDX_HEADS = 4
IDX_DIM = 64
CHUNK = 64
CHUNK_SHIFT = 6
TOPK_MAX = 256
N_BRANCH = 3
N_MOD = 9
ROPE_THETA = 10000.0
EPS = 1e-6
FFN_RES = 0.5
QK_SCALE = HEAD_DIM ** -0.5
IDX_SCALE = IDX_DIM ** -0.5
IDX_HEAD_SCALE = IDX_HEADS ** -0.5
LANES = 128
LF_LANE = IDX_DIM
WI_LANE = IDX_DIM + N_HEADS
PROJ_W = 3 * 3 * MIX_W + IDX_HEADS * IDX_DIM + LANES
VMEM_LIMIT = 56 * 1024 * 1024
SELECT_BATCH = 8
SEARCH_UNROLL = 4
SAMPLE_KEYS = 2048
INT_MIN = np.int32(-2 ** 31)
KEY_NEG_INF = np.int32(-2139095041)
NEG_BIG = -1e30


def _cparams(sem):
    return pltpu.CompilerParams(dimension_semantics=sem, vmem_limit_bytes=VMEM_LIMIT)


def _dot(a, b):
    return jnp.dot(a, b, preferred_element_type=F32)


def _dot_nt(a, b):
    return lax.dot_general(a, b, (((1,), (1,)), ((), ())), preferred_element_type=F32)


def _split2(x):
    hi = x.astype(BF16)
    lo = (x - hi.astype(F32)).astype(BF16)
    return hi, lo


def _split3(x):
    p1 = x.astype(BF16)
    r1 = x - p1.astype(F32)
    p2 = r1.astype(BF16)
    p3 = (r1 - p2.astype(F32)).astype(BF16)
    return p1, p2, p3


def _log_sigmoid(x):
    return jnp.minimum(x, 0.0) - jnp.log1p(jnp.exp(-jnp.abs(x)))


def _rms(x, g):
    return x * lax.rsqrt(jnp.mean(x * x, axis=-1, keepdims=True) + EPS) * g


def _norm_mod(x, g, scale, shift):
    return _rms(x, g) * (1.0 + scale) + shift


def _strict_upper(n):
    r = lax.broadcasted_iota(jnp.int32, (n, n), 0)
    c = lax.broadcasted_iota(jnp.int32, (n, n), 1)
    return jnp.where(r > c, 1.0, 0.0).astype(BF16)


def _strict_lower(n):
    r = lax.broadcasted_iota(jnp.int32, (n, n), 0)
    c = lax.broadcasted_iota(jnp.int32, (n, n), 1)
    return jnp.where(c > r, 1.0, 0.0).astype(BF16)


def _ada_kernel(c_ref, w_ref, b_ref, o_ref):
    c = c_ref[...]
    s = (c * jax.nn.sigmoid(c)).astype(BF16)
    o_ref[0] = _dot(s, w_ref[0].astype(BF16)) + b_ref[0]


def _ada(c_all, w_ada, b_ada):
    depth, d, n = w_ada.shape
    rows = c_all.shape[0]
    tn = n // 8
    return pl.pallas_call(
        _ada_kernel,
        grid=(depth, n // tn),
        in_specs=[
            pl.BlockSpec((rows, d), lambda l, j: (0, 0)),
            pl.BlockSpec((1, d, tn), lambda l, j: (l, 0, j)),
            pl.BlockSpec((1, 1, tn), lambda l, j: (l, 0, j)),
        ],
        out_specs=pl.BlockSpec((1, rows, tn), lambda l, j: (l, 0, j)),
        out_shape=jax.ShapeDtypeStruct((depth, rows, n), F32),
        compiler_params=_cparams(("arbitrary", "arbitrary")),
        name="ada",
    )(c_all, w_ada, b_ada.reshape(depth, 1, n))


def _ffn_kernel(x_ref, shift_ref, scale_ref, gate_ref, gpre_ref, gpost_ref, wg_ref, wu_ref, wd_ref,
                o_ref, acc_scr, *, nj):
    j = pl.program_id(1)
    for jj in range(nj):
        @pl.when(j == jj)
        def _():
            x = x_ref[...]
            h = _norm_mod(x, gpre_ref[...], scale_ref[0, 0], shift_ref[0, 0]).astype(BF16)
            g = _dot(h, wg_ref[...])
            u = _dot(h, wu_ref[...])
            a = (g * jax.nn.sigmoid(g) * u).astype(BF16)
            d = _dot(a, wd_ref[...])
            acc = d if jj == 0 else acc_scr[...] + d
            if jj < nj - 1:
                acc_scr[...] = acc
            else:
                o_ref[...] = x + FFN_RES * gate_ref[0, 0] * _rms(acc, gpost_ref[...])


def _mod_spec(r, d, gdiv, comp):
    return pl.BlockSpec((1, 1, r, d), lambda i, *_: (i // gdiv, comp, 0, 0))


def _ffn(x, mod, comp0, g_pre, g_post, w_up, w_dn, *, tm, gdiv):
    n, d = x.shape
    ff = w_dn.shape[0]
    nj = 2 if (ff // 2) % LANES == 0 else 1
    tf = ff // nj
    r = mod.shape[2]
    return pl.pallas_call(
        functools.partial(_ffn_kernel, nj=nj),
        grid=(n // tm, nj),
        in_specs=[
            pl.BlockSpec((tm, d), lambda i, j: (i, 0)),
            _mod_spec(r, d, gdiv, comp0),
            _mod_spec(r, d, gdiv, comp0 + 1),
            _mod_spec(r, d, gdiv, comp0 + 2),
            pl.BlockSpec((1, d), lambda i, j: (0, 0)),
            pl.BlockSpec((1, d), lambda i, j: (0, 0)),
            pl.BlockSpec((d, tf), lambda i, j: (0, j)),
            pl.BlockSpec((d, tf), lambda i, j: (0, nj + j)),
            pl.BlockSpec((tf, d), lambda i, j: (j, 0)),
        ],
        out_specs=pl.BlockSpec((tm, d), lambda i, j: (i, 0)),
        out_shape=jax.ShapeDtypeStruct((n, d), F32),
        scratch_shapes=[pltpu.VMEM((tm, d), F32)],
        compiler_params=_cparams(("arbitrary", "arbitrary")),
        name="ffn",
    )(x, mod, mod, mod, g_pre, g_post, w_up, w_up, w_dn)


def _rope(x, cos, sin, first_half):
    outs = []
    for c in range(x.shape[1] // LANES):
        xs = x[:, c * LANES:(c + 1) * LANES]
        below = pltpu.roll(xs, 32, 1)
        above = pltpu.roll(xs, LANES - 32, 1)
        outs.append(xs * cos + jnp.where(first_half, above, below) * sin)
    return outs[0] if len(outs) == 1 else jnp.concatenate(outs, axis=1)


def _proj_kernel(*refs, n_prev, with_state_t):
    x_ref, shift_ref, scale_ref, gpre_ref, w_ref, bf_ref, cos_ref, sin_ref = refs[:8]
    prev_t = refs[8:8 + n_prev]
    qsb, ksb, vsb, qfx, kfx, vfx, qds, kds, vds, qidx, kidx2, misc = refs[8 + n_prev:20 + n_prev]
    state_t = refs[20 + n_prev:] if with_state_t else ()
    h = _norm_mod(x_ref[...], gpre_ref[...], scale_ref[0, 0], shift_ref[0, 0]).astype(BF16)
    cos = cos_ref[...]
    sin = sin_ref[...]
    lane = lax.broadcasted_iota(jnp.int32, (1, LANES), 1)
    first_half = (lane & (HEAD_DIM - 1)) < (HEAD_DIM // 2)
    w3 = 3 * MIX_W

    def put_kv(k, v, k_ref, v_ref, branch):
        k_ref[...] = k.astype(k_ref.dtype)
        v_ref[...] = v.astype(v_ref.dtype)
        for val, idx in ((k, 2 * branch), (v, 2 * branch + 1)):
            if state_t:
                layers = state_t[idx].shape[0]
                if n_prev:
                    state_t[idx][0:layers - 1] = prev_t[idx][...]
                state_t[idx][layers - 1] = val.T

    y = _dot(h, w_ref[:, 0:w3])
    qsb[...] = (y[:, 0:MIX_W] * QK_SCALE).astype(BF16)
    put_kv(y[:, MIX_W:2 * MIX_W], y[:, 2 * MIX_W:w3], ksb, vsb, 0)

    y = _dot(h, w_ref[:, w3:2 * w3])
    qfx[...] = (y[:, 0:MIX_W] * QK_SCALE).astype(BF16)
    put_kv(y[:, MIX_W:2 * MIX_W], y[:, 2 * MIX_W:w3], kfx, vfx, 1)

    y = _dot(h, w_ref[:, 2 * w3:3 * w3])
    qds[...] = (_rope(y[:, 0:MIX_W], cos, sin, first_half) * QK_SCALE).astype(BF16)
    put_kv(_rope(y[:, MIX_W:2 * MIX_W], cos, sin, first_half), y[:, 2 * MIX_W:w3], kds, vds, 2)

    y = _dot(h, w_ref[:, 3 * w3:PROJ_W])
    nq = IDX_HEADS * IDX_DIM
    qidx[...] = (_rope(y[:, 0:nq], cos, sin, first_half) * IDX_SCALE).astype(BF16)
    m = y[:, nq:nq + LANES]
    m_rot = _rope(m, cos, sin, first_half)
    lf = _log_sigmoid(m + bf_ref[...])
    misc[...] = jnp.where(lane < LF_LANE, m_rot,
                          jnp.where(lane < WI_LANE, lf,
                                    jnp.where(lane < WI_LANE + IDX_HEADS, m, 0.0)))
    kidx2[...] = jnp.where(lane < IDX_DIM, m_rot, pltpu.roll(m_rot, IDX_DIM, 1)).astype(BF16)


def _proj(x, mod, g_pre, w_main, bf_row, cos_t, sin_t, *, tm, gdiv, batch_t=None, prev_state_t=()):
    n, d = x.shape
    r = mod.shape[2]
    tab_tiles = cos_t.shape[0] // tm
    row = lambda w: pl.BlockSpec((tm, w), lambda i: (i, 0))
    tab = pl.BlockSpec((tm, LANES), lambda i: (i % tab_tiles, 0))
    shp = lambda w, dt: jax.ShapeDtypeStruct((n, w), dt)
    nq = IDX_HEADS * IDX_DIM
    kv_dt = F32 if batch_t is None else BF16
    out_specs = [row(MIX_W)] * 9 + [row(nq), row(LANES), row(LANES)]
    out_shape = [shp(MIX_W, BF16), shp(MIX_W, kv_dt), shp(MIX_W, kv_dt)] * 3 + [
        shp(nq, BF16), shp(LANES, BF16), shp(LANES, F32)]
    prev_specs = []
    if batch_t is not None:
        t = n // batch_t
        layers = (prev_state_t[0].shape[0] if prev_state_t else 0) + 1
        state_spec = lambda nl: pl.BlockSpec((nl, None, MIX_W, tm), lambda i: (0, i // gdiv, 0, i % gdiv))
        out_specs += [state_spec(layers)] * 6
        out_shape += [jax.ShapeDtypeStruct((layers, batch_t, MIX_W, t), F32)] * 6
        prev_specs = [state_spec(layers - 1)] * len(prev_state_t)
    return pl.pallas_call(
        functools.partial(_proj_kernel, n_prev=len(prev_state_t), with_state_t=batch_t is not None),
        grid=(n // tm,),
        in_specs=[
            row(d),
            _mod_spec(r, d, gdiv, 3),
            _mod_spec(r, d, gdiv, 4),
            pl.BlockSpec((1, d), lambda i: (0, 0)),
            pl.BlockSpec((d, PROJ_W), lambda i: (0, 0)),
            pl.BlockSpec((1, LANES), lambda i: (0, 0)),
            tab, tab,
        ] + prev_specs,
        out_specs=out_specs,
        out_shape=out_shape,
        compiler_params=_cparams(("arbitrary",)),
        name="proj",
    )(x, mod, mod, g_pre, w_main, bf_row, cos_t, sin_t, *prev_state_t)


def _merge_kernel(x_ref, shift_ref, scale_ref, gate_ref, gpre_ref, gpost_ref,
                  osb_ref, ofx_ref, ods_ref, wg_ref, wb_ref, wo_ref, o_ref):
    x = x_ref[...]
    d = x.shape[1]
    h = _norm_mod(x, gpre_ref[...], scale_ref[0, 0], shift_ref[0, 0]).astype(BF16)
    merged = None
    for nb, o_br in enumerate((osb_ref, ofx_ref, ods_ref)):
        gl = _dot(h, wg_ref[:, nb * d:(nb + 1) * d])
        y = _dot(o_br[...], wb_ref[nb])
        t = jax.nn.sigmoid(gl) * y
        merged = t if merged is None else merged + t
    out = _dot(merged.astype(BF16), wo_ref[...])
    o_ref[...] = x + gate_ref[0, 0] * _rms(out, gpost_ref[...])


def _merge(x, mod, g_pre, g_post, o_sb, o_fx, o_ds, w_gate, w_branch, w_out, *, tm, gdiv):
    n, d = x.shape
    r = mod.shape[2]
    row = lambda w: pl.BlockSpec((tm, w), lambda i: (i, 0))
    return pl.pallas_call(
        _merge_kernel,
        grid=(n // tm,),
        in_specs=[
            row(d),
            _mod_spec(r, d, gdiv, 3),
            _mod_spec(r, d, gdiv, 4),
            _mod_spec(r, d, gdiv, 5),
            pl.BlockSpec((1, d), lambda i: (0, 0)),
            pl.BlockSpec((1, d), lambda i: (0, 0)),
            row(MIX_W), row(MIX_W), row(MIX_W),
            pl.BlockSpec((d, N_BRANCH * d), lambda i: (0, 0)),
            pl.BlockSpec((N_BRANCH, MIX_W, d), lambda i: (0, 0, 0)),
            pl.BlockSpec((d, d), lambda i: (0, 0)),
        ],
        out_specs=row(d),
        out_shape=jax.ShapeDtypeStruct((n, d), F32),
        compiler_params=_cparams(("arbitrary",)),
        name="merge",
    )(x, mod, mod, mod, g_pre, g_post, o_sb, o_fx, o_ds, w_gate, w_branch, w_out)


def _aug_lanes(lane, base, ones_first, parts):
    one_lo, val_lo = (base, base + 3) if ones_first else (base + 3, base)
    out = jnp.where((lane >= one_lo) & (lane < one_lo + 3), 1.0, 0.0)
    for k, p in enumerate(parts):
        out = jnp.where(lane == val_lo + k, p, out)
    return out


def _fox_prep_kernel(q_ref, k_ref, misc_ref, qa_ref, ka_ref, carry):
    j = pl.program_id(1)

    @pl.when(j == 0)
    def _():
        carry[...] = jnp.zeros_like(carry)

    lf = misc_ref[...]
    tm = lf.shape[0]
    low = _strict_lower(tm)
    p1, p2, p3 = _split3(lf)
    g = _dot(low, p1) + _dot(low, p2) + _dot(low, p3) + carry[...]
    carry[...] += jnp.sum(lf, axis=0, keepdims=True)

    lane = lax.broadcasted_iota(jnp.int32, (1, LANES), 1)
    lo_half = lane < HEAD_DIM
    for p in range(N_PAIR):
        qp = q_ref[:, p * LANES:(p + 1) * LANES].astype(F32)
        kp = k_ref[:, p * LANES:(p + 1) * LANES]
        for a in range(2):
            hd = 2 * p + a
            gcol = g[:, LF_LANE + hd:LF_LANE + hd + 1]
            g1, g2, g3 = (t.astype(F32) for t in _split3(gcol))
            own = lo_half if a == 0 else jnp.logical_not(lo_half)
            base = HEAD_DIM if a == 0 else 0
            q_aug = jnp.where(own, qp, _aug_lanes(lane, base, True, (-g1, -g2, -g3)))
            k_aug = jnp.where(own, kp, _aug_lanes(lane, base, False, (g1, g2, g3)))
            qa_ref[:, hd * LANES:(hd + 1) * LANES] = q_aug.astype(BF16)
            ka_ref[:, hd * LANES:(hd + 1) * LANES] = k_aug.astype(BF16)


def _fox_prep(q, k, misc, *, tm):
    b, t, _ = q.shape
    nt = t // tm
    rev = lambda w: pl.BlockSpec((None, tm, w), lambda bb, j: (bb, nt - 1 - j, 0))
    return pl.pallas_call(
        _fox_prep_kernel,
        grid=(b, nt),
        in_specs=[rev(MIX_W), rev(MIX_W), rev(LANES)],
        out_specs=[rev(N_HEADS * LANES), rev(N_HEADS * LANES)],
        out_shape=[jax.ShapeDtypeStruct((b, t, N_HEADS * LANES), BF16)] * 2,
        scratch_shapes=[pltpu.VMEM((1, LANES), F32)],
        compiler_params=_cparams(("arbitrary", "arbitrary")),
        name="fox_prep",
    )(q, k, misc)


def _sb_block(qh, kblk, vblk, mask, upper, acc, c, kv_t=False):
    z = _dot(qh, kblk) if kv_t else _dot_nt(qh, kblk)
    neg_abs = lax.bitcast_convert_type(lax.bitcast_convert_type(z, jnp.int32) | INT_MIN, F32)
    ls = jnp.minimum(z, 0.0) - jnp.log(1.0 + jnp.exp(neg_abs))
    lk = ls - z
    if mask is not None:
        lk = jnp.where(mask, lk, 0.0)
    hi, lo = _split2(lk)
    sub = upper.shape[0]
    tot = []
    for j in range(z.shape[1] // sub - 1, -1, -1):
        sl = slice(j * sub, (j + 1) * sub)
        tot.append(_dot(hi[:, sl], upper) + _dot(lo[:, sl], upper) + c)
        c = c + jnp.sum(lk[:, sl], axis=-1, keepdims=True)
    tot = tot[0] if len(tot) == 1 else jnp.concatenate(tot[::-1], axis=1)
    a = jnp.exp(ls + tot)
    if mask is not None:
        a = jnp.where(mask, a, 0.0)
    a = a.astype(BF16)
    acc = acc + (_dot_nt(a, vblk) if kv_t else _dot(a, vblk))
    return acc, c


def _sb_kernel(q_ref, k_ref, v_ref, o_ref, *, tq, tk):
    i = pl.program_id(1)
    lane = lax.broadcasted_iota(jnp.int32, (1, LANES), 1)
    lo_half = lane < HEAD_DIM
    q2 = []
    for p in range(N_PAIR):
        q = q_ref[:, p * LANES:(p + 1) * LANES]
        zero = jnp.zeros_like(q)
        q2.append(jnp.concatenate([jnp.where(lo_half, q, zero), jnp.where(lo_half, zero, q)], axis=0))
    qpos = i * tq + (lax.broadcasted_iota(jnp.int32, (2 * tq, 1), 0) & (tq - 1))
    upper = _strict_upper(LANES)
    nkb = jnp.right_shift((i + 1) * tq + (tk - 1), tk.bit_length() - 1)

    def body(it, carry):
        start = pl.multiple_of((nkb - 1 - it) * tk, tk)
        mask = (start + lax.broadcasted_iota(jnp.int32, (1, tk), 1)) < qpos
        out = []
        for p in range(N_PAIR):
            kblk = k_ref[pl.ds(start, tk), p * LANES:(p + 1) * LANES].astype(BF16)
            vblk = v_ref[pl.ds(start, tk), p * LANES:(p + 1) * LANES].astype(BF16)
            out.extend(_sb_block(q2[p], kblk, vblk, mask, upper, carry[2 * p], carry[2 * p + 1]))
        return tuple(out)

    init = (jnp.zeros((2 * tq, LANES), F32), jnp.zeros((2 * tq, 1), F32)) * N_PAIR
    fin = lax.fori_loop(0, nkb, body, init)
    for p in range(N_PAIR):
        acc = fin[2 * p]
        o_ref[:, p * LANES:(p + 1) * LANES] = jnp.where(lo_half, acc[0:tq], acc[tq:2 * tq]).astype(o_ref.dtype)


def _sb_prompt(q, k, v, *, tq, tk):
    b, t, _ = q.shape
    full = pl.BlockSpec((None, t, MIX_W), lambda bb, i: (bb, 0, 0))
    rows = pl.BlockSpec((None, tq, MIX_W), lambda bb, i: (bb, i, 0))
    return pl.pallas_call(
        functools.partial(_sb_kernel, tq=tq, tk=tk),
        grid=(b, t // tq),
        in_specs=[rows, full, full],
        out_specs=rows,
        out_shape=jax.ShapeDtypeStruct((b, t, MIX_W), BF16),
        compiler_params=_cparams(("arbitrary", "arbitrary")),
        name="sb_prompt",
    )(q, k, v)


def _softmax_step(s, vblk, m, l, acc, v_t=False):
    m_new = jnp.maximum(m, jnp.max(s, axis=-1, keepdims=True))
    alpha = jnp.exp(m - m_new)
    p = jnp.exp(s - m_new)
    l = alpha * l + jnp.sum(p, axis=-1, keepdims=True)
    p = p.astype(BF16)
    acc = alpha * acc + (_dot_nt(p, vblk) if v_t else _dot(p, vblk))
    return m_new, l, acc


def _fox_kernel(qa_ref, ka_ref, v_ref, o_ref, *, tq, tk):
    i = pl.program_id(1)
    qpos = i * tq + lax.broadcasted_iota(jnp.int32, (tq, 1), 0)
    nkb = jnp.right_shift((i + 1) * tq + (tk - 1), tk.bit_length() - 1)

    def body(kb, carry):
        start = pl.multiple_of(kb * tk, tk)
        kpos = start + lax.broadcasted_iota(jnp.int32, (1, tk), 1)
        mask = kpos <= qpos
        out = []
        for p in range(N_PAIR):
            vblk = v_ref[pl.ds(start, tk), p * LANES:(p + 1) * LANES].astype(BF16)
            for a in range(2):
                hd = 2 * p + a
                m, l, acc = carry[3 * hd:3 * hd + 3]
                s = _dot_nt(qa_ref[:, hd * LANES:(hd + 1) * LANES],
                            ka_ref[pl.ds(start, tk), hd * LANES:(hd + 1) * LANES])
                s = jnp.where(mask, s, -jnp.inf)
                out.extend(_softmax_step(s, vblk, m, l, acc))
        return tuple(out)

    init = (jnp.full((tq, 1), NEG_BIG, F32), jnp.zeros((tq, 1), F32), jnp.zeros((tq, LANES), F32)) * N_HEADS
    fin = lax.fori_loop(0, nkb, body, init)
    lane = lax.broadcasted_iota(jnp.int32, (1, LANES), 1)
    for p in range(N_PAIR):
        o0 = fin[6 * p + 2] / fin[6 * p + 1]
        o1 = fin[6 * p + 5] / fin[6 * p + 4]
        o_ref[:, p * LANES:(p + 1) * LANES] = jnp.where(lane < HEAD_DIM, o0, o1).astype(o_ref.dtype)


def _fox_prompt(qa, ka, v, *, tq, tk):
    b, t, _ = v.shape
    return pl.pallas_call(
        functools.partial(_fox_kernel, tq=tq, tk=tk),
        grid=(b, t // tq),
        in_specs=[
            pl.BlockSpec((None, tq, N_HEADS * LANES), lambda bb, i: (bb, i, 0)),
            pl.BlockSpec((None, t, N_HEADS * LANES), lambda bb, i: (bb, 0, 0)),
            pl.BlockSpec((None, t, MIX_W), lambda bb, i: (bb, 0, 0)),
        ],
        out_specs=pl.BlockSpec((None, tq, MIX_W), lambda bb, i: (bb, i, 0)),
        out_shape=jax.ShapeDtypeStruct((b, t, MIX_W), BF16),
        compiler_params=_cparams(("arbitrary", "arbitrary")),
        name="fox_prompt",
    )(qa, ka, v)


def _sort_key(x):
    bits = lax.bitcast_convert_type(x, jnp.int32)
    key = jnp.where(bits < 0, bits ^ jnp.int32(0x7FFFFFFF), bits)
    return jnp.where(x == 0.0, 0, key)


def _count(pred_parts):
    tot = None
    for preds in pred_parts:
        ind = jnp.where(preds[-1], 1.0, 0.0)
        for p in preds[-2::-1]:
            ind = jnp.where(p, ind, 0.0)
        c = jnp.sum(ind, axis=-1, keepdims=True)
        tot = c if tot is None else tot + c
    return tot


def _kth_threshold(key_gets, rows, kf):
    def step(it, thr):
        cand = thr + jnp.left_shift(jnp.int32(1), 31 - it)
        cnt = _count([(kg() >= cand,) for kg in key_gets])
        return jnp.where(cnt >= kf, cand, thr)

    return lax.fori_loop(0, 32, step, jnp.full((rows, 1), INT_MIN, jnp.int32), unroll=SEARCH_UNROLL)


def _topk_select(key_gets, pos_list, rows, topk, idx_bits):
    kf = float(topk)
    thr = _kth_threshold(key_gets, rows, kf)
    need = kf - _count([(kg() > thr,) for kg in key_gets])
    n_eq = _count([(kg() == thr,) for kg in key_gets])

    def ibody(it, bound):
        cand = bound + jnp.left_shift(jnp.int32(1), idx_bits - 1 - it)
        cnt = _count([(kg() == thr, pos < cand) for kg, pos in zip(key_gets, pos_list)])
        return jnp.where(cnt <= need, cand, bound)

    some_partial = jnp.max(jnp.where(n_eq > need, 1.0, 0.0)) > 0.0
    bound = lax.cond(
        some_partial,
        lambda: lax.fori_loop(0, idx_bits, ibody, jnp.zeros((rows, 1), jnp.int32)),
        lambda: jnp.full((rows, 1), 1 << idx_bits, jnp.int32))
    tie = jnp.where(thr > KEY_NEG_INF, 0.0, -jnp.inf)
    out = []
    for kg, pos in zip(key_gets, pos_list):
        key = kg()
        at_thr = jnp.where(pos < bound, tie, -jnp.inf)
        out.append(jnp.where(key > thr, 0.0, jnp.where(key == thr, at_thr, -jnp.inf)))
    return out


def _dsa_body(i, n_keys, qidx_ref, kidx2_ref, miscq_ref, q_ref, k_ref, v_ref, o_ref, key_scr, bias_scr, *, tq, topk):
    lane = lax.broadcasted_iota(jnp.int32, (1, LANES), 1)
    lo_half = lane < HEAD_DIM
    kidx2 = kidx2_ref[0:n_keys, :]
    wq = miscq_ref[...]
    isc = jnp.zeros((tq, n_keys), F32)
    for h in range(IDX_HEADS):
        qp = qidx_ref[:, (h // 2) * LANES:(h // 2 + 1) * LANES]
        own = lo_half if h % 2 == 0 else jnp.logical_not(lo_half)
        qh = jnp.where(own, qp, jnp.zeros_like(qp))
        s = _dot_nt(qh, kidx2)
        isc = isc + jnp.maximum(s, 0.0) * wq[:, WI_LANE + h:WI_LANE + h + 1]
    isc = isc * IDX_HEAD_SCALE
    qpos = i * tq + lax.broadcasted_iota(jnp.int32, (tq, 1), 0)
    kpos = lax.broadcasted_iota(jnp.int32, (1, n_keys), 1)
    allowed = jnp.right_shift(kpos, CHUNK_SHIFT) <= jnp.right_shift(qpos, CHUNK_SHIFT)
    isc = jnp.where(allowed, isc, -jnp.inf)
    key_scr[:, 0:n_keys] = _sort_key(isc)
    (bias,) = _topk_select([lambda: key_scr[:, 0:n_keys]], [kpos], tq, topk, int(n_keys).bit_length())
    bias_scr[:, 0:n_keys] = bias

    for p in range(N_PAIR):
        qp = q_ref[:, p * LANES:(p + 1) * LANES]
        kp = k_ref[0:n_keys, p * LANES:(p + 1) * LANES].astype(BF16)
        vp = v_ref[0:n_keys, p * LANES:(p + 1) * LANES].astype(BF16)
        outs = []
        for a in range(2):
            own = lo_half if a == 0 else jnp.logical_not(lo_half)
            qh = jnp.where(own, qp, jnp.zeros_like(qp))
            s = _dot_nt(qh, kp) + bias_scr[:, 0:n_keys]
            m = jnp.max(s, axis=-1, keepdims=True)
            e = jnp.exp(s - m)
            l = jnp.sum(e, axis=-1, keepdims=True)
            outs.append(_dot(e.astype(BF16), vp) / l)
        o_ref[:, p * LANES:(p + 1) * LANES] = jnp.where(lo_half, outs[0], outs[1]).astype(o_ref.dtype)


def _dsa_kernel(*refs, tq, topk, first_block):
    n_keys = refs[4].shape[0]
    _dsa_body(first_block + pl.program_id(1), n_keys, *refs, tq=tq, topk=topk)


def _dsa_prompt(qidx, kidx2, misc, q, k, v, *, tq):
    b, t, _ = q.shape
    topk = min(TOPK_MAX, t // 4)
    n_groups = 1
    for cand in (8, 4, 2):
        if (t // cand) % tq == 0 and t // cand >= topk:
            n_groups = cand
            break
    glen = t // n_groups
    outs = []
    for g in range(n_groups):
        n_keys = (g + 1) * glen
        first = g * (glen // tq)
        qrow = lambda w: pl.BlockSpec((None, tq, w), lambda bb, i, first=first: (bb, first + i, 0))
        keys = lambda w: pl.BlockSpec((None, n_keys, w), lambda bb, i: (bb, 0, 0))
        outs.append(pl.pallas_call(
            functools.partial(_dsa_kernel, tq=tq, topk=topk, first_block=first),
            grid=(b, glen // tq),
            in_specs=[qrow(IDX_HEADS * IDX_DIM), keys(LANES), qrow(LANES), qrow(MIX_W), keys(MIX_W), keys(MIX_W)],
            out_specs=pl.BlockSpec((None, tq, MIX_W), lambda bb, i: (bb, i, 0)),
            out_shape=jax.ShapeDtypeStruct((b, glen, MIX_W), BF16),
            scratch_shapes=[pltpu.VMEM((tq, n_keys), jnp.int32), pltpu.VMEM((tq, n_keys), F32)],
            compiler_params=_cparams(("arbitrary", "arbitrary")),
            name="dsa_prompt",
        )(qidx, kidx2, misc, q, k, v))
    return outs[0] if n_groups == 1 else jnp.concatenate(outs, axis=1)


def _head_rows(x, n_rep):
    return jnp.concatenate([x] * n_rep, axis=0)


def _block_diag_q(q, tq):
    rows = N_HEADS * tq
    rhead = jnp.right_shift(lax.broadcasted_iota(jnp.int32, (rows, 1), 0), tq.bit_length() - 1)
    lhead = jnp.right_shift(lax.broadcasted_iota(jnp.int32, (1, MIX_W), 1), HEAD_DIM.bit_length() - 1)
    qq = _head_rows(q, N_HEADS)
    return jnp.where(rhead == lhead, qq, jnp.zeros_like(qq)), rhead == lhead


def _diag_out(o_bd, diag, tq):
    o = jnp.where(diag, o_bd, 0.0)
    out = o[0:tq]
    for h in range(1, N_HEADS):
        out = out + o[h * tq:(h + 1) * tq]
    return out


def _pad_rows(x, rows):
    return jnp.concatenate([x, jnp.zeros((rows - x.shape[0], x.shape[1]), x.dtype)], axis=0)


def _sb_sample_kernel(q_ref, kn_ref, vn_ref, kp_ref, vp_ref, o_ref, *, tk):
    tq = q_ref.shape[0]
    past = kp_ref.shape[1]
    rows = N_HEADS * tq
    qbd, diag = _block_diag_q(q_ref[...], tq)
    trow = lax.broadcasted_iota(jnp.int32, (rows, 1), 0) & (tq - 1)
    kn = _pad_rows(kn_ref[...], LANES).astype(BF16)
    vn = _pad_rows(vn_ref[...], LANES).astype(BF16)
    mask_n = lax.broadcasted_iota(jnp.int32, (1, LANES), 1) < trow
    acc = jnp.zeros((rows, MIX_W), F32)
    c = jnp.zeros((rows, 1), F32)
    upper = _strict_upper(LANES)
    acc, c = _sb_block(qbd, kn, vn, mask_n, upper, acc, c)
    for kb in range(past // tk - 1, -1, -1):
        kblk = kp_ref[:, kb * tk:(kb + 1) * tk].astype(BF16)
        vblk = vp_ref[:, kb * tk:(kb + 1) * tk].astype(BF16)
        acc, c = _sb_block(qbd, kblk, vblk, None, upper, acc, c, kv_t=True)
    o_ref[...] = _diag_out(acc, diag, tq).astype(o_ref.dtype)


def _fox_sample_kernel(q_ref, kn_ref, vn_ref, miscn_ref, kp_ref, vp_ref, lfp_ref, o_ref, *, tk):
    tq = q_ref.shape[0]
    past = kp_ref.shape[1]
    rows = N_HEADS * tq
    hp = lfp_ref.shape[0]
    qbd, diag = _block_diag_q(q_ref[...], tq)
    trow = lax.broadcasted_iota(jnp.int32, (rows, 1), 0) & (tq - 1)
    lane = lax.broadcasted_iota(jnp.int32, (1, LANES), 1)

    misc_n = _pad_rows(miscn_ref[...], LANES)
    pick = jnp.where(lax.broadcasted_iota(jnp.int32, (hp, LANES), 1)
                     == LF_LANE + lax.broadcasted_iota(jnp.int32, (hp, LANES), 0), 1.0, 0.0).astype(BF16)
    lf_new = sum(_dot_nt(pick, part) for part in _split3(misc_n))
    up_n = _strict_upper(LANES)
    g_new = sum(_dot(part, up_n) for part in _split3(lf_new))
    tot_new = jnp.sum(lf_new, axis=-1, keepdims=True)

    def expand(g):
        return jnp.concatenate([jnp.broadcast_to(g[h:h + 1], (tq, g.shape[1])) for h in range(N_HEADS)], axis=0)

    g_new_x = expand(g_new)
    g_q = jnp.sum(jnp.where(lane == trow, g_new_x, 0.0), axis=-1, keepdims=True)

    kn = _pad_rows(kn_ref[...], LANES).astype(BF16)
    vn = _pad_rows(vn_ref[...], LANES).astype(BF16)
    s_n = _dot_nt(qbd, kn) + g_new_x - g_q
    s_n = jnp.where(lane <= trow, s_n, -jnp.inf)
    m, l, acc = _softmax_step(s_n, vn, jnp.full((rows, 1), NEG_BIG, F32), jnp.zeros((rows, 1), F32),
                              jnp.zeros((rows, MIX_W), F32))
    sub = min(2 * LANES, tk)
    upper = _strict_upper(sub)
    carry = tot_new
    for kb in range(past // tk - 1, -1, -1):
        g_parts = []
        for j in range(tk // sub - 1, -1, -1):
            lf_sub = lfp_ref[:, kb * tk + j * sub:kb * tk + (j + 1) * sub]
            g_parts.append(sum(_dot(part, upper) for part in _split3(lf_sub)) + carry)
            carry = carry + jnp.sum(lf_sub, axis=-1, keepdims=True)
        g_blk = g_parts[0] if len(g_parts) == 1 else jnp.concatenate(g_parts[::-1], axis=1)
        kblk = kp_ref[:, kb * tk:(kb + 1) * tk].astype(BF16)
        vblk = vp_ref[:, kb * tk:(kb + 1) * tk].astype(BF16)
        s = _dot(qbd, kblk) + expand(g_blk) - g_q
        m, l, acc = _softmax_step(s, vblk, m, l, acc, v_t=True)
    o_ref[...] = _diag_out(acc / l, diag, tq).astype(o_ref.dtype)


def _dsa_sample_select_kernel(qidx_ref, miscn_ref, kip_ref, biasp_ref, biasn_ref, keyp_scr, keyn_scr, *, topk, pos0):
    nb, tq, _ = qidx_ref.shape
    past = kip_ref.shape[2]
    lane = lax.broadcasted_iota(jnp.int32, (1, LANES), 1)
    qchunk = jnp.right_shift(pos0 + lax.broadcasted_iota(jnp.int32, (tq, 1), 0), CHUNK_SHIFT)
    ppos = lax.broadcasted_iota(jnp.int32, (1, past), 1)
    npos = pos0 + lane
    ok_p = jnp.right_shift(ppos, CHUNK_SHIFT) <= qchunk
    ok_n = (jnp.right_shift(npos, CHUNK_SHIFT) <= qchunk) & (lane < tq)
    for bi in range(nb):
        qi = qidx_ref[bi].astype(F32)
        qs = jnp.concatenate([qi[:, h * IDX_DIM:(h + 1) * IDX_DIM] for h in range(IDX_HEADS)], axis=0).astype(BF16)
        wq = miscn_ref[bi]
        ws = jnp.concatenate([wq[:, WI_LANE + h:WI_LANE + h + 1] for h in range(IDX_HEADS)], axis=0)
        kip = kip_ref[bi].astype(BF16)
        kin = _pad_rows(wq[:, 0:IDX_DIM], LANES).astype(BF16)

        def head_sum(s):
            s = jnp.maximum(s, 0.0) * ws
            out = s[0:tq]
            for h in range(1, IDX_HEADS):
                out = out + s[h * tq:(h + 1) * tq]
            return out * IDX_HEAD_SCALE

        isc_p = jnp.where(ok_p, head_sum(_dot(qs, kip)), -jnp.inf)
        isc_n = jnp.where(ok_n, head_sum(_dot_nt(qs, kin)), -jnp.inf)
        keyp_scr[bi * tq:(bi + 1) * tq, :] = _sort_key(isc_p)
        keyn_scr[bi * tq:(bi + 1) * tq, :] = jnp.where(lane < tq, _sort_key(isc_n), INT_MIN)
    bias_p, bias_n = _topk_select([lambda: keyp_scr[...], lambda: keyn_scr[...]], [ppos, npos], nb * tq, topk,
                                  int(past + LANES).bit_length())
    for bi in range(nb):
        biasp_ref[bi] = bias_p[bi * tq:(bi + 1) * tq]
        biasn_ref[bi] = bias_n[bi * tq:(bi + 1) * tq]


def _dsa_sample_kernel(biasp_ref, biasn_ref, q_ref, kn_ref, vn_ref, kp_ref, vp_ref, o_ref):
    tq = q_ref.shape[0]
    bias_p = _head_rows(biasp_ref[...], N_HEADS)
    bias_n = _head_rows(biasn_ref[...], N_HEADS)

    qbd, diag = _block_diag_q(q_ref[...], tq)
    kn = _pad_rows(kn_ref[...], LANES).astype(BF16)
    vn = _pad_rows(vn_ref[...], LANES).astype(BF16)
    s_p = _dot(qbd, kp_ref[...].astype(BF16)) + bias_p
    s_n = _dot_nt(qbd, kn) + bias_n
    m = jnp.maximum(jnp.max(s_p, axis=-1, keepdims=True), jnp.max(s_n, axis=-1, keepdims=True))
    e_p = jnp.exp(s_p - m)
    e_n = jnp.exp(s_n - m)
    l = jnp.sum(e_p, axis=-1, keepdims=True) + jnp.sum(e_n, axis=-1, keepdims=True)
    o = (_dot_nt(e_p.astype(BF16), vp_ref[...].astype(BF16)) + _dot(e_n.astype(BF16), vn)) / l
    o_ref[...] = _diag_out(o, diag, tq).astype(o_ref.dtype)


def _sample_attention(layer, q_sb, k_sb, v_sb, q_fx, k_fx, v_fx, q_ds, k_ds, v_ds, qidx, misc,
                      c_sb_k, c_sb_v, c_fx_k, c_fx_v, lf_t, c_ds_k, c_ds_v, c_kidx):
    b, tq, _ = q_sb.shape
    past = c_sb_k.shape[3]
    tk = min(SAMPLE_KEYS, past)
    new = lambda w: pl.BlockSpec((None, tq, w), lambda bb: (bb, 0, 0))
    cache = lambda w: pl.BlockSpec((None, None, w, past), lambda bb: (layer, bb, 0, 0))
    out = jax.ShapeDtypeStruct((b, tq, MIX_W), BF16)
    cp = _cparams(("arbitrary",))
    o_sb = pl.pallas_call(
        functools.partial(_sb_sample_kernel, tk=tk), grid=(b,),
        in_specs=[new(MIX_W), new(MIX_W), new(MIX_W), cache(MIX_W), cache(MIX_W)],
        out_specs=new(MIX_W), out_shape=out, compiler_params=cp, name="sb_sample",
    )(q_sb, k_sb, v_sb, c_sb_k, c_sb_v)
    hp = lf_t.shape[2]
    o_fx = pl.pallas_call(
        functools.partial(_fox_sample_kernel, tk=tk), grid=(b,),
        in_specs=[new(MIX_W), new(MIX_W), new(MIX_W), new(LANES), cache(MIX_W), cache(MIX_W),
                  cache(hp)],
        out_specs=new(MIX_W), out_shape=out, compiler_params=cp, name="fox_sample",
    )(q_fx, k_fx, v_fx, misc, c_fx_k, c_fx_v, lf_t)
    topk = min(TOPK_MAX, (past + tq) // 4)
    nb = SELECT_BATCH if b % SELECT_BATCH == 0 else 1
    grp = lambda w: pl.BlockSpec((nb, tq, w), lambda g: (g, 0, 0))
    bias_p, bias_n = pl.pallas_call(
        functools.partial(_dsa_sample_select_kernel, topk=topk, pos0=past), grid=(b // nb,),
        in_specs=[grp(IDX_HEADS * IDX_DIM), grp(LANES),
                  pl.BlockSpec((None, nb, IDX_DIM, past), lambda g: (layer, g, 0, 0))],
        out_specs=[grp(past), grp(LANES)],
        out_shape=[jax.ShapeDtypeStruct((b, tq, past), F32), jax.ShapeDtypeStruct((b, tq, LANES), F32)],
        scratch_shapes=[pltpu.VMEM((nb * tq, past), jnp.int32), pltpu.VMEM((nb * tq, LANES), jnp.int32)],
        compiler_params=cp, name="dsa_sample_select",
    )(qidx, misc, c_kidx)
    o_ds = pl.pallas_call(
        _dsa_sample_kernel, grid=(b,),
        in_specs=[new(past), new(LANES), new(MIX_W), new(MIX_W), new(MIX_W), cache(MIX_W), cache(MIX_W)],
        out_specs=new(MIX_W), out_shape=out, compiler_params=cp, name="dsa_sample",
    )(bias_p, bias_n, q_ds, k_ds, v_ds, c_ds_k, c_ds_v)
    return o_sb, o_fx, o_ds


def _rope_tables(pos):
    half = HEAD_DIM // 2
    inv_freq = ROPE_THETA ** (-jnp.arange(half, dtype=F32) / half)
    ang = pos.astype(F32)[:, None] * inv_freq[None, :]
    cos = jnp.cos(ang)
    sin = jnp.sin(ang)
    cos_t = jnp.concatenate([cos, cos, cos, cos], axis=1)
    sin_t = jnp.concatenate([-sin, sin, -sin, sin], axis=1)
    return cos_t, sin_t


def _prep_weights(w_in, b_forget, ffn_w_in, ffn_w_out, w_branch, w_out):
    w3 = 3 * MIX_W
    o = np.cumsum([0, w3, w3, N_HEADS, w3, IDX_HEADS * IDX_DIM, IDX_DIM, IDX_HEADS]).tolist()
    sb, fx, fl, ds, qi, ki, wi = (w_in[:, :, o[k]:o[k + 1]] for k in range(7))
    pad = jnp.zeros(w_in.shape[:2] + (LANES - IDX_DIM - N_HEADS - IDX_HEADS,), w_in.dtype)
    w_main = jnp.concatenate([sb, fx, ds, qi, ki, fl, wi, pad], axis=2).astype(BF16)
    w_gate = w_in[:, :, o[7]:].astype(BF16)
    depth = w_in.shape[0]
    bf_row = jnp.zeros((depth, 1, LANES), F32).at[:, 0, LF_LANE:LF_LANE + N_HEADS].set(b_forget)
    return (w_main, w_gate, bf_row, ffn_w_in.astype(BF16), ffn_w_out.astype(BF16),
            w_branch.astype(BF16), w_out.astype(BF16))


def _trunk(x3, mod_l, gdiv, tm, pos, caches, norm_g, weights, *, tq_attn):
    w_main, w_gate, bf_row, w_up, w_dn, w_branch, w_out = weights
    b, t, d = x3.shape
    n = b * t
    depth = w_main.shape[0]
    x = x3.reshape(n, d)
    cos_t, sin_t = _rope_tables(pos)
    if caches is not None:
        cos_t = jnp.tile(cos_t, (tm // t, 1))
        sin_t = jnp.tile(sin_t, (tm // t, 1))
        lf_t = jnp.swapaxes(caches[4], 2, 3)
        lf_t = jnp.pad(lf_t, ((0, 0), (0, 0), (0, 16 - lf_t.shape[2]), (0, 0)))
    states = []
    prev_t = ()
    for l in range(depth):
        g = lambda k: norm_g[l, k][None, :]
        mod = mod_l[l]
        x = _ffn(x, mod, 0, g(0), g(1), w_up[l, 0], w_dn[l, 0], tm=tm, gdiv=gdiv)
        (q_sb, k_sb, v_sb, q_fx, k_fx, v_fx, q_ds, k_ds, v_ds, qidx, kidx2, misc, *state_t) = _proj(
            x, mod, g(2), w_main[l], bf_row[l], cos_t, sin_t, tm=tm, gdiv=gdiv,
            batch_t=b if caches is None else None, prev_state_t=prev_t)
        r3 = lambda a: a.reshape(b, t, a.shape[-1])
        if caches is None:
            qa, ka = _fox_prep(r3(q_fx), r3(k_fx), r3(misc), tm=min(256, t))
            o_sb = _sb_prompt(r3(q_sb), r3(k_sb), r3(v_sb), tq=tq_attn, tk=min(512, t))
            o_fx = _fox_prompt(qa, ka, r3(v_fx), tq=tq_attn, tk=min(512, t))
            o_ds = _dsa_prompt(r3(qidx), r3(kidx2), r3(misc), r3(q_ds), r3(k_ds), r3(v_ds), tq=tq_attn)
        else:
            o_sb, o_fx, o_ds = _sample_attention(
                l, r3(q_sb), r3(k_sb), r3(v_sb), r3(q_fx), r3(k_fx), r3(v_fx), r3(q_ds), r3(k_ds), r3(v_ds),
                r3(qidx), r3(misc), caches[0], caches[1], caches[2], caches[3], lf_t,
                caches[5], caches[6], caches[7])
        o2 = lambda a: a.reshape(n, MIX_W)
        x = _merge(x, mod, g(2), g(3), o2(o_sb), o2(o_fx), o2(o_ds), w_gate[l], w_branch[l], w_out[l],
                   tm=tm, gdiv=gdiv)
        x = _ffn(x, mod, 6, g(4), g(5), w_up[l, 1], w_dn[l, 1], tm=tm, gdiv=gdiv)
        prev_t = tuple(state_t)
        hd = lambda a: a.reshape(b, t, N_HEADS, HEAD_DIM)
        kv = [None] * 6 if state_t else [hd(a) for a in (k_sb, v_sb, k_fx, v_fx, k_ds, v_ds)]
        states.append((kv[0], kv[1], kv[2], kv[3],
                       misc[:, LF_LANE:LF_LANE + N_HEADS].reshape(b, t, N_HEADS),
                       kv[4], kv[5], misc[:, 0:IDX_DIM].reshape(b, t, IDX_DIM)))
    stacked = [None if s[0] is None else jnp.stack(s, axis=0) for s in zip(*states)]
    if prev_t:
        hd_t = lambda a: jnp.transpose(a.reshape(depth, b, N_HEADS, HEAD_DIM, t), (0, 1, 4, 2, 3))
        for slot, a in zip((0, 1, 2, 3, 5, 6), prev_t):
            stacked[slot] = hd_t(a)
    return x.reshape(b, t, d), tuple(stacked)


def kernel(x_prompt, x_sample, cache_sb_k, cache_sb_v, cache_fox_k, cache_fox_v, cache_fox_logf, cache_dsa_k, cache_dsa_v, cache_dsa_kidx, c_prompt, c_sample, norm_g, w_ada, b_ada, ffn_w_in, ffn_w_out, w_in, b_forget, w_branch, w_out):
    bp, tp, d = x_prompt.shape
    bs, ts, _ = x_sample.shape
    depth = w_in.shape[0]
    past = cache_sb_k.shape[2]
    mix = lambda a: jnp.transpose(a, (0, 1, 3, 4, 2)).reshape(a.shape[:2] + (MIX_W, a.shape[2]))
    caches = (mix(cache_sb_k), mix(cache_sb_v), mix(cache_fox_k), mix(cache_fox_v), cache_fox_logf,
              mix(cache_dsa_k), mix(cache_dsa_v), jnp.swapaxes(cache_dsa_kidx, 2, 3))
    weights = _prep_weights(w_in, b_forget, ffn_w_in, ffn_w_out, w_branch, w_out)

    rows = bp + bs
    rows_pad = -(-rows // 8) * 8
    c_all = jnp.concatenate([c_prompt, c_sample, jnp.zeros((rows_pad - rows, d), F32)], axis=0)
    mod = _ada(c_all, w_ada, b_ada)
    mod_p = [mod[l, :bp].reshape(bp, N_MOD, 1, d) for l in range(depth)]
    ns = bs * ts
    mod_s = [jnp.repeat(mod[l, bp:rows].reshape(bs, N_MOD, d), ts, axis=0)
             .reshape(1, ns, N_MOD, d).transpose(0, 2, 1, 3) for l in range(depth)]

    tm_p = min(512, tp)
    pos_p = jnp.arange(tp, dtype=jnp.int32)
    pos_s = past + jnp.arange(ts, dtype=jnp.int32)
    y_p, st_p = _trunk(x_prompt, mod_p, tp // tm_p, tm_p, pos_p, None, norm_g, weights, tq_attn=min(256, tp))
    y_s, st_s = _trunk(x_sample, mod_s, 1, ns, pos_s, caches, norm_g, weights, tq_attn=ts)
    return (y_p, y_s) + st_p + st_s
```

```python
import functools

import numpy as np
import jax
import jax.numpy as jnp
from jax import lax
from jax.experimental import pallas as pl
from jax.experimental.pallas import tpu as pltpu

F32 = jnp.float32
BF16 = jnp.bfloat16

HEAD_DIM = 64
N_HEADS = 6
MIX_W = N_HEADS * HEAD_DIM
N_PAIR = MIX_W // 128
IDX_HEADS = 4
IDX_DIM = 64
CHUNK = 64
CHUNK_SHIFT = 6
TOPK_MAX = 256
N_BRANCH = 3
N_MOD = 9
ROPE_THETA = 10000.0
EPS = 1e-6
FFN_RES = 0.5
QK_SCALE = HEAD_DIM ** -0.5
IDX_SCALE = IDX_DIM ** -0.5
IDX_HEAD_SCALE = IDX_HEADS ** -0.5
LANES = 128
LF_LANE = IDX_DIM
WI_LANE = IDX_DIM + N_HEADS
PROJ_W = 3 * 3 * MIX_W + IDX_HEADS * IDX_DIM + LANES
VMEM_LIMIT = 56 * 1024 * 1024
SELECT_BATCH = 8
SEARCH_UNROLL = 4
SAMPLE_KEYS = 2048
INT_MIN = np.int32(-2 ** 31)
KEY_NEG_INF = np.int32(-2139095041)
NEG_BIG = -1e30


def _cparams(sem):
    return pltpu.CompilerParams(dimension_semantics=sem, vmem_limit_bytes=VMEM_LIMIT)


def _dot(a, b):
    return jnp.dot(a, b, preferred_element_type=F32)


def _dot_nt(a, b):
    return lax.dot_general(a, b, (((1,), (1,)), ((), ())), preferred_element_type=F32)


def _split2(x):
    hi = x.astype(BF16)
    lo = (x - hi.astype(F32)).astype(BF16)
    return hi, lo


def _split3(x):
    p1 = x.astype(BF16)
    r1 = x - p1.astype(F32)
    p2 = r1.astype(BF16)
    p3 = (r1 - p2.astype(F32)).astype(BF16)
    return p1, p2, p3


def _log_sigmoid(x):
    return jnp.minimum(x, 0.0) - jnp.log1p(jnp.exp(-jnp.abs(x)))


def _rms(x, g):
    return x * lax.rsqrt(jnp.mean(x * x, axis=-1, keepdims=True) + EPS) * g


def _norm_mod(x, g, scale, shift):
    return _rms(x, g) * (1.0 + scale) + shift


def _strict_upper(n):
    r = lax.broadcasted_iota(jnp.int32, (n, n), 0)
    c = lax.broadcasted_iota(jnp.int32, (n, n), 1)
    return jnp.where(r > c, 1.0, 0.0).astype(BF16)


def _strict_lower(n):
    r = lax.broadcasted_iota(jnp.int32, (n, n), 0)
    c = lax.broadcasted_iota(jnp.int32, (n, n), 1)
    return jnp.where(c > r, 1.0, 0.0).astype(BF16)


def _ada_kernel(c_ref, w_ref, b_ref, o_ref):
    c = c_ref[...]
    s = (c * jax.nn.sigmoid(c)).astype(BF16)
    o_ref[0] = _dot(s, w_ref[0].astype(BF16)) + b_ref[0]


def _ada(c_all, w_ada, b_ada):
    depth, d, n = w_ada.shape
    rows = c_all.shape[0]
    tn = n // 8
    return pl.pallas_call(
        _ada_kernel,
        grid=(depth, n // tn),
        in_specs=[
            pl.BlockSpec((rows, d), lambda l, j: (0, 0)),
            pl.BlockSpec((1, d, tn), lambda l, j: (l, 0, j)),
            pl.BlockSpec((1, 1, tn), lambda l, j: (l, 0, j)),
        ],
        out_specs=pl.BlockSpec((1, rows, tn), lambda l, j: (l, 0, j)),
        out_shape=jax.ShapeDtypeStruct((depth, rows, n), F32),
        compiler_params=_cparams(("arbitrary", "arbitrary")),
        name="ada",
    )(c_all, w_ada, b_ada.reshape(depth, 1, n))


def _ffn_kernel(x_ref, shift_ref, scale_ref, gate_ref, gpre_ref, gpost_ref, wg_ref, wu_ref, wd_ref,
                o_ref, acc_scr, *, nj):
    j = pl.program_id(1)
    for jj in range(nj):
        @pl.when(j == jj)
        def _():
            x = x_ref[...]
            h = _norm_mod(x, gpre_ref[...], scale_ref[0, 0], shift_ref[0, 0]).astype(BF16)
            g = _dot(h, wg_ref[...])
            u = _dot(h, wu_ref[...])
            a = (g * jax.nn.sigmoid(g) * u).astype(BF16)
            d = _dot(a, wd_ref[...])
            acc = d if jj == 0 else acc_scr[...] + d
            if jj < nj - 1:
                acc_scr[...] = acc
            else:
                o_ref[...] = x + FFN_RES * gate_ref[0, 0] * _rms(acc, gpost_ref[...])


def _mod_spec(r, d, gdiv, comp):
    return pl.BlockSpec((1, 1, r, d), lambda i, *_: (i // gdiv, comp, 0, 0))


def _ffn(x, mod, comp0, g_pre, g_post, w_up, w_dn, *, tm, gdiv):
    n, d = x.shape
    ff = w_dn.shape[0]
    nj = 2 if (ff // 2) % LANES == 0 else 1
    tf = ff // nj
    r = mod.shape[2]
    return pl.pallas_call(
        functools.partial(_ffn_kernel, nj=nj),
        grid=(n // tm, nj),
        in_specs=[
            pl.BlockSpec((tm, d), lambda i, j: (i, 0)),
            _mod_spec(r, d, gdiv, comp0),
            _mod_spec(r, d, gdiv, comp0 + 1),
            _mod_spec(r, d, gdiv, comp0 + 2),
            pl.BlockSpec((1, d), lambda i, j: (0, 0)),
            pl.BlockSpec((1, d), lambda i, j: (0, 0)),
            pl.BlockSpec((d, tf), lambda i, j: (0, j)),
            pl.BlockSpec((d, tf), lambda i, j: (0, nj + j)),
            pl.BlockSpec((tf, d), lambda i, j: (j, 0)),
        ],
        out_specs=pl.BlockSpec((tm, d), lambda i, j: (i, 0)),
        out_shape=jax.ShapeDtypeStruct((n, d), F32),
        scratch_shapes=[pltpu.VMEM((tm, d), F32)],
        compiler_params=_cparams(("arbitrary", "arbitrary")),
        name="ffn",
    )(x, mod, mod, mod, g_pre, g_post, w_up, w_up, w_dn)


def _rope(x, cos, sin, first_half):
    outs = []
    for c in range(x.shape[1] // LANES):
        xs = x[:, c * LANES:(c + 1) * LANES]
        below = pltpu.roll(xs, 32, 1)
        above = pltpu.roll(xs, LANES - 32, 1)
        outs.append(xs * cos + jnp.where(first_half, above, below) * sin)
    return outs[0] if len(outs) == 1 else jnp.concatenate(outs, axis=1)


def _proj_kernel(*refs, n_prev, with_state_t):
    x_ref, shift_ref, scale_ref, gpre_ref, w_ref, bf_ref, cos_ref, sin_ref = refs[:8]
    prev_t = refs[8:8 + n_prev]
    qsb, ksb, vsb, qfx, kfx, vfx, qds, kds, vds, qidx, kidx2, misc = refs[8 + n_prev:20 + n_prev]
    state_t = refs[20 + n_prev:] if with_state_t else ()
    h = _norm_mod(x_ref[...], gpre_ref[...], scale_ref[0, 0], shift_ref[0, 0]).astype(BF16)
    cos = cos_ref[...]
    sin = sin_ref[...]
    lane = lax.broadcasted_iota(jnp.int32, (1, LANES), 1)
    first_half = (lane & (HEAD_DIM - 1)) < (HEAD_DIM // 2)
    w3 = 3 * MIX_W

    def put_kv(k, v, k_ref, v_ref, branch):
        k_ref[...] = k.astype(k_ref.dtype)
        v_ref[...] = v.astype(v_ref.dtype)
        for val, idx in ((k, 2 * branch), (v, 2 * branch + 1)):
            if state_t:
                layers = state_t[idx].shape[0]
                if n_prev:
                    state_t[idx][0:layers - 1] = prev_t[idx][...]
                state_t[idx][layers - 1] = val.T

    y = _dot(h, w_ref[:, 0:w3])
    qsb[...] = (y[:, 0:MIX_W] * QK_SCALE).astype(BF16)
    put_kv(y[:, MIX_W:2 * MIX_W], y[:, 2 * MIX_W:w3], ksb, vsb, 0)

    y = _dot(h, w_ref[:, w3:2 * w3])
    qfx[...] = (y[:, 0:MIX_W] * QK_SCALE).astype(BF16)
    put_kv(y[:, MIX_W:2 * MIX_W], y[:, 2 * MIX_W:w3], kfx, vfx, 1)

    y = _dot(h, w_ref[:, 2 * w3:3 * w3])
    qds[...] = (_rope(y[:, 0:MIX_W], cos, sin, first_half) * QK_SCALE).astype(BF16)
    put_kv(_rope(y[:, MIX_W:2 * MIX_W], cos, sin, first_half), y[:, 2 * MIX_W:w3], kds, vds, 2)

    y = _dot(h, w_ref[:, 3 * w3:PROJ_W])
    nq = IDX_HEADS * IDX_DIM
    qidx[...] = (_rope(y[:, 0:nq], cos, sin, first_half) * IDX_SCALE).astype(BF16)
    m = y[:, nq:nq + LANES]
    m_rot = _rope(m, cos, sin, first_half)
    lf = _log_sigmoid(m + bf_ref[...])
    misc[...] = jnp.where(lane < LF_LANE, m_rot,
                          jnp.where(lane < WI_LANE, lf,
                                    jnp.where(lane < WI_LANE + IDX_HEADS, m, 0.0)))
    kidx2[...] = jnp.where(lane < IDX_DIM, m_rot, pltpu.roll(m_rot, IDX_DIM, 1)).astype(BF16)


def _proj(x, mod, g_pre, w_main, bf_row, cos_t, sin_t, *, tm, gdiv, batch_t=None, prev_state_t=()):
    n, d = x.shape
    r = mod.shape[2]
    tab_tiles = cos_t.shape[0] // tm
    row = lambda w: pl.BlockSpec((tm, w), lambda i: (i, 0))
    tab = pl.BlockSpec((tm, LANES), lambda i: (i % tab_tiles, 0))
    shp = lambda w, dt: jax.ShapeDtypeStruct((n, w), dt)
    nq = IDX_HEADS * IDX_DIM
    kv_dt = F32 if batch_t is None else BF16
    out_specs = [row(MIX_W)] * 9 + [row(nq), row(LANES), row(LANES)]
    out_shape = [shp(MIX_W, BF16), shp(MIX_W, kv_dt), shp(MIX_W, kv_dt)] * 3 + [
        shp(nq, BF16), shp(LANES, BF16), shp(LANES, F32)]
    prev_specs = []
    if batch_t is not None:
        t = n // batch_t
        layers = (prev_state_t[0].shape[0] if prev_state_t else 0) + 1
        state_spec = lambda nl: pl.BlockSpec((nl, None, MIX_W, tm), lambda i: (0, i // gdiv, 0, i % gdiv))
        out_specs += [state_spec(layers)] * 6
        out_shape += [jax.ShapeDtypeStruct((layers, batch_t, MIX_W, t), F32)] * 6
        prev_specs = [state_spec(layers - 1)] * len(prev_state_t)
    return pl.pallas_call(
        functools.partial(_proj_kernel, n_prev=len(prev_state_t), with_state_t=batch_t is not None),
        grid=(n // tm,),
        in_specs=[
            row(d),
            _mod_spec(r, d, gdiv, 3),
            _mod_spec(r, d, gdiv, 4),
            pl.BlockSpec((1, d), lambda i: (0, 0)),
            pl.BlockSpec((d, PROJ_W), lambda i: (0, 0)),
            pl.BlockSpec((1, LANES), lambda i: (0, 0)),
            tab, tab,
        ] + prev_specs,
        out_specs=out_specs,
        out_shape=out_shape,
        compiler_params=_cparams(("arbitrary",)),
        name="proj",
    )(x, mod, mod, g_pre, w_main, bf_row, cos_t, sin_t, *prev_state_t)


def _merge_kernel(x_ref, shift_ref, scale_ref, gate_ref, gpre_ref, gpost_ref,
                  osb_ref, ofx_ref, ods_ref, wg_ref, wb_ref, wo_ref, o_ref):
    x = x_ref[...]
    d = x.shape[1]
    h = _norm_mod(x, gpre_ref[...], scale_ref[0, 0], shift_ref[0, 0]).astype(BF16)
    merged = None
    for nb, o_br in enumerate((osb_ref, ofx_ref, ods_ref)):
        gl = _dot(h, wg_ref[:, nb * d:(nb + 1) * d])
        y = _dot(o_br[...], wb_ref[nb])
        t = jax.nn.sigmoid(gl) * y
        merged = t if merged is None else merged + t
    out = _dot(merged.astype(BF16), wo_ref[...])
    o_ref[...] = x + gate_ref[0, 0] * _rms(out, gpost_ref[...])


def _merge(x, mod, g_pre, g_post, o_sb, o_fx, o_ds, w_gate, w_branch, w_out, *, tm, gdiv):
    n, d = x.shape
    r = mod.shape[2]
    row = lambda w: pl.BlockSpec((tm, w), lambda i: (i, 0))
    return pl.pallas_call(
        _merge_kernel,
        grid=(n // tm,),
        in_specs=[
            row(d),
            _mod_spec(r, d, gdiv, 3),
            _mod_spec(r, d, gdiv, 4),
            _mod_spec(r, d, gdiv, 5),
            pl.BlockSpec((1, d), lambda i: (0, 0)),
            pl.BlockSpec((1, d), lambda i: (0, 0)),
            row(MIX_W), row(MIX_W), row(MIX_W),
            pl.BlockSpec((d, N_BRANCH * d), lambda i: (0, 0)),
            pl.BlockSpec((N_BRANCH, MIX_W, d), lambda i: (0, 0, 0)),
            pl.BlockSpec((d, d), lambda i: (0, 0)),
        ],
        out_specs=row(d),
        out_shape=jax.ShapeDtypeStruct((n, d), F32),
        compiler_params=_cparams(("arbitrary",)),
        name="merge",
    )(x, mod, mod, mod, g_pre, g_post, o_sb, o_fx, o_ds, w_gate, w_branch, w_out)


def _aug_lanes(lane, base, ones_first, parts):
    one_lo, val_lo = (base, base + 3) if ones_first else (base + 3, base)
    out = jnp.where((lane >= one_lo) & (lane < one_lo + 3), 1.0, 0.0)
    for k, p in enumerate(parts):
        out = jnp.where(lane == val_lo + k, p, out)
    return out


def _fox_prep_kernel(q_ref, k_ref, misc_ref, qa_ref, ka_ref, carry):
    j = pl.program_id(1)

    @pl.when(j == 0)
    def _():
        carry[...] = jnp.zeros_like(carry)

    lf = misc_ref[...]
    tm = lf.shape[0]
    low = _strict_lower(tm)
    p1, p2, p3 = _split3(lf)
    g = _dot(low, p1) + _dot(low, p2) + _dot(low, p3) + carry[...]
    carry[...] += jnp.sum(lf, axis=0, keepdims=True)

    lane = lax.broadcasted_iota(jnp.int32, (1, LANES), 1)
    lo_half = lane < HEAD_DIM
    for p in range(N_PAIR):
        qp = q_ref[:, p * LANES:(p + 1) * LANES].astype(F32)
        kp = k_ref[:, p * LANES:(p + 1) * LANES]
        for a in range(2):
            hd = 2 * p + a
            gcol = g[:, LF_LANE + hd:LF_LANE + hd + 1]
            g1, g2, g3 = (t.astype(F32) for t in _split3(gcol))
            own = lo_half if a == 0 else jnp.logical_not(lo_half)
            base = HEAD_DIM if a == 0 else 0
            q_aug = jnp.where(own, qp, _aug_lanes(lane, base, True, (-g1, -g2, -g3)))
            k_aug = jnp.where(own, kp, _aug_lanes(lane, base, False, (g1, g2, g3)))
            qa_ref[:, hd * LANES:(hd + 1) * LANES] = q_aug.astype(BF16)
            ka_ref[:, hd * LANES:(hd + 1) * LANES] = k_aug.astype(BF16)


def _fox_prep(q, k, misc, *, tm):
    b, t, _ = q.shape
    nt = t // tm
    rev = lambda w: pl.BlockSpec((None, tm, w), lambda bb, j: (bb, nt - 1 - j, 0))
    return pl.pallas_call(
        _fox_prep_kernel,
        grid=(b, nt),
        in_specs=[rev(MIX_W), rev(MIX_W), rev(LANES)],
        out_specs=[rev(N_HEADS * LANES), rev(N_HEADS * LANES)],
        out_shape=[jax.ShapeDtypeStruct((b, t, N_HEADS * LANES), BF16)] * 2,
        scratch_shapes=[pltpu.VMEM((1, LANES), F32)],
        compiler_params=_cparams(("arbitrary", "arbitrary")),
        name="fox_prep",
    )(q, k, misc)


def _sb_block(qh, kblk, vblk, mask, upper, acc, c, kv_t=False):
    z = _dot(qh, kblk) if kv_t else _dot_nt(qh, kblk)
    neg_abs = lax.bitcast_convert_type(lax.bitcast_convert_type(z, jnp.int32) | INT_MIN, F32)
    ls = jnp.minimum(z, 0.0) - jnp.log(1.0 + jnp.exp(neg_abs))
    lk = ls - z
    if mask is not None:
        lk = jnp.where(mask, lk, 0.0)
    hi, lo = _split2(lk)
    sub = upper.shape[0]
    tot = []
    for j in range(z.shape[1] // sub - 1, -1, -1):
        sl = slice(j * sub, (j + 1) * sub)
        tot.append(_dot(hi[:, sl], upper) + _dot(lo[:, sl], upper) + c)
        c = c + jnp.sum(lk[:, sl], axis=-1, keepdims=True)
    tot = tot[0] if len(tot) == 1 else jnp.concatenate(tot[::-1], axis=1)
    a = jnp.exp(ls + tot)
    if mask is not None:
        a = jnp.where(mask, a, 0.0)
    a = a.astype(BF16)
    acc = acc + (_dot_nt(a, vblk) if kv_t else _dot(a, vblk))
    return acc, c


def _sb_kernel(q_ref, k_ref, v_ref, o_ref, *, tq, tk):
    i = pl.program_id(1)
    lane = lax.broadcasted_iota(jnp.int32, (1, LANES), 1)
    lo_half = lane < HEAD_DIM
    q2 = []
    for p in range(N_PAIR):
        q = q_ref[:, p * LANES:(p + 1) * LANES]
        zero = jnp.zeros_like(q)
        q2.append(jnp.concatenate([jnp.where(lo_half, q, zero), jnp.where(lo_half, zero, q)], axis=0))
    qpos = i * tq + (lax.broadcasted_iota(jnp.int32, (2 * tq, 1), 0) & (tq - 1))
    upper = _strict_upper(LANES)
    nkb = jnp.right_shift((i + 1) * tq + (tk - 1), tk.bit_length() - 1)

    def body(it, carry):
        start = pl.multiple_of((nkb - 1 - it) * tk, tk)
        mask = (start + lax.broadcasted_iota(jnp.int32, (1, tk), 1)) < qpos
        out = []
        for p in range(N_PAIR):
            kblk = k_ref[pl.ds(start, tk), p * LANES:(p + 1) * LANES].astype(BF16)
            vblk = v_ref[pl.ds(start, tk), p * LANES:(p + 1) * LANES].astype(BF16)
            out.extend(_sb_block(q2[p], kblk, vblk, mask, upper, carry[2 * p], carry[2 * p + 1]))
        return tuple(out)

    init = (jnp.zeros((2 * tq, LANES), F32), jnp.zeros((2 * tq, 1), F32)) * N_PAIR
    fin = lax.fori_loop(0, nkb, body, init)
    for p in range(N_PAIR):
        acc = fin[2 * p]
        o_ref[:, p * LANES:(p + 1) * LANES] = jnp.where(lo_half, acc[0:tq], acc[tq:2 * tq]).astype(o_ref.dtype)


def _sb_prompt(q, k, v, *, tq, tk):
    b, t, _ = q.shape
    full = pl.BlockSpec((None, t, MIX_W), lambda bb, i: (bb, 0, 0))
    rows = pl.BlockSpec((None, tq, MIX_W), lambda bb, i: (bb, i, 0))
    return pl.pallas_call(
        functools.partial(_sb_kernel, tq=tq, tk=tk),
        grid=(b, t // tq),
        in_specs=[rows, full, full],
        out_specs=rows,
        out_shape=jax.ShapeDtypeStruct((b, t, MIX_W), BF16),
        compiler_params=_cparams(("arbitrary", "arbitrary")),
        name="sb_prompt",
    )(q, k, v)


def _softmax_step(s, vblk, m, l, acc, v_t=False):
    m_new = jnp.maximum(m, jnp.max(s, axis=-1, keepdims=True))
    alpha = jnp.exp(m - m_new)
    p = jnp.exp(s - m_new)
    l = alpha * l + jnp.sum(p, axis=-1, keepdims=True)
    p = p.astype(BF16)
    acc = alpha * acc + (_dot_nt(p, vblk) if v_t else _dot(p, vblk))
    return m_new, l, acc


def _fox_kernel(qa_ref, ka_ref, v_ref, o_ref, *, tq, tk):
    i = pl.program_id(1)
    qpos = i * tq + lax.broadcasted_iota(jnp.int32, (tq, 1), 0)
    nkb = jnp.right_shift((i + 1) * tq + (tk - 1), tk.bit_length() - 1)

    def body(kb, carry):
        start = pl.multiple_of(kb * tk, tk)
        kpos = start + lax.broadcasted_iota(jnp.int32, (1, tk), 1)
        mask = kpos <= qpos
        out = []
        for p in range(N_PAIR):
            vblk = v_ref[pl.ds(start, tk), p * LANES:(p + 1) * LANES].astype(BF16)
            for a in range(2):
                hd = 2 * p + a
                m, l, acc = carry[3 * hd:3 * hd + 3]
                s = _dot_nt(qa_ref[:, hd * LANES:(hd + 1) * LANES],
                            ka_ref[pl.ds(start, tk), hd * LANES:(hd + 1) * LANES])
                s = jnp.where(mask, s, -jnp.inf)
                out.extend(_softmax_step(s, vblk, m, l, acc))
        return tuple(out)

    init = (jnp.full((tq, 1), NEG_BIG, F32), jnp.zeros((tq, 1), F32), jnp.zeros((tq, LANES), F32)) * N_HEADS
    fin = lax.fori_loop(0, nkb, body, init)
    lane = lax.broadcasted_iota(jnp.int32, (1, LANES), 1)
    for p in range(N_PAIR):
        o0 = fin[6 * p + 2] / fin[6 * p + 1]
        o1 = fin[6 * p + 5] / fin[6 * p + 4]
        o_ref[:, p * LANES:(p + 1) * LANES] = jnp.where(lane < HEAD_DIM, o0, o1).astype(o_ref.dtype)


def _fox_prompt(qa, ka, v, *, tq, tk):
    b, t, _ = v.shape
    return pl.pallas_call(
        functools.partial(_fox_kernel, tq=tq, tk=tk),
        grid=(b, t // tq),
        in_specs=[
            pl.BlockSpec((None, tq, N_HEADS * LANES), lambda bb, i: (bb, i, 0)),
            pl.BlockSpec((None, t, N_HEADS * LANES), lambda bb, i: (bb, 0, 0)),
            pl.BlockSpec((None, t, MIX_W), lambda bb, i: (bb, 0, 0)),
        ],
        out_specs=pl.BlockSpec((None, tq, MIX_W), lambda bb, i: (bb, i, 0)),
        out_shape=jax.ShapeDtypeStruct((b, t, MIX_W), BF16),
        compiler_params=_cparams(("arbitrary", "arbitrary")),
        name="fox_prompt",
    )(qa, ka, v)


def _sort_key(x):
    bits = lax.bitcast_convert_type(x, jnp.int32)
    key = jnp.where(bits < 0, bits ^ jnp.int32(0x7FFFFFFF), bits)
    return jnp.where(x == 0.0, 0, key)


def _count(pred_parts):
    tot = None
    for preds in pred_parts:
        ind = jnp.where(preds[-1], 1.0, 0.0)
        for p in preds[-2::-1]:
            ind = jnp.where(p, ind, 0.0)
        c = jnp.sum(ind, axis=-1, keepdims=True)
        tot = c if tot is None else tot + c
    return tot


def _kth_threshold(key_gets, rows, kf):
    def step(it, thr):
        cand = thr + jnp.left_shift(jnp.int32(1), 31 - it)
        cnt = _count([(kg() >= cand,) for kg in key_gets])
        return jnp.where(cnt >= kf, cand, thr)

    return lax.fori_loop(0, 32, step, jnp.full((rows, 1), INT_MIN, jnp.int32), unroll=SEARCH_UNROLL)


def _topk_select(key_gets, pos_list, rows, topk, idx_bits):
    kf = float(topk)
    thr = _kth_threshold(key_gets, rows, kf)
    need = kf - _count([(kg() > thr,) for kg in key_gets])
    n_eq = _count([(kg() == thr,) for kg in key_gets])

    def ibody(it, bound):
        cand = bound + jnp.left_shift(jnp.int32(1), idx_bits - 1 - it)
        cnt = _count([(kg() == thr, pos < cand) for kg, pos in zip(key_gets, pos_list)])
        return jnp.where(cnt <= need, cand, bound)

    some_partial = jnp.max(jnp.where(n_eq > need, 1.0, 0.0)) > 0.0
    bound = lax.cond(
        some_partial,
        lambda: lax.fori_loop(0, idx_bits, ibody, jnp.zeros((rows, 1), jnp.int32)),
        lambda: jnp.full((rows, 1), 1 << idx_bits, jnp.int32))
    tie = jnp.where(thr > KEY_NEG_INF, 0.0, -jnp.inf)
    out = []
    for kg, pos in zip(key_gets, pos_list):
        key = kg()
        at_thr = jnp.where(pos < bound, tie, -jnp.inf)
        out.append(jnp.where(key > thr, 0.0, jnp.where(key == thr, at_thr, -jnp.inf)))
    return out


def _dsa_body(i, n_keys, qidx_ref, kidx2_ref, miscq_ref, q_ref, k_ref, v_ref, o_ref, key_scr, bias_scr, *, tq, topk):
    lane = lax.broadcasted_iota(jnp.int32, (1, LANES), 1)
    lo_half = lane < HEAD_DIM
    kidx2 = kidx2_ref[0:n_keys, :]
    wq = miscq_ref[...]
    isc = jnp.zeros((tq, n_keys), F32)
    for h in range(IDX_HEADS):
        qp = qidx_ref[:, (h // 2) * LANES:(h // 2 + 1) * LANES]
        own = lo_half if h % 2 == 0 else jnp.logical_not(lo_half)
        qh = jnp.where(own, qp, jnp.zeros_like(qp))
        s = _dot_nt(qh, kidx2)
        isc = isc + jnp.maximum(s, 0.0) * wq[:, WI_LANE + h:WI_LANE + h + 1]
    isc = isc * IDX_HEAD_SCALE
    qpos = i * tq + lax.broadcasted_iota(jnp.int32, (tq, 1), 0)
    kpos = lax.broadcasted_iota(jnp.int32, (1, n_keys), 1)
    allowed = jnp.right_shift(kpos, CHUNK_SHIFT) <= jnp.right_shift(qpos, CHUNK_SHIFT)
    isc = jnp.where(allowed, isc, -jnp.inf)
    key_scr[:, 0:n_keys] = _sort_key(isc)
    (bias,) = _topk_select([lambda: key_scr[:, 0:n_keys]], [kpos], tq, topk, int(n_keys).bit_length())
    bias_scr[:, 0:n_keys] = bias

    for p in range(N_PAIR):
        qp = q_ref[:, p * LANES:(p + 1) * LANES]
        kp = k_ref[0:n_keys, p * LANES:(p + 1) * LANES].astype(BF16)
        vp = v_ref[0:n_keys, p * LANES:(p + 1) * LANES].astype(BF16)
        outs = []
        for a in range(2):
            own = lo_half if a == 0 else jnp.logical_not(lo_half)
            qh = jnp.where(own, qp, jnp.zeros_like(qp))
            s = _dot_nt(qh, kp) + bias_scr[:, 0:n_keys]
            m = jnp.max(s, axis=-1, keepdims=True)
            e = jnp.exp(s - m)
            l = jnp.sum(e, axis=-1, keepdims=True)
            outs.append(_dot(e.astype(BF16), vp) / l)
        o_ref[:, p * LANES:(p + 1) * LANES] = jnp.where(lo_half, outs[0], outs[1]).astype(o_ref.dtype)


def _dsa_kernel(*refs, tq, topk, first_block):
    n_keys = refs[4].shape[0]
    _dsa_body(first_block + pl.program_id(1), n_keys, *refs, tq=tq, topk=topk)


def _dsa_prompt(qidx, kidx2, misc, q, k, v, *, tq):
    b, t, _ = q.shape
    topk = min(TOPK_MAX, t // 4)
    n_groups = 1
    for cand in (8, 4, 2):
        if (t // cand) % tq == 0 and t // cand >= topk:
            n_groups = cand
            break
    glen = t // n_groups
    outs = []
    for g in range(n_groups):
        n_keys = (g + 1) * glen
        first = g * (glen // tq)
        qrow = lambda w: pl.BlockSpec((None, tq, w), lambda bb, i, first=first: (bb, first + i, 0))
        keys = lambda w: pl.BlockSpec((None, n_keys, w), lambda bb, i: (bb, 0, 0))
        outs.append(pl.pallas_call(
            functools.partial(_dsa_kernel, tq=tq, topk=topk, first_block=first),
            grid=(b, glen // tq),
            in_specs=[qrow(IDX_HEADS * IDX_DIM), keys(LANES), qrow(LANES), qrow(MIX_W), keys(MIX_W), keys(MIX_W)],
            out_specs=pl.BlockSpec((None, tq, MIX_W), lambda bb, i: (bb, i, 0)),
            out_shape=jax.ShapeDtypeStruct((b, glen, MIX_W), BF16),
            scratch_shapes=[pltpu.VMEM((tq, n_keys), jnp.int32), pltpu.VMEM((tq, n_keys), F32)],
            compiler_params=_cparams(("arbitrary", "arbitrary")),
            name="dsa_prompt",
        )(qidx, kidx2, misc, q, k, v))
    return outs[0] if n_groups == 1 else jnp.concatenate(outs, axis=1)


def _head_rows(x, n_rep):
    return jnp.concatenate([x] * n_rep, axis=0)


def _block_diag_q(q, tq):
    rows = N_HEADS * tq
    rhead = jnp.right_shift(lax.broadcasted_iota(jnp.int32, (rows, 1), 0), tq.bit_length() - 1)
    lhead = jnp.right_shift(lax.broadcasted_iota(jnp.int32, (1, MIX_W), 1), HEAD_DIM.bit_length() - 1)
    qq = _head_rows(q, N_HEADS)
    return jnp.where(rhead == lhead, qq, jnp.zeros_like(qq)), rhead == lhead


def _diag_out(o_bd, diag, tq):
    o = jnp.where(diag, o_bd, 0.0)
    out = o[0:tq]
    for h in range(1, N_HEADS):
        out = out + o[h * tq:(h + 1) * tq]
    return out


def _pad_rows(x, rows):
    return jnp.concatenate([x, jnp.zeros((rows - x.shape[0], x.shape[1]), x.dtype)], axis=0)


def _sb_sample_kernel(q_ref, kn_ref, vn_ref, kp_ref, vp_ref, o_ref, *, tk):
    tq = q_ref.shape[0]
    past = kp_ref.shape[1]
    rows = N_HEADS * tq
    qbd, diag = _block_diag_q(q_ref[...], tq)
    trow = lax.broadcasted_iota(jnp.int32, (rows, 1), 0) & (tq - 1)
    kn = _pad_rows(kn_ref[...], LANES).astype(BF16)
    vn = _pad_rows(vn_ref[...], LANES).astype(BF16)
    mask_n = lax.broadcasted_iota(jnp.int32, (1, LANES), 1) < trow
    acc = jnp.zeros((rows, MIX_W), F32)
    c = jnp.zeros((rows, 1), F32)
    upper = _strict_upper(LANES)
    acc, c = _sb_block(qbd, kn, vn, mask_n, upper, acc, c)
    for kb in range(past // tk - 1, -1, -1):
        kblk = kp_ref[:, kb * tk:(kb + 1) * tk].astype(BF16)
        vblk = vp_ref[:, kb * tk:(kb + 1) * tk].astype(BF16)
        acc, c = _sb_block(qbd, kblk, vblk, None, upper, acc, c, kv_t=True)
    o_ref[...] = _diag_out(acc, diag, tq).astype(o_ref.dtype)


def _fox_sample_kernel(q_ref, kn_ref, vn_ref, miscn_ref, kp_ref, vp_ref, lfp_ref, o_ref, *, tk):
    tq = q_ref.shape[0]
    past = kp_ref.shape[1]
    rows = N_HEADS * tq
    hp = lfp_ref.shape[0]
    qbd, diag = _block_diag_q(q_ref[...], tq)
    trow = lax.broadcasted_iota(jnp.int32, (rows, 1), 0) & (tq - 1)
    lane = lax.broadcasted_iota(jnp.int32, (1, LANES), 1)

    misc_n = _pad_rows(miscn_ref[...], LANES)
    pick = jnp.where(lax.broadcasted_iota(jnp.int32, (hp, LANES), 1)
                     == LF_LANE + lax.broadcasted_iota(jnp.int32, (hp, LANES), 0), 1.0, 0.0).astype(BF16)
    lf_new = sum(_dot_nt(pick, part) for part in _split3(misc_n))
    up_n = _strict_upper(LANES)
    g_new = sum(_dot(part, up_n) for part in _split3(lf_new))
    tot_new = jnp.sum(lf_new, axis=-1, keepdims=True)

    def expand(g):
        return jnp.concatenate([jnp.broadcast_to(g[h:h + 1], (tq, g.shape[1])) for h in range(N_HEADS)], axis=0)

    g_new_x = expand(g_new)
    g_q = jnp.sum(jnp.where(lane == trow, g_new_x, 0.0), axis=-1, keepdims=True)

    kn = _pad_rows(kn_ref[...], LANES).astype(BF16)
    vn = _pad_rows(vn_ref[...], LANES).astype(BF16)
    s_n = _dot_nt(qbd, kn) + g_new_x - g_q
    s_n = jnp.where(lane <= trow, s_n, -jnp.inf)
    m, l, acc = _softmax_step(s_n, vn, jnp.full((rows, 1), NEG_BIG, F32), jnp.zeros((rows, 1), F32),
                              jnp.zeros((rows, MIX_W), F32))
    sub = min(2 * LANES, tk)
    upper = _strict_upper(sub)
    carry = tot_new
    for kb in range(past // tk - 1, -1, -1):
        g_parts = []
        for j in range(tk // sub - 1, -1, -1):
            lf_sub = lfp_ref[:, kb * tk + j * sub:kb * tk + (j + 1) * sub]
            g_parts.append(sum(_dot(part, upper) for part in _split3(lf_sub)) + carry)
            carry = carry + jnp.sum(lf_sub, axis=-1, keepdims=True)
        g_blk = g_parts[0] if len(g_parts) == 1 else jnp.concatenate(g_parts[::-1], axis=1)
        kblk = kp_ref[:, kb * tk:(kb + 1) * tk].astype(BF16)
        vblk = vp_ref[:, kb * tk:(kb + 1) * tk].astype(BF16)
        s = _dot(qbd, kblk) + expand(g_blk) - g_q
        m, l, acc = _softmax_step(s, vblk, m, l, acc, v_t=True)
    o_ref[...] = _diag_out(acc / l, diag, tq).astype(o_ref.dtype)


def _dsa_sample_select_kernel(qidx_ref, miscn_ref, kip_ref, biasp_ref, biasn_ref, keyp_scr, keyn_scr, *, topk, pos0):
    nb, tq, _ = qidx_ref.shape
    past = kip_ref.shape[2]
    lane = lax.broadcasted_iota(jnp.int32, (1, LANES), 1)
    qchunk = jnp.right_shift(pos0 + lax.broadcasted_iota(jnp.int32, (tq, 1), 0), CHUNK_SHIFT)
    ppos = lax.broadcasted_iota(jnp.int32, (1, past), 1)
    npos = pos0 + lane
    ok_p = jnp.right_shift(ppos, CHUNK_SHIFT) <= qchunk
    ok_n = (jnp.right_shift(npos, CHUNK_SHIFT) <= qchunk) & (lane < tq)
    for bi in range(nb):
        qi = qidx_ref[bi].astype(F32)
        qs = jnp.concatenate([qi[:, h * IDX_DIM:(h + 1) * IDX_DIM] for h in range(IDX_HEADS)], axis=0).astype(BF16)
        wq = miscn_ref[bi]
        ws = jnp.concatenate([wq[:, WI_LANE + h:WI_LANE + h + 1] for h in range(IDX_HEADS)], axis=0)
        kip = kip_ref[bi].astype(BF16)
        kin = _pad_rows(wq[:, 0:IDX_DIM], LANES).astype(BF16)

        def head_sum(s):
            s = jnp.maximum(s, 0.0) * ws
            out = s[0:tq]
            for h in range(1, IDX_HEADS):
                out = out + s[h * tq:(h + 1) * tq]
            return out * IDX_HEAD_SCALE

        isc_p = jnp.where(ok_p, head_sum(_dot(qs, kip)), -jnp.inf)
        isc_n = jnp.where(ok_n, head_sum(_dot_nt(qs, kin)), -jnp.inf)
        keyp_scr[bi * tq:(bi + 1) * tq, :] = _sort_key(isc_p)
        keyn_scr[bi * tq:(bi + 1) * tq, :] = jnp.where(lane < tq, _sort_key(isc_n), INT_MIN)
    bias_p, bias_n = _topk_select([lambda: keyp_scr[...], lambda: keyn_scr[...]], [ppos, npos], nb * tq, topk,
                                  int(past + LANES).bit_length())
    for bi in range(nb):
        biasp_ref[bi] = bias_p[bi * tq:(bi + 1) * tq]
        biasn_ref[bi] = bias_n[bi * tq:(bi + 1) * tq]


def _dsa_sample_kernel(biasp_ref, biasn_ref, q_ref, kn_ref, vn_ref, kp_ref, vp_ref, o_ref):
    tq = q_ref.shape[0]
    bias_p = _head_rows(biasp_ref[...], N_HEADS)
    bias_n = _head_rows(biasn_ref[...], N_HEADS)

    qbd, diag = _block_diag_q(q_ref[...], tq)
    kn = _pad_rows(kn_ref[...], LANES).astype(BF16)
    vn = _pad_rows(vn_ref[...], LANES).astype(BF16)
    s_p = _dot(qbd, kp_ref[...].astype(BF16)) + bias_p
    s_n = _dot_nt(qbd, kn) + bias_n
    m = jnp.maximum(jnp.max(s_p, axis=-1, keepdims=True), jnp.max(s_n, axis=-1, keepdims=True))
    e_p = jnp.exp(s_p - m)
    e_n = jnp.exp(s_n - m)
    l = jnp.sum(e_p, axis=-1, keepdims=True) + jnp.sum(e_n, axis=-1, keepdims=True)
    o = (_dot_nt(e_p.astype(BF16), vp_ref[...].astype(BF16)) + _dot(e_n.astype(BF16), vn)) / l
    o_ref[...] = _diag_out(o, diag, tq).astype(o_ref.dtype)


def _sample_attention(layer, q_sb, k_sb, v_sb, q_fx, k_fx, v_fx, q_ds, k_ds, v_ds, qidx, misc,
                      c_sb_k, c_sb_v, c_fx_k, c_fx_v, lf_t, c_ds_k, c_ds_v, c_kidx):
    b, tq, _ = q_sb.shape
    past = c_sb_k.shape[3]
    tk = min(SAMPLE_KEYS, past)
    new = lambda w: pl.BlockSpec((None, tq, w), lambda bb: (bb, 0, 0))
    cache = lambda w: pl.BlockSpec((None, None, w, past), lambda bb: (layer, bb, 0, 0))
    out = jax.ShapeDtypeStruct((b, tq, MIX_W), BF16)
    cp = _cparams(("arbitrary",))
    o_sb = pl.pallas_call(
        functools.partial(_sb_sample_kernel, tk=tk), grid=(b,),
        in_specs=[new(MIX_W), new(MIX_W), new(MIX_W), cache(MIX_W), cache(MIX_W)],
        out_specs=new(MIX_W), out_shape=out, compiler_params=cp, name="sb_sample",
    )(q_sb, k_sb, v_sb, c_sb_k, c_sb_v)
    hp = lf_t.shape[2]
    o_fx = pl.pallas_call(
        functools.partial(_fox_sample_kernel, tk=tk), grid=(b,),
        in_specs=[new(MIX_W), new(MIX_W), new(MIX_W), new(LANES), cache(MIX_W), cache(MIX_W),
                  cache(hp)],
        out_specs=new(MIX_W), out_shape=out, compiler_params=cp, name="fox_sample",
    )(q_fx, k_fx, v_fx, misc, c_fx_k, c_fx_v, lf_t)
    topk = min(TOPK_MAX, (past + tq) // 4)
    nb = SELECT_BATCH if b % SELECT_BATCH == 0 else 1
    grp = lambda w: pl.BlockSpec((nb, tq, w), lambda g: (g, 0, 0))
    bias_p, bias_n = pl.pallas_call(
        functools.partial(_dsa_sample_select_kernel, topk=topk, pos0=past), grid=(b // nb,),
        in_specs=[grp(IDX_HEADS * IDX_DIM), grp(LANES),
                  pl.BlockSpec((None, nb, IDX_DIM, past), lambda g: (layer, g, 0, 0))],
        out_specs=[grp(past), grp(LANES)],
        out_shape=[jax.ShapeDtypeStruct((b, tq, past), F32), jax.ShapeDtypeStruct((b, tq, LANES), F32)],
        scratch_shapes=[pltpu.VMEM((nb * tq, past), jnp.int32), pltpu.VMEM((nb * tq, LANES), jnp.int32)],
        compiler_params=cp, name="dsa_sample_select",
    )(qidx, misc, c_kidx)
    o_ds = pl.pallas_call(
        _dsa_sample_kernel, grid=(b,),
        in_specs=[new(past), new(LANES), new(MIX_W), new(MIX_W), new(MIX_W), cache(MIX_W), cache(MIX_W)],
        out_specs=new(MIX_W), out_shape=out, compiler_params=cp, name="dsa_sample",
    )(bias_p, bias_n, q_ds, k_ds, v_ds, c_ds_k, c_ds_v)
    return o_sb, o_fx, o_ds


def _rope_tables(pos):
    half = HEAD_DIM // 2
    inv_freq = ROPE_THETA ** (-jnp.arange(half, dtype=F32) / half)
    ang = pos.astype(F32)[:, None] * inv_freq[None, :]
    cos = jnp.cos(ang)
    sin = jnp.sin(ang)
    cos_t = jnp.concatenate([cos, cos, cos, cos], axis=1)
    sin_t = jnp.concatenate([-sin, sin, -sin, sin], axis=1)
    return cos_t, sin_t


def _prep_weights(w_in, b_forget, ffn_w_in, ffn_w_out, w_branch, w_out):
    w3 = 3 * MIX_W
    o = np.cumsum([0, w3, w3, N_HEADS, w3, IDX_HEADS * IDX_DIM, IDX_DIM, IDX_HEADS]).tolist()
    sb, fx, fl, ds, qi, ki, wi = (w_in[:, :, o[k]:o[k + 1]] for k in range(7))
    pad = jnp.zeros(w_in.shape[:2] + (LANES - IDX_DIM - N_HEADS - IDX_HEADS,), w_in.dtype)
    w_main = jnp.concatenate([sb, fx, ds, qi, ki, fl, wi, pad], axis=2).astype(BF16)
    w_gate = w_in[:, :, o[7]:].astype(BF16)
    depth = w_in.shape[0]
    bf_row = jnp.zeros((depth, 1, LANES), F32).at[:, 0, LF_LANE:LF_LANE + N_HEADS].set(b_forget)
    return (w_main, w_gate, bf_row, ffn_w_in.astype(BF16), ffn_w_out.astype(BF16),
            w_branch.astype(BF16), w_out.astype(BF16))


def _trunk(x3, mod_l, gdiv, tm, pos, caches, norm_g, weights, *, tq_attn):
    w_main, w_gate, bf_row, w_up, w_dn, w_branch, w_out = weights
    b, t, d = x3.shape
    n = b * t
    depth = w_main.shape[0]
    x = x3.reshape(n, d)
    cos_t, sin_t = _rope_tables(pos)
    if caches is not None:
        cos_t = jnp.tile(cos_t, (tm // t, 1))
        sin_t = jnp.tile(sin_t, (tm // t, 1))
        lf_t = jnp.swapaxes(caches[4], 2, 3)
        lf_t = jnp.pad(lf_t, ((0, 0), (0, 0), (0, 16 - lf_t.shape[2]), (0, 0)))
    states = []
    prev_t = ()
    for l in range(depth):
        g = lambda k: norm_g[l, k][None, :]
        mod = mod_l[l]
        x = _ffn(x, mod, 0, g(0), g(1), w_up[l, 0], w_dn[l, 0], tm=tm, gdiv=gdiv)
        (q_sb, k_sb, v_sb, q_fx, k_fx, v_fx, q_ds, k_ds, v_ds, qidx, kidx2, misc, *state_t) = _proj(
            x, mod, g(2), w_main[l], bf_row[l], cos_t, sin_t, tm=tm, gdiv=gdiv,
            batch_t=b if caches is None else None, prev_state_t=prev_t)
        r3 = lambda a: a.reshape(b, t, a.shape[-1])
        if caches is None:
            qa, ka = _fox_prep(r3(q_fx), r3(k_fx), r3(misc), tm=min(256, t))
            o_sb = _sb_prompt(r3(q_sb), r3(k_sb), r3(v_sb), tq=min(512, t), tk=min(512, t))
            o_fx = _fox_prompt(qa, ka, r3(v_fx), tq=min(512, t), tk=min(512, t))
            o_ds = _dsa_prompt(r3(qidx), r3(kidx2), r3(misc), r3(q_ds), r3(k_ds), r3(v_ds), tq=tq_attn)
        else:
            o_sb, o_fx, o_ds = _sample_attention(
                l, r3(q_sb), r3(k_sb), r3(v_sb), r3(q_fx), r3(k_fx), r3(v_fx), r3(q_ds), r3(k_ds), r3(v_ds),
                r3(qidx), r3(misc), caches[0], caches[1], caches[2], caches[3], lf_t,
                caches[5], caches[6], caches[7])
        o2 = lambda a: a.reshape(n, MIX_W)
        x = _merge(x, mod, g(2), g(3), o2(o_sb), o2(o_fx), o2(o_ds), w_gate[l], w_branch[l], w_out[l],
                   tm=tm, gdiv=gdiv)
        x = _ffn(x, mod, 6, g(4), g(5), w_up[l, 1], w_dn[l, 1], tm=tm, gdiv=gdiv)
        prev_t = tuple(state_t)
        hd = lambda a: a.reshape(b, t, N_HEADS, HEAD_DIM)
        kv = [None] * 6 if state_t else [hd(a) for a in (k_sb, v_sb, k_fx, v_fx, k_ds, v_ds)]
        states.append((kv[0], kv[1], kv[2], kv[3],
                       misc[:, LF_LANE:LF_LANE + N_HEADS].reshape(b, t, N_HEADS),
                       kv[4], kv[5], misc[:, 0:IDX_DIM].reshape(b, t, IDX_DIM)))
    stacked = [None if s[0] is None else jnp.stack(s, axis=0) for s in zip(*states)]
    if prev_t:
        hd_t = lambda a: jnp.transpose(a.reshape(depth, b, N_HEADS, HEAD_DIM, t), (0, 1, 4, 2, 3))
        for slot, a in zip((0, 1, 2, 3, 5, 6), prev_t):
            stacked[slot] = hd_t(a)
    return x.reshape(b, t, d), tuple(stacked)


def kernel(x_prompt, x_sample, cache_sb_k, cache_sb_v, cache_fox_k, cache_fox_v, cache_fox_logf, cache_dsa_k, cache_dsa_v, cache_dsa_kidx, c_prompt, c_sample, norm_g, w_ada, b_ada, ffn_w_in, ffn_w_out, w_in, b_forget, w_branch, w_out):
    bp, tp, d = x_prompt.shape
    bs, ts, _ = x_sample.shape
    depth = w_in.shape[0]
    past = cache_sb_k.shape[2]
    mix = lambda a: jnp.transpose(a, (0, 1, 3, 4, 2)).reshape(a.shape[:2] + (MIX_W, a.shape[2]))
    caches = (mix(cache_sb_k), mix(cache_sb_v), mix(cache_fox_k), mix(cache_fox_v), cache_fox_logf,
              mix(cache_dsa_k), mix(cache_dsa_v), jnp.swapaxes(cache_dsa_kidx, 2, 3))
    weights = _prep_weights(w_in, b_forget, ffn_w_in, ffn_w_out, w_branch, w_out)

    rows = bp + bs
    rows_pad = -(-rows // 8) * 8
    c_all = jnp.concatenate([c_prompt, c_sample, jnp.zeros((rows_pad - rows, d), F32)], axis=0)
    mod = _ada(c_all, w_ada, b_ada)
    mod_p = [mod[l, :bp].reshape(bp, N_MOD, 1, d) for l in range(depth)]
    ns = bs * ts
    mod_s = [jnp.repeat(mod[l, bp:rows].reshape(bs, N_MOD, d), ts, axis=0)
             .reshape(1, ns, N_MOD, d).transpose(0, 2, 1, 3) for l in range(depth)]

    tm_p = min(512, tp)
    pos_p = jnp.arange(tp, dtype=jnp.int32)
    pos_s = past + jnp.arange(ts, dtype=jnp.int32)
    y_p, st_p = _trunk(x_prompt, mod_p, tp // tm_p, tm_p, pos_p, None, norm_g, weights, tq_attn=min(256, tp))
    y_s, st_s = _trunk(x_sample, mod_s, 1, ns, pos_s, caches, norm_g, weights, tq_attn=ts)
    return (y_p, y_s) + st_p + st_s
```
